```python
import jax, jax.numpy as jnp
from jax import lax
import numpy as np

D_MODEL = 4096
BATCH = 1
SEQ = 16384
DEPTH = 2
DEC_BATCH = 2
DEC_SEQ = 8192
PAST_LEN = 128

HEAD_DIM = 128
MIX_WIDTH = D_MODEL
N_MIX_HEADS = MIX_WIDTH // HEAD_DIM
MLSTM_HEADS = N_MIX_HEADS // 4
ATTN_HEADS = 3 * N_MIX_HEADS // 8
GLA_HEADS = N_MIX_HEADS - MLSTM_HEADS - ATTN_HEADS
MLSTM_WIDTH = MLSTM_HEADS * HEAD_DIM
ATTN_WIDTH = ATTN_HEADS * HEAD_DIM
GLA_HEAD_K = HEAD_DIM // 2
GLA_HEAD_V = HEAD_DIM
GLA_KEY_WIDTH = GLA_HEADS * GLA_HEAD_K
GLA_WIDTH = GLA_HEADS * GLA_HEAD_V
GLA_RANK = 16
GLA_TAU = 16.0
MLSTM_CHUNK = 64
GLA_CHUNK = 64
ATTN_Q_BLOCK = 128
ATTN_PAIRS = ((128, 1), (512, 4), (2048, 16))
ROPE_THETA = 10000.0
D_FF = 11008
CONV_W = 3
CONV_HALF = CONV_W // 2
PLE_DIM = 256
LN_EPS = 1e-5
HEAD_NORM_EPS = 1e-6
DEEPNORM_ALPHA = (2 * DEPTH) ** 0.25
DEEPNORM_BETA = (8 * DEPTH) ** -0.25
IN_SIZES = (MLSTM_WIDTH,) * 4 + (4 * MLSTM_HEADS,) + (ATTN_WIDTH,) * 3 + (GLA_KEY_WIDTH,) * 2 + (GLA_WIDTH,) * 2 + (GLA_RANK,) * 2
IN_COLS = 4 * MLSTM_WIDTH + 4 * MLSTM_HEADS + 3 * ATTN_WIDTH + 2 * GLA_KEY_WIDTH + 2 * GLA_WIDTH + 2 * GLA_RANK

kernel_name = 'hymba_bidir_mlstm_dilattn_gla_encoder'


def layer_norm(x, g, b):
    xf = x.astype(jnp.float32)
    mu = jnp.mean(xf, axis=-1, keepdims=True)
    var = jnp.mean(jnp.square(xf - mu), axis=-1, keepdims=True)
    return ((xf - mu) * lax.rsqrt(var + LN_EPS) * g + b).astype(x.dtype)


def to_bhsd(t, n_heads):
    B, S, _ = t.shape
    return t.reshape(B, S, n_heads, -1).transpose(0, 2, 1, 3)


def rev(t):
    return jnp.flip(t, axis=2)


def rope(t, pos):
    half = t.shape[-1] // 2
    inv = ROPE_THETA ** (-jnp.arange(half, dtype=jnp.float32) / half)
    ang = pos[:, None] * inv[None, :]
    cos = jnp.cos(ang)[None, :, None, :]
    sin = jnp.sin(ang)[None, :, None, :]
    t1, t2 = t[..., :half], t[..., half:]
    return jnp.concatenate([t1 * cos - t2 * sin, t1 * sin + t2 * cos], axis=-1)


def mlstm_scan(q, k, v, i_pre, f_pre):
    B, H, S, D = q.shape
    L = MLSTM_CHUNK
    N = S // L
    q = q.reshape(B, H, N, L, D)
    k = k.reshape(B, H, N, L, D)
    v = v.reshape(B, H, N, L, D)
    ip = i_pre.reshape(B, H, N, L)
    lf = jax.nn.log_sigmoid(f_pre).reshape(B, H, N, L)
    b = jnp.cumsum(lf, axis=-1)
    g = b[..., -1]
    a = g[..., None] - b + ip
    m_loc = jnp.max(a, axis=-1)
    wa = jnp.exp(a - m_loc[..., None])
    C_loc = jnp.einsum('bhnl,bhnld,bhnle->bhnde', wa, v, k)
    n_loc = jnp.einsum('bhnl,bhnle->bhne', wa, k)

    def step(carry, inp):
        C, n, m = carry
        gc, ml, Cl, nl = inp
        m_new = jnp.maximum(gc + m, ml)
        s_old = jnp.exp(gc + m - m_new)
        s_new = jnp.exp(ml - m_new)
        C_new = s_old[..., None, None] * C + s_new[..., None, None] * Cl
        n_new = s_old[..., None] * n + s_new[..., None] * nl
        return (C_new, n_new, m_new), (C, n, m)

    init = (jnp.zeros((B, H, D, D), q.dtype), jnp.zeros((B, H, D), q.dtype), jnp.zeros((B, H), q.dtype))
    xs = (jnp.moveaxis(g, 2, 0), jnp.moveaxis(m_loc, 2, 0), jnp.moveaxis(C_loc, 2, 0), jnp.moveaxis(n_loc, 2, 0))
    _, (C_prev, n_prev, m_prev) = lax.scan(step, init, xs)
    C_prev = jnp.moveaxis(C_prev, 0, 2)
    n_prev = jnp.moveaxis(n_prev, 0, 2)
    m_prev = jnp.moveaxis(m_prev, 0, 2)

    tril = jnp.tril(jnp.ones((L, L), dtype=bool))
    Dm = b[..., :, None] - b[..., None, :] + ip[..., None, :]
    Dm = jnp.where(tril, Dm, -jnp.inf)
    inter = b + m_prev[..., None]
    m_t = jnp.maximum(inter, jnp.max(Dm, axis=-1))
    w_inter = jnp.exp(inter - m_t)
    P = jnp.exp(Dm - m_t[..., None]) * jnp.einsum('bhnld,bhnsd->bhnls', q, k)
    num = w_inter[..., None] * jnp.einsum('bhnde,bhnle->bhnld', C_prev, q) + jnp.einsum('bhnls,bhnsd->bhnld', P, v)
    nq = w_inter * jnp.einsum('bhne,bhnle->bhnl', n_prev, q) + jnp.sum(P, axis=-1)
    h = num / jnp.maximum(jnp.abs(nq), jnp.exp(-m_t))[..., None]
    return h.reshape(B, H, S, D)


def mlstm_mixer(q, k, v, o, gate_pre, gate_b, norm_g):
    B, S, _ = q.shape
    H, D = MLSTM_HEADS, HEAD_DIM
    qh = to_bhsd(q, H)
    kh = to_bhsd(k, H) * (D ** -0.5)
    vh = to_bhsd(v, H)
    pre = (gate_pre.reshape(B, S, 4, H) + gate_b).transpose(2, 0, 3, 1)
    h_f = mlstm_scan(qh, kh, vh, pre[0], pre[1])
    h_b = rev(mlstm_scan(rev(qh), rev(kh), rev(vh), rev(pre[2]), rev(pre[3])))
    h = (h_f + h_b).transpose(0, 2, 1, 3)
    mu = jnp.mean(h, axis=-1, keepdims=True)
    var = jnp.mean(jnp.square(h - mu), axis=-1, keepdims=True)
    hn = (h - mu) * lax.rsqrt(var + HEAD_NORM_EPS) * norm_g.reshape(H, D)
    return (jax.nn.sigmoid(o).reshape(B, S, H, D) * hn).reshape(B, S, H * D)


def dilated_attention(q, k, v):
    B, S, H, D = q.shape
    nb = S // ATTN_Q_BLOCK
    scale = D ** -0.5

    def block(bi):
        i0 = bi * ATTN_Q_BLOCK
        qb = lax.dynamic_slice_in_dim(q, i0, ATTN_Q_BLOCK, axis=1) * scale
        pos = i0 + jnp.arange(ATTN_Q_BLOCK)
        outs, lses = [], []
        for window, dil in ATTN_PAIRS:
            reach = (window // 2) // dil
            offs = dil * jnp.arange(-reach, reach + 1)
            idx = pos[:, None] + offs[None, :]
            valid = (idx >= 0) & (idx < S)
            idx_c = jnp.clip(idx, 0, S - 1)
            kg = jnp.take(k, idx_c, axis=1)
            vg = jnp.take(v, idx_c, axis=1)
            s = jnp.einsum('bqhd,bqkhd->bhqk', qb, kg)
            s = jnp.where(valid[None, None], s, -jnp.inf)
            m = jnp.max(s, axis=-1, keepdims=True)
            e = jnp.exp(s - m)
            den = jnp.sum(e, axis=-1)
            o = jnp.einsum('bhqk,bqkhd->bqhd', e, vg) / den.transpose(0, 2, 1)[..., None]
            outs.append(o)
            lses.append((m[..., 0] + jnp.log(den)).transpose(0, 2, 1))
        wts = jax.nn.softmax(jnp.stack(lses, axis=0), axis=0)
        out = wts[0][..., None] * outs[0]
        for j in range(1, len(outs)):
            out = out + wts[j][..., None] * outs[j]
        return out

    out = lax.map(block, jnp.arange(nb))
    return out.transpose(1, 0, 2, 3, 4).reshape(B, S, H, D)


def gla_scan(q, k, v, la):
    B, H, S, Dk = q.shape
    Dv = v.shape[-1]
    L = GLA_CHUNK
    N = S // L
    q = q.reshape(B, H, N, L, Dk)
    k = k.reshape(B, H, N, L, Dk)
    v = v.reshape(B, H, N, L, Dv)
    la = la.reshape(B, H, N, L, Dk)
    b = jnp.cumsum(la, axis=3)
    g = b[:, :, :, -1, :]
    b_ref = b[:, :, :, L // 2:L // 2 + 1, :]
    tril = jnp.tril(jnp.ones((L, L), dtype=bool))
    A = jnp.einsum('bhnld,bhnsd->bhnls', q * jnp.exp(b - b_ref), k * jnp.exp(b_ref - b))
    A = jnp.where(tril, A, 0.0)
    o_intra = jnp.einsum('bhnls,bhnse->bhnle', A, v)
    S_loc = jnp.einsum('bhnld,bhnle->bhnde', k * jnp.exp(g[:, :, :, None, :] - b), v)

    def step(state, inp):
        dec, sl = inp
        return dec[..., None] * state + sl, state

    _, S_prev = lax.scan(step, jnp.zeros((B, H, Dk, Dv), q.dtype),
                         (jnp.moveaxis(jnp.exp(g), 2, 0), jnp.moveaxis(S_loc, 2, 0)))
    S_prev = jnp.moveaxis(S_prev, 0, 2)
    o_inter = jnp.einsum('bhnld,bhnde->bhnle', q * jnp.exp(b), S_prev)
    return (o_intra + o_inter).reshape(B, H, S, Dv)


def gla_mixer(q, k, v, r, a1_f, a1_b, w_a2, a_b, norm_g):
    B, S, _ = q.shape
    H = GLA_HEADS
    qh = to_bhsd(q, H) * (GLA_HEAD_K ** -0.5)
    kh = to_bhsd(k, H)
    vh = to_bhsd(v, H)
    la_f = to_bhsd(jax.nn.log_sigmoid(a1_f @ w_a2[0] + a_b[0]) / GLA_TAU, H)
    la_b = to_bhsd(jax.nn.log_sigmoid(a1_b @ w_a2[1] + a_b[1]) / GLA_TAU, H)
    o = gla_scan(qh, kh, vh, la_f) + rev(gla_scan(rev(qh), rev(kh), rev(vh), rev(la_b)))
    o = o.transpose(0, 2, 1, 3)
    on = o * lax.rsqrt(jnp.mean(jnp.square(o), axis=-1, keepdims=True) + HEAD_NORM_EPS) * norm_g.reshape(H, GLA_HEAD_V)
    return on.reshape(B, S, GLA_WIDTH) * jax.nn.silu(r)


def conv_ffn(x, w_up, conv_w, conv_b, w_down):
    S = x.shape[1]
    u = x @ w_up
    up = jnp.pad(u, ((0, 0), (CONV_HALF, CONV_HALF), (0, 0)))
    acc = conv_b
    for j in range(CONV_W):
        acc = acc + up[:, j:j + S] * conv_w[j]
    gate, val = jnp.split(acc, 2, axis=-1)
    return (jax.nn.silu(gate) * val) @ w_down


def encoder_layer(x, p_i, w_in, mlstm_gate_b, mlstm_norm_g, gla_w_a2, gla_a_b, gla_norm_g, w_out,
                  ln1_g, ln1_b, w_up, conv_w, conv_b, w_down, ln2_g, ln2_b, w_ple, w_ple_gate):
    B, S, _ = x.shape
    proj = (x @ w_in).astype(jnp.float32)
    splits = []
    acc = 0
    for size in IN_SIZES[:-1]:
        acc += size
        splits.append(acc)
    (mq, mk, mv, mo, mgate, aq, ak, av, gq, gk, gv, gr, ga_f, ga_b) = jnp.split(proj, splits, axis=-1)
    y_m = mlstm_mixer(mq, mk, mv, mo, mgate, mlstm_gate_b, mlstm_norm_g)
    pos = jnp.arange(S, dtype=jnp.float32)
    aqh = rope(aq.reshape(B, S, ATTN_HEADS, HEAD_DIM), pos)
    akh = rope(ak.reshape(B, S, ATTN_HEADS, HEAD_DIM), pos)
    y_a = dilated_attention(aqh, akh, av.reshape(B, S, ATTN_HEADS, HEAD_DIM)).reshape(B, S, ATTN_WIDTH)
    y_g = gla_mixer(gq, gk, gv, gr, ga_f, ga_b, gla_w_a2, gla_a_b, gla_norm_g)
    mix = jnp.concatenate([y_m, y_a, y_g], axis=-1).astype(x.dtype) @ w_out
    x = layer_norm(DEEPNORM_ALPHA * x + mix, ln1_g, ln1_b)
    x = layer_norm(DEEPNORM_ALPHA * x + conv_ffn(x, w_up, conv_w, conv_b, w_down), ln2_g, ln2_b)
    x = x + jax.nn.sigmoid(x @ w_ple_gate) * (p_i @ w_ple)
    return x


def trunk(x, p, w_in, mlstm_gate_b, mlstm_norm_g, gla_w_a2, gla_a_b, gla_norm_g, w_out,
          ln1_g, ln1_b, w_up, conv_w, conv_b, w_down, ln2_g, ln2_b, w_ple, w_ple_gate):
    for i in range(DEPTH):
        x = encoder_layer(x, p[i], w_in[i], mlstm_gate_b[i], mlstm_norm_g[i], gla_w_a2[i], gla_a_b[i],
                          gla_norm_g[i], w_out[i], ln1_g[i], ln1_b[i], w_up[i], conv_w[i], conv_b[i],
                          w_down[i], ln2_g[i], ln2_b[i], w_ple[i], w_ple_gate[i])
    return x


def setup_inputs(seed: int = 0) -> dict:
    key = jax.random.key(seed)
    ks = jax.random.split(key, 24)

    def nrm(k, shape, scale):
        return jax.random.normal(k, shape, jnp.float32) * scale

    gate_base = jnp.array([0.0, 3.0, 0.0, 3.0], jnp.float32)[None, :, None]
    return {
        'x_prompt': nrm(ks[0], (BATCH, SEQ, D_MODEL), 1.0),
        'x_sample': nrm(ks[1], (DEC_BATCH, DEC_SEQ, D_MODEL), 1.0),
        'p_prompt': nrm(ks[2], (DEPTH, BATCH, SEQ, PLE_DIM), 1.0),
        'p_sample': nrm(ks[3], (DEPTH, DEC_BATCH, DEC_SEQ, PLE_DIM), 1.0),
        'w_in': nrm(ks[4], (DEPTH, D_MODEL, IN_COLS), D_MODEL ** -0.5),
        'mlstm_gate_b': gate_base + nrm(ks[5], (DEPTH, 4, MLSTM_HEADS), 0.1),
        'mlstm_norm_g': 1.0 + nrm(ks[6], (DEPTH, MLSTM_WIDTH), 0.01),
        'gla_w_a2': nrm(ks[7], (DEPTH, 2, GLA_RANK, GLA_KEY_WIDTH), GLA_RANK ** -0.5),
        'gla_a_b': nrm(ks[8], (DEPTH, 2, GLA_KEY_WIDTH), 0.1),
        'gla_norm_g': 1.0 + nrm(ks[9], (DEPTH, GLA_WIDTH), 0.01),
        'w_out': nrm(ks[10], (DEPTH, MIX_WIDTH, D_MODEL), MIX_WIDTH ** -0.5 * DEEPNORM_BETA),
        'ln1_g': 1.0 + nrm(ks[11], (DEPTH, D_MODEL), 0.01),
        'ln1_b': nrm(ks[12], (DEPTH, D_MODEL), 0.01),
        'w_up': nrm(ks[13], (DEPTH, D_MODEL, 2 * D_FF), D_MODEL ** -0.5),
        'conv_w': nrm(ks[14], (DEPTH, CONV_W, 2 * D_FF), CONV_W ** -0.5),
        'conv_b': nrm(ks[15], (DEPTH, 2 * D_FF), 0.01),
        'w_down': nrm(ks[16], (DEPTH, D_FF, D_MODEL), D_FF ** -0.5 * DEEPNORM_BETA),
        'ln2_g': 1.0 + nrm(ks[17], (DEPTH, D_MODEL), 0.01),
        'ln2_b': nrm(ks[18], (DEPTH, D_MODEL), 0.01),
        'w_ple': nrm(ks[19], (DEPTH, PLE_DIM, D_MODEL), PLE_DIM ** -0.5),
        'w_ple_gate': nrm(ks[20], (DEPTH, D_MODEL, D_MODEL), D_MODEL ** -0.5),
    }


def reference(x_prompt, x_sample, p_prompt, p_sample, w_in, mlstm_gate_b, mlstm_norm_g, gla_w_a2, gla_a_b,
              gla_norm_g, w_out, ln1_g, ln1_b, w_up, conv_w, conv_b, w_down, ln2_g, ln2_b, w_ple, w_ple_gate):
    y_prompt = trunk(x_prompt, p_prompt, w_in, mlstm_gate_b, mlstm_norm_g, gla_w_a2, gla_a_b, gla_norm_g,
                     w_out, ln1_g, ln1_b, w_up, conv_w, conv_b, w_down, ln2_g, ln2_b, w_ple, w_ple_gate)
    y_sample = trunk(x_sample, p_sample, w_in, mlstm_gate_b, mlstm_norm_g, gla_w_a2, gla_a_b, gla_norm_g,
                     w_out, ln1_g, ln1_b, w_up, conv_w, conv_b, w_down, ln2_g, ln2_b, w_ple, w_ple_gate)
    return (y_prompt, y_sample)
```

```python
import functools

import jax
import jax.numpy as jnp
from jax import lax
from jax.experimental import pallas as pl
from jax.experimental.pallas import tpu as pltpu

F32 = jnp.float32
BF16 = jnp.bfloat16
HIGHEST = lax.Precision.HIGHEST

D_MODEL = 4096
HEAD_DIM = 128
MLSTM_HEADS = 8
ATTN_HEADS = 12
GLA_HEADS = 12
GLA_HEAD_K = 64
GLA_RANK = 16
GLA_TAU = 16.0
GLA_CHUNK = 64
ATTN_REACH = 64
ATTN_DILATIONS = (1, 4, 16)
ROPE_THETA = 10000.0
D_FF = 11008
PLE_DIM = 256
LN_EPS = 1e-5
HEAD_NORM_EPS = 1e-6
DEPTH = 2
DEEPNORM_ALPHA = (2 * DEPTH) ** 0.25

LANES = 128
VMEM_LIMIT = 56 * 1024 * 1024

G_MQ, G_MK, G_MV, G_MO = 0, 8, 16, 24
G_AQ, G_AK, G_AV = 32, 44, 56
G_GQ, G_GK, G_GV, G_GR = 68, 74, 80, 92
N_PROJ_GROUPS = 104
GATE_I_F, GATE_F_F, GATE_I_B, GATE_F_B, GATE_A_F, GATE_A_B = 0, 8, 16, 24, 32, 48

MLSTM_CHUNK = 256
GLA_STEP = 256
ATTN_TILE = 2048
ATTN_HALO = ATTN_REACH * ATTN_DILATIONS[-1]
ATTN_QB = 128
ATTN_KB = ATTN_QB + 2 * ATTN_REACH


def _cparams(sem):
    return pltpu.CompilerParams(dimension_semantics=sem, vmem_limit_bytes=VMEM_LIMIT)


def _log_sigmoid(x):
    return jnp.minimum(x, 0.0) - jnp.log1p(jnp.exp(-jnp.abs(x)))


def _sigmoid(x):
    return 1.0 / (1.0 + jnp.exp(-x))


def _dot(a, b):
    return jnp.dot(a, b, preferred_element_type=F32)


def _dot_nt(a, b):
    return lax.dot_general(a, b, (((1,), (1,)), ((), ())), preferred_element_type=F32)


def _dot_exact(a, b):
    return jnp.dot(a, b, precision=HIGHEST, preferred_element_type=F32)


def _proj_kernel(x_ref, w_ref, o_ref):
    acc = _dot(x_ref[...], w_ref[...])
    for g in range(o_ref.shape[0]):
        o_ref[g] = acc[:, g * LANES:(g + 1) * LANES].astype(o_ref.dtype)


def proj_matmul(xb, w, tm=1024, tn=1024):
    T, K = xb.shape
    N = w.shape[1]
    return pl.pallas_call(
        _proj_kernel,
        grid=(T // tm, N // tn),
        in_specs=[pl.BlockSpec((tm, K), lambda i, j: (i, 0)),
                  pl.BlockSpec((K, tn), lambda i, j: (0, j))],
        out_specs=pl.BlockSpec((tn // LANES, tm, LANES), lambda i, j: (j, i, 0)),
        out_shape=jax.ShapeDtypeStruct((N // LANES, T, LANES), F32),
        name="proj_matmul",
        compiler_params=_cparams(("parallel", "parallel")),
    )(xb, w)


def _mm_kernel(x_ref, w_ref, o_ref):
    o_ref[...] = _dot(x_ref[...], w_ref[...]).astype(o_ref.dtype)


def matmul(xb, w, tm, tn, out_dtype=F32):
    T, K = xb.shape
    N = w.shape[1]
    return pl.pallas_call(
        _mm_kernel,
        grid=(T // tm, N // tn),
        in_specs=[pl.BlockSpec((tm, K), lambda i, j: (i, 0)),
                  pl.BlockSpec((K, tn), lambda i, j: (0, j))],
        out_specs=pl.BlockSpec((tm, tn), lambda i, j: (i, j)),
        out_shape=jax.ShapeDtypeStruct((T, N), out_dtype),
        name="matmul",
        compiler_params=_cparams(("parallel", "parallel")),
    )(xb, w)


def _mix_out_kernel(ym_ref, ya_ref, yg_ref, w_ref, o_ref):
    parts = ([ym_ref[g] for g in range(ym_ref.shape[0])] + [ya_ref[g] for g in range(ya_ref.shape[0])]
             + [yg_ref[g] for g in range(yg_ref.shape[0])])
    y = jnp.concatenate(parts, axis=-1)
    o_ref[...] = _dot(y, w_ref[...])


def mix_out_matmul(ym, ya, yg, w, tm=512, tn=1024):
    T = ym.shape[1]
    N = w.shape[1]

    def gspec(a):
        return pl.BlockSpec((a.shape[0], tm, LANES), lambda i, j: (0, i, 0))

    return pl.pallas_call(
        _mix_out_kernel,
        grid=(T // tm, N // tn),
        in_specs=[gspec(ym), gspec(ya), gspec(yg), pl.BlockSpec((w.shape[0], tn), lambda i, j: (0, j))],
        out_specs=pl.BlockSpec((tm, tn), lambda i, j: (i, j)),
        out_shape=jax.ShapeDtypeStruct((T, N), F32),
        name="mix_out_matmul",
        compiler_params=_cparams(("parallel", "parallel")),
    )(ym, ya, yg, w)


def _gates_kernel(x_ref, w_ref, wt_ref, bc_ref, br_ref, pc_ref, cc_ref, pr_ref, cr_ref, *, chunk):
    x = x_ref[...]
    tm = x.shape[0]
    pc = _dot(x, w_ref[...]) + bc_ref[...]
    pr = _dot_nt(wt_ref[...], x) + br_ref[...]
    pc_ref[...] = pc
    pr_ref[...] = pr
    ls_c = _log_sigmoid(pc)
    ls_r = _log_sigmoid(pr)
    row = lax.broadcasted_iota(jnp.int32, (chunk, chunk), 0)
    col = lax.broadcasted_iota(jnp.int32, (chunk, chunk), 1)
    lower = (row >= col).astype(F32)
    upper = (row <= col).astype(F32)
    lane = lax.broadcasted_iota(jnp.int32, (chunk, LANES), 1)
    sub = lax.broadcasted_iota(jnp.int32, (LANES, chunk), 0)
    bwd_c = (lane >= GATE_F_B) & (lane < GATE_F_B + MLSTM_HEADS)
    bwd_r = (sub >= GATE_F_B) & (sub < GATE_F_B + MLSTM_HEADS)
    for c in range(tm // chunk):
        sl = slice(c * chunk, (c + 1) * chunk)
        lc = ls_c[sl]
        cc_ref[sl, :] = jnp.where(bwd_c, _dot_exact(upper, lc), _dot_exact(lower, lc))
        lr = ls_r[:, sl]
        cr_ref[:, sl] = jnp.where(bwd_r, _dot_exact(lr, lower), _dot_exact(lr, upper))


def gates_project(xb, w_small, w_small_t, bias_c, bias_r, chunk, tm=1024):
    T, K = xb.shape
    return pl.pallas_call(
        functools.partial(_gates_kernel, chunk=chunk),
        grid=(T // tm,),
        in_specs=[pl.BlockSpec((tm, K), lambda i: (i, 0)),
                  pl.BlockSpec((K, LANES), lambda i: (0, 0)),
                  pl.BlockSpec((LANES, K), lambda i: (0, 0)),
                  pl.BlockSpec((1, LANES), lambda i: (0, 0)),
                  pl.BlockSpec((LANES, 1), lambda i: (0, 0))],
        out_specs=[pl.BlockSpec((tm, LANES), lambda i: (i, 0)),
                   pl.BlockSpec((tm, LANES), lambda i: (i, 0)),
                   pl.BlockSpec((LANES, tm), lambda i: (0, i)),
                   pl.BlockSpec((LANES, tm), lambda i: (0, i))],
        out_shape=[jax.ShapeDtypeStruct((T, LANES), F32), jax.ShapeDtypeStruct((T, LANES), F32),
                   jax.ShapeDtypeStruct((LANES, T), F32), jax.ShapeDtypeStruct((LANES, T), F32)],
        name="gates_project",
        compiler_params=_cparams(("parallel",)),
    )(xb, w_small, w_small_t, bias_c, bias_r)


def _mlstm_kernel(*refs, reverse, final):
    if final:
        (q_ref, k_ref, v_ref, pc_ref, cc_ref, pr_ref, cr_ref, hf_ref, o_ref, g_ref,
         out_ref, ct_s, n_s, m_s) = refs
    else:
        q_ref, k_ref, v_ref, pc_ref, cc_ref, pr_ref, cr_ref, out_ref, ct_s, n_s, m_s = refs
    c = pl.program_id(1)
    h = pl.program_id(2)
    L = q_ref.shape[0]

    @pl.when(c == 0)
    def _():
        ct_s[h] = jnp.zeros(ct_s.shape[1:], F32)
        n_s[h] = jnp.zeros(n_s.shape[1:], F32)
        m_s[h] = jnp.zeros(m_s.shape[1:], F32)

    qf = q_ref[...]
    kf = k_ref[...] * (HEAD_DIM ** -0.5)
    qb = qf.astype(BF16)
    kb = kf.astype(BF16)
    vb = v_ref[...].astype(BF16)

    icol = (GATE_I_B if reverse else GATE_I_F) + h
    fcol = (GATE_F_B if reverse else GATE_F_F) + h
    lane = lax.broadcasted_iota(jnp.int32, (L, LANES), 1)
    i_col = jnp.sum(jnp.where(lane == icol, pc_ref[...], 0.0), axis=-1, keepdims=True)
    b_col = jnp.sum(jnp.where(lane == fcol, cc_ref[...], 0.0), axis=-1, keepdims=True)
    i_row = pr_ref[pl.ds(icol, 1), :]
    b_row = cr_ref[pl.ds(fcol, 1), :]
    r_row = i_row - b_row
    r_col = i_col - b_col
    m_prev = m_s[h][:, 0:1]

    t_idx = lax.broadcasted_iota(jnp.int32, (L, L), 0)
    s_idx = lax.broadcasted_iota(jnp.int32, (L, L), 1)
    causal = (s_idx >= t_idx) if reverse else (s_idx <= t_idx)
    r_mat = jnp.where(causal, r_row, -jnp.inf)
    m_row = jnp.maximum(jnp.max(r_mat, axis=-1, keepdims=True), m_prev)
    p = jnp.exp(r_mat - m_row) * _dot_nt(qb, kb)
    w_inter = jnp.exp(m_prev - m_row)
    ct = ct_s[h]
    num = w_inter * _dot(qb, ct.astype(BF16)) + _dot(p.astype(BF16), vb)
    nq = w_inter * jnp.sum(qf * n_s[h], axis=-1, keepdims=True) + jnp.sum(p, axis=-1, keepdims=True)
    den = jnp.maximum(jnp.abs(nq), jnp.exp(-(b_col + m_row)))
    hout = num / den

    g = b_row[:, 0:1] if reverse else b_row[:, L - 1:L]
    r_max = jnp.max(r_row, axis=-1, keepdims=True)
    m_loc = g + r_max
    m_new = jnp.maximum(g + m_prev, m_loc)
    s_old = jnp.exp(g + m_prev - m_new)
    s_new = jnp.exp(m_loc - m_new)
    kw = kf * jnp.exp(r_col - r_max)
    ct_loc = _dot(kw.T.astype(BF16), vb)
    ct_s[h] = s_old * ct + s_new * ct_loc
    n_s[h] = s_old * n_s[h] + s_new * jnp.sum(kw, axis=0, keepdims=True)
    m_s[h] = jnp.broadcast_to(m_new, m_s.shape[1:])

    if final:
        hs = hf_ref[...] + hout
        mu = jnp.mean(hs, axis=-1, keepdims=True)
        var = jnp.mean(jnp.square(hs - mu), axis=-1, keepdims=True)
        hn = (hs - mu) * lax.rsqrt(var + HEAD_NORM_EPS) * g_ref[...]
        out_ref[...] = (_sigmoid(o_ref[...]) * hn).astype(out_ref.dtype)
    else:
        out_ref[...] = hout


def mlstm_direction(proj, gates, B, S, reverse, hf=None, norm_g=None):
    pc, cc, pr, cr = gates
    L = MLSTM_CHUNK
    nc = S // L
    T = B * S
    H = MLSTM_HEADS
    final = hf is not None

    def tok(b, c):
        return b * nc + (nc - 1 - c if reverse else c)

    def gspec(goff):
        return pl.BlockSpec((None, L, LANES), lambda b, c, h: (goff + h, tok(b, c), 0))

    in_specs = [gspec(G_MQ), gspec(G_MK), gspec(G_MV),
                pl.BlockSpec((L, LANES), lambda b, c, h: (tok(b, c), 0)),
                pl.BlockSpec((L, LANES), lambda b, c, h: (tok(b, c), 0)),
                pl.BlockSpec((LANES, L), lambda b, c, h: (0, tok(b, c))),
                pl.BlockSpec((LANES, L), lambda b, c, h: (0, tok(b, c)))]
    args = [proj, proj, proj, pc, cc, pr, cr]
    if final:
        in_specs += [gspec(0), gspec(G_MO), pl.BlockSpec((None, 1, LANES), lambda b, c, h: (h, 0, 0))]
        args += [hf, proj, norm_g]
    return pl.pallas_call(
        functools.partial(_mlstm_kernel, reverse=reverse, final=final),
        grid=(B, nc, H),
        in_specs=in_specs,
        out_specs=gspec(0),
        out_shape=jax.ShapeDtypeStruct((H, T, LANES), BF16 if final else F32),
        scratch_shapes=[pltpu.VMEM((H, HEAD_DIM, HEAD_DIM), F32), pltpu.VMEM((H, 1, HEAD_DIM), F32),
                        pltpu.VMEM((H, 1, LANES), F32)],
        name="mlstm_bwd" if reverse else "mlstm_fwd",
        compiler_params=_cparams(("arbitrary", "arbitrary", "arbitrary")),
    )(*args)


def _gla_kernel(*refs, reverse, final):
    if final:
        (q_ref, k_ref, v_ref, pc_ref, wa_ref, ab_ref, of_ref, r_ref, g_ref, out_ref, st_s) = refs
    else:
        q_ref, k_ref, v_ref, pc_ref, wa_ref, ab_ref, out_ref, st_s = refs
    c = pl.program_id(1)
    gi = pl.program_id(2)
    TS = q_ref.shape[0]
    L = GLA_CHUNK
    n_sub = TS // L

    @pl.when(c == 0)
    def _():
        st_s[gi] = jnp.zeros(st_s.shape[1:], F32)

    row = lax.broadcasted_iota(jnp.int32, (L, L), 0)
    col = lax.broadcasted_iota(jnp.int32, (L, L), 1)
    lower = row >= col
    causal = (col >= row) if reverse else lower
    cum_mat = ((row <= col) if reverse else lower).astype(F32)
    lane = lax.broadcasted_iota(jnp.int32, (L, LANES), 1)
    head_mask = [lane < GLA_HEAD_K, lane >= GLA_HEAD_K]
    wa = wa_ref[...]
    a_bias = ab_ref[...]

    order = range(n_sub - 1, -1, -1) if reverse else range(n_sub)
    for sc in order:
        sl = slice(sc * L, (sc + 1) * L)
        la = _log_sigmoid(_dot(pc_ref[sl, :].astype(BF16), wa) + a_bias) * (1.0 / GLA_TAU)
        b = _dot_exact(cum_mat, la)
        if reverse:
            g = b[0:1, :]
            b_mid = b[L // 2 - 1:L // 2, :]
        else:
            g = b[L - 1:L, :]
            b_mid = b[L // 2:L // 2 + 1, :]
        qf = q_ref[sl, :] * (GLA_HEAD_K ** -0.5)
        kf = k_ref[sl, :]
        qd = qf * jnp.exp(b - b_mid)
        kd = (kf * jnp.exp(b_mid - b)).astype(BF16)
        kg = kf * jnp.exp(g - b)
        qe = qf * jnp.exp(b)
        decay = jnp.exp(g)
        for hh in range(2):
            vb = v_ref[hh, sl, :].astype(BF16)
            a = jnp.where(causal, _dot_nt(jnp.where(head_mask[hh], qd, 0.0).astype(BF16), kd), 0.0)
            st = st_s[gi, hh]
            o = _dot(a.astype(BF16), vb) + _dot_nt(jnp.where(head_mask[hh], qe, 0.0).astype(BF16),
                                                  st.astype(BF16))
            st_loc = _dot(v_ref[hh, sl, :].T.astype(BF16), jnp.where(head_mask[hh], kg, 0.0).astype(BF16))
            st_s[gi, hh] = st * decay + st_loc
            if final:
                ot = of_ref[hh, sl, :] + o
                on = ot * lax.rsqrt(jnp.mean(jnp.square(ot), axis=-1, keepdims=True) + HEAD_NORM_EPS) * g_ref[hh]
                rr = r_ref[hh, sl, :]
                out_ref[hh, sl, :] = (on * (rr * _sigmoid(rr))).astype(out_ref.dtype)
            else:
                out_ref[hh, sl, :] = o


def gla_direction(proj, pc, wa, a_bias, B, S, reverse, of=None, norm_g=None):
    TS = GLA_STEP
    nc = S // TS
    T = B * S
    NG = GLA_HEADS // 2
    final = of is not None
    d = 1 if reverse else 0

    def tok(b, c):
        return b * nc + (nc - 1 - c if reverse else c)

    def spec1(goff):
        return pl.BlockSpec((None, TS, LANES), lambda b, c, g: (goff + g, tok(b, c), 0))

    def spec2(goff):
        return pl.BlockSpec((2, TS, LANES), lambda b, c, g: (goff // 2 + g, tok(b, c), 0))

    in_specs = [spec1(G_GQ), spec1(G_GK), spec2(G_GV),
                pl.BlockSpec((TS, LANES), lambda b, c, g: (tok(b, c), 0)),
                pl.BlockSpec((None, None, LANES, LANES), lambda b, c, g: (d, g, 0, 0)),
                pl.BlockSpec((None, None, 1, LANES), lambda b, c, g: (d, g, 0, 0))]
    args = [proj, proj, proj, pc, wa, a_bias]
    if final:
        in_specs += [spec2(0), spec2(G_GR), pl.BlockSpec((2, 1, LANES), lambda b, c, g: (g, 0, 0))]
        args += [of, proj, norm_g]
    return pl.pallas_call(
        functools.partial(_gla_kernel, reverse=reverse, final=final),
        grid=(B, nc, NG),
        in_specs=in_specs,
        out_specs=spec2(0),
        out_shape=jax.ShapeDtypeStruct((GLA_HEADS, T, LANES), BF16 if final else F32),
        scratch_shapes=[pltpu.VMEM((NG, 2, HEAD_DIM, LANES), F32)],
        name="gla_bwd" if reverse else "gla_fwd",
        compiler_params=_cparams(("arbitrary", "arbitrary", "arbitrary")),
    )(*args)


def _rope(t, cos2, sin2):
    return t * cos2 + pltpu.roll(t, shift=HEAD_DIM // 2, axis=1) * sin2


def _attn_kernel(q_ref, kp_ref, km_ref, kn_ref, vp_ref, vm_ref, vn_ref,
                 cp_ref, cm_ref, cn_ref, sp_ref, sm_ref, sn_ref, out_ref,
                 q_s, k_s, v_s, acc_s, m_s, l_s, *, seq_len):
    i = pl.program_id(1)
    TQ, HALO, QB, KB = ATTN_TILE, ATTN_HALO, ATTN_QB, ATTN_KB
    t0 = i * TQ

    q_s[...] = _rope(q_ref[...], cm_ref[...], sm_ref[...]) * (HEAD_DIM ** -0.5)
    k_s[0:HALO, :] = _rope(kp_ref[...], cp_ref[...], sp_ref[...])
    k_s[HALO:HALO + TQ, :] = _rope(km_ref[...], cm_ref[...], sm_ref[...])
    k_s[HALO + TQ:, :] = _rope(kn_ref[...], cn_ref[...], sn_ref[...])
    v_s[0:HALO, :] = vp_ref[...]
    v_s[HALO:HALO + TQ, :] = vm_ref[...]
    v_s[HALO + TQ:, :] = vn_ref[...]

    qi = lax.broadcasted_iota(jnp.int32, (QB, KB), 0)
    ki = lax.broadcasted_iota(jnp.int32, (QB, KB), 1)
    band = (ki >= qi) & (ki <= qi + 2 * ATTN_REACH)
    kcol = lax.broadcasted_iota(jnp.int32, (1, KB), 1)

    def block(branch, dil, q_start, k_start):
        def ds(start, n):
            return pl.ds(start, n) if dil == 1 else pl.ds(start, n, stride=dil)
        qb = q_s[ds(q_start, QB), :].astype(BF16)
        kb = k_s[ds(k_start, KB), :].astype(BF16)
        vb = v_s[ds(k_start, KB), :].astype(BF16)
        s = _dot_nt(qb, kb)
        kpos = t0 - HALO + k_start + dil * kcol
        s = jnp.where(band & (kpos >= 0) & (kpos < seq_len), s, -jnp.inf)
        m = jnp.max(s, axis=-1, keepdims=True)
        e = jnp.exp(s - m)
        acc_s[branch, ds(q_start, QB), :] = _dot(e.astype(BF16), vb)
        m_s[branch, ds(q_start, QB), :] = jnp.broadcast_to(m, (QB, LANES))
        l_s[branch, ds(q_start, QB), :] = jnp.broadcast_to(jnp.sum(e, axis=-1, keepdims=True), (QB, LANES))

    for branch, dil in enumerate(ATTN_DILATIONS):
        span = QB * dil
        n_outer = TQ // span

        def body(it, carry, branch=branch, dil=dil, span=span):
            base = (it // dil) * span
            r = it % dil
            q_start = base + r
            block(branch, dil, q_start, HALO + q_start - ATTN_REACH * dil)
            return carry

        lax.fori_loop(0, n_outer * dil, body, 0)

    m_all = jnp.maximum(jnp.maximum(m_s[0], m_s[1]), m_s[2])
    num = jnp.zeros((TQ, LANES), F32)
    den = jnp.zeros((TQ, LANES), F32)
    for branch in range(len(ATTN_DILATIONS)):
        w = jnp.exp(m_s[branch] - m_all)
        num = num + w * acc_s[branch]
        den = den + w * l_s[branch]
    out_ref[...] = (num / den).astype(out_ref.dtype)


def dilated_attention(proj, cos2, sin2, B, S):
    TQ, HALO = ATTN_TILE, ATTN_HALO
    nt = S // TQ
    T = B * S
    hpt = TQ // HALO
    n_halo = T // HALO
    nb = len(ATTN_DILATIONS)

    def main(goff):
        return pl.BlockSpec((None, TQ, LANES), lambda b, i, h: (goff + h, b * nt + i, 0))

    def prev(goff):
        return pl.BlockSpec((None, HALO, LANES),
                            lambda b, i, h: (goff + h, jnp.maximum((b * nt + i) * hpt - 1, 0), 0))

    def nxt(goff):
        return pl.BlockSpec((None, HALO, LANES),
                            lambda b, i, h: (goff + h, jnp.minimum((b * nt + i + 1) * hpt, n_halo - 1), 0))

    t_prev = pl.BlockSpec((HALO, LANES), lambda b, i, h: (jnp.maximum(i * hpt - 1, 0), 0))
    t_main = pl.BlockSpec((TQ, LANES), lambda b, i, h: (i, 0))
    t_next = pl.BlockSpec((HALO, LANES), lambda b, i, h: (jnp.minimum((i + 1) * hpt, S // HALO - 1), 0))

    return pl.pallas_call(
        functools.partial(_attn_kernel, seq_len=S),
        grid=(B, nt, ATTN_HEADS),
        in_specs=[main(G_AQ), prev(G_AK), main(G_AK), nxt(G_AK), prev(G_AV), main(G_AV), nxt(G_AV),
                  t_prev, t_main, t_next, t_prev, t_main, t_next],
        out_specs=main(0),
        out_shape=jax.ShapeDtypeStruct((ATTN_HEADS, T, LANES), BF16),
        scratch_shapes=[pltpu.VMEM((TQ, LANES), F32), pltpu.VMEM((TQ + 2 * HALO, LANES), F32),
                        pltpu.VMEM((TQ + 2 * HALO, LANES), F32), pltpu.VMEM((nb, TQ, LANES), F32),
                        pltpu.VMEM((nb, TQ, LANES), F32), pltpu.VMEM((nb, TQ, LANES), F32)],
        name="dilated_attention",
        compiler_params=_cparams(("parallel", "parallel", "arbitrary")),
    )(proj, proj, proj, proj, proj, proj, proj, cos2, cos2, cos2, sin2, sin2, sin2)


def _ln_kernel(x_ref, y_ref, g_ref, b_ref, o_ref, ob_ref):
    z = DEEPNORM_ALPHA * x_ref[...] + y_ref[...]
    mu = jnp.mean(z, axis=-1, keepdims=True)
    zc = z - mu
    var = jnp.mean(jnp.square(zc), axis=-1, keepdims=True)
    out = zc * lax.rsqrt(var + LN_EPS) * g_ref[...] + b_ref[...]
    o_ref[...] = out
    ob_ref[...] = out.astype(BF16)


def residual_layer_norm(x, y, g, b, tr=256):
    T, D = x.shape
    row = pl.BlockSpec((tr, D), lambda i: (i, 0))
    vec = pl.BlockSpec((1, D), lambda i: (0, 0))
    return pl.pallas_call(
        _ln_kernel,
        grid=(T // tr,),
        in_specs=[row, row, vec, vec],
        out_specs=[row, row],
        out_shape=[jax.ShapeDtypeStruct((T, D), F32), jax.ShapeDtypeStruct((T, D), BF16)],
        name="residual_layer_norm",
        compiler_params=_cparams(("parallel",)),
    )(x, y, g, b)


def _conv_gate_kernel(gp_ref, gm_ref, gn_ref, vp_ref, vm_ref, vn_ref, wg_ref, wv_ref, bg_ref, bv_ref, o_ref,
                      *, n_tiles):
    i = pl.program_id(1)
    ts = gm_ref.shape[0]
    rows = lax.broadcasted_iota(jnp.int32, gm_ref.shape, 0)
    has_prev = (i > 0).astype(F32)
    has_next = (i < n_tiles - 1).astype(F32)

    def conv(p_ref, m_ref, n_ref, w_ref, b_ref):
        u = m_ref[...]
        before = p_ref[p_ref.shape[0] - 1:, :] * has_prev
        after = n_ref[0:1, :] * has_next
        u_prev = jnp.where(rows == 0, before, pltpu.roll(u, shift=1, axis=0))
        u_next = jnp.where(rows == ts - 1, after, pltpu.roll(u, shift=ts - 1, axis=0))
        return b_ref[...] + u_prev * w_ref[0:1, :] + u * w_ref[1:2, :] + u_next * w_ref[2:3, :]

    gate = conv(gp_ref, gm_ref, gn_ref, wg_ref, bg_ref)
    val = conv(vp_ref, vm_ref, vn_ref, wv_ref, bv_ref)
    o_ref[...] = (gate * _sigmoid(gate) * val).astype(o_ref.dtype)


def conv_gate(u, conv_w, conv_b, B, S, ts=1024, tf=256):
    T = B * S
    nt = S // ts
    nf = D_FF // tf
    rb = ts // 8
    n_rb = T // 8

    def main(off):
        return pl.BlockSpec((ts, tf), lambda b, i, j: (b * nt + i, off + j))

    def prev(off):
        return pl.BlockSpec((8, tf), lambda b, i, j: (jnp.maximum((b * nt + i) * rb - 1, 0), off + j))

    def nxt(off):
        return pl.BlockSpec((8, tf), lambda b, i, j: (jnp.minimum((b * nt + i + 1) * rb, n_rb - 1), off + j))

    def wspec(off, r):
        return pl.BlockSpec((r, tf), lambda b, i, j: (0, off + j))

    return pl.pallas_call(
        functools.partial(_conv_gate_kernel, n_tiles=nt),
        grid=(B, nt, nf),
        in_specs=[prev(0), main(0), nxt(0), prev(nf), main(nf), nxt(nf),
                  wspec(0, 3), wspec(nf, 3), wspec(0, 1), wspec(nf, 1)],
        out_specs=pl.BlockSpec((ts, tf), lambda b, i, j: (b * nt + i, j)),
        out_shape=jax.ShapeDtypeStruct((T, D_FF), BF16),
        name="conv_gate",
        compiler_params=_cparams(("parallel", "parallel", "parallel")),
    )(u, u, u, u, u, u, conv_w, conv_w, conv_b, conv_b)


def _ple_kernel(xb_ref, wg_ref, p_ref, wp_ref, x_ref, o_ref, ob_ref):
    gate = _sigmoid(_dot(xb_ref[...], wg_ref[...]))
    emb = _dot(p_ref[...].astype(BF16), wp_ref[...])
    out = x_ref[...] + gate * emb
    o_ref[...] = out
    ob_ref[...] = out.astype(BF16)


def ple_update(x, xb, p, w_gate, w_ple, tm=1024, tn=512):
    T, D = x.shape
    return pl.pallas_call(
        _ple_kernel,
        grid=(T // tm, D // tn),
        in_specs=[pl.BlockSpec((tm, D), lambda i, j: (i, 0)),
                  pl.BlockSpec((D, tn), lambda i, j: (0, j)),
                  pl.BlockSpec((tm, PLE_DIM), lambda i, j: (i, 0)),
                  pl.BlockSpec((PLE_DIM, tn), lambda i, j: (0, j)),
                  pl.BlockSpec((tm, tn), lambda i, j: (i, j))],
        out_specs=[pl.BlockSpec((tm, tn), lambda i, j: (i, j)), pl.BlockSpec((tm, tn), lambda i, j: (i, j))],
        out_shape=[jax.ShapeDtypeStruct((T, D), F32), jax.ShapeDtypeStruct((T, D), BF16)],
        name="ple_update",
        compiler_params=_cparams(("parallel", "parallel")),
    )(xb, w_gate, p, w_ple, x)


def _prepare_layer_params(w_in, mlstm_gate_b, mlstm_norm_g, gla_w_a2, gla_a_b, gla_norm_g, w_out,
                          ln1_g, ln1_b, w_up, conv_w, conv_b, w_down, ln2_g, ln2_b, w_ple, w_ple_gate):
    mw = MLSTM_HEADS * HEAD_DIM
    n_gate = 4 * MLSTM_HEADS
    aw = ATTN_HEADS * HEAD_DIM
    gkw = GLA_HEADS * GLA_HEAD_K
    gvw = GLA_HEADS * HEAD_DIM
    o_gate = 4 * mw
    o_attn = o_gate + n_gate
    o_gla = o_attn + 3 * aw
    o_alpha = o_gla + 2 * gkw + 2 * gvw
    w_big = jnp.concatenate([w_in[:, :o_gate], w_in[:, o_attn:o_alpha]], axis=1).astype(BF16)
    w_small = jnp.concatenate([w_in[:, o_gate:o_attn], w_in[:, o_alpha:]], axis=1)
    w_small = jnp.pad(w_small, ((0, 0), (0, LANES - w_small.shape[1]))).astype(BF16)
    bias = jnp.pad(mlstm_gate_b.reshape(-1), (0, LANES - n_gate)).astype(F32)
    wa = jnp.zeros((2, GLA_HEADS // 2, LANES, LANES), F32)
    a2 = gla_w_a2.reshape(2, GLA_RANK, GLA_HEADS // 2, LANES).transpose(0, 2, 1, 3)
    wa = wa.at[0, :, GATE_A_F:GATE_A_F + GLA_RANK, :].set(a2[0])
    wa = wa.at[1, :, GATE_A_B:GATE_A_B + GLA_RANK, :].set(a2[1])
    return dict(
        w_big=w_big, w_small=w_small, w_small_t=w_small.T,
        bias_c=bias.reshape(1, LANES), bias_r=bias.reshape(LANES, 1),
        mlstm_norm_g=mlstm_norm_g.reshape(MLSTM_HEADS, 1, HEAD_DIM),
        wa=wa.astype(BF16), a_bias=gla_a_b.reshape(2, GLA_HEADS // 2, 1, LANES),
        gla_norm_g=gla_norm_g.reshape(GLA_HEADS, 1, HEAD_DIM),
        w_out=w_out.astype(BF16), ln1_g=ln1_g.reshape(1, -1), ln1_b=ln1_b.reshape(1, -1),
        w_up=w_up.astype(BF16), conv_w=conv_w, conv_b=conv_b.reshape(1, -1), w_down=w_down.astype(BF16),
        ln2_g=ln2_g.reshape(1, -1), ln2_b=ln2_b.reshape(1, -1),
        w_ple=w_ple.astype(BF16), w_ple_gate=w_ple_gate.astype(BF16))


def _rope_tables(S):
    half = HEAD_DIM // 2
    inv = ROPE_THETA ** (-jnp.arange(half, dtype=F32) / half)
    ang = jnp.arange(S, dtype=F32)[:, None] * inv[None, :]
    cos, sin = jnp.cos(ang), jnp.sin(ang)
    return jnp.concatenate([cos, cos], axis=-1), jnp.concatenate([-sin, sin], axis=-1)


def _encoder_layer(x, xb, p_i, prm, tables, B, S):
    proj = proj_matmul(xb, prm['w_big'])
    gates = gates_project(xb, prm['w_small'], prm['w_small_t'], prm['bias_c'], prm['bias_r'], MLSTM_CHUNK)
    hf = mlstm_direction(proj, gates, B, S, reverse=False)
    y_m = mlstm_direction(proj, gates, B, S, reverse=True, hf=hf, norm_g=prm['mlstm_norm_g'])
    y_a = dilated_attention(proj, tables[0], tables[1], B, S)
    of = gla_direction(proj, gates[0], prm['wa'], prm['a_bias'], B, S, reverse=False)
    y_g = gla_direction(proj, gates[0], prm['wa'], prm['a_bias'], B, S, reverse=True, of=of,
                        norm_g=prm['gla_norm_g'])
    mix = mix_out_matmul(y_m, y_a, y_g, prm['w_out'])
    x, xb = residual_layer_norm(x, mix, prm['ln1_g'], prm['ln1_b'])
    u = matmul(xb, prm['w_up'], tm=1024, tn=512)
    hmid = conv_gate(u, prm['conv_w'], prm['conv_b'], B, S)
    ffn = matmul(hmid, prm['w_down'], tm=512, tn=512)
    x, xb = residual_layer_norm(x, ffn, prm['ln2_g'], prm['ln2_b'])
    return ple_update(x, xb, p_i, prm['w_ple_gate'], prm['w_ple'])


def _trunk(x, p, layer_params):
    B, S, D = x.shape
    tables = _rope_tables(S)
    x = x.reshape(B * S, D)
    xb = x.astype(BF16)
    for i, prm in enumerate(layer_params):
        x, xb = _encoder_layer(x, xb, p[i].reshape(B * S, -1), prm, tables, B, S)
    return x.reshape(B, S, D)


def kernel(x_prompt, x_sample, p_prompt, p_sample, w_in, mlstm_gate_b, mlstm_norm_g, gla_w_a2, gla_a_b,
           gla_norm_g, w_out, ln1_g, ln1_b, w_up, conv_w, conv_b, w_down, ln2_g, ln2_b, w_ple, w_ple_gate):
    weights = (w_in, mlstm_gate_b, mlstm_norm_g, gla_w_a2, gla_a_b, gla_norm_g, w_out, ln1_g, ln1_b,
               w_up, conv_w, conv_b, w_down, ln2_g, ln2_b, w_ple, w_ple_gate)
    layer_params = [_prepare_layer_params(*(w[i] for w in weights)) for i in range(w_in.shape[0])]
    return (_trunk(x_prompt, p_prompt, layer_params), _trunk(x_sample, p_sample, layer_params))
```

```python
import functools

import jax
import jax.numpy as jnp
from jax import lax
from jax.experimental import pallas as pl
from jax.experimental.pallas import tpu as pltpu

F32 = jnp.float32
BF16 = jnp.bfloat16

D_MODEL = 4096
HEAD_DIM = 128
MLSTM_HEADS = 8
ATTN_HEADS = 12
GLA_HEADS = 12
GLA_HEAD_K = 64
GLA_RANK = 16
GLA_TAU = 16.0
GLA_CHUNK = 64
ATTN_REACH = 64
ATTN_DILATIONS = (1, 4, 16)
ROPE_THETA = 10000.0
D_FF = 11008
PLE_DIM = 256
LN_EPS = 1e-5
HEAD_NORM_EPS = 1e-6
DEPTH = 2
DEEPNORM_ALPHA = (2 * DEPTH) ** 0.25

LANES = 128
VMEM_LIMIT = 56 * 1024 * 1024

G_GV, G_GR, G_GQ, G_GK = 0, 12, 24, 30
G_AQ = 36
G_MQ, G_MK, G_MV, G_MO = 48, 56, 64, 72
G_AK, G_AV = 80, 92
N_PROJ_GROUPS = 104
GATE_I_F, GATE_F_F, GATE_I_B, GATE_F_B, GATE_A_F, GATE_A_B = 0, 8, 16, 24, 32, 48

MLSTM_CHUNK = 256
GLA_STEP = 256
ATTN_TILE = 2048
ATTN_HALO = ATTN_REACH * ATTN_DILATIONS[-1]
ATTN_QB = 128
ATTN_KB = ATTN_QB + 2 * ATTN_REACH
ATTN_UNROLL = 8


def _cparams(sem):
    return pltpu.CompilerParams(dimension_semantics=sem, vmem_limit_bytes=VMEM_LIMIT)


def _log_sigmoid(x):
    return jnp.minimum(x, 0.0) - jnp.log1p(jnp.exp(-jnp.abs(x)))


def _sigmoid(x):
    return 1.0 / (1.0 + jnp.exp(-x))


def _dot(a, b):
    return jnp.dot(a, b, preferred_element_type=F32)


def _dot_nt(a, b):
    return lax.dot_general(a, b, (((1,), (1,)), ((), ())), preferred_element_type=F32)


def _dot_tn(a, b):
    return lax.dot_general(a, b, (((0,), (0,)), ((), ())), preferred_element_type=F32)


def _split3(x):
    hi = x.astype(BF16)
    r1 = x - hi.astype(F32)
    mid = r1.astype(BF16)
    lo = (r1 - mid.astype(F32)).astype(BF16)
    return hi, mid, lo


def _tri_dot(tri, x):
    hi, mid, lo = _split3(x)
    return _dot(tri, hi) + _dot(tri, mid) + _dot(tri, lo)


def _dot_tri(x, tri):
    hi, mid, lo = _split3(x)
    return _dot(hi, tri) + _dot(mid, tri) + _dot(lo, tri)


def _proj_kernel(x_ref, w_ref, o_ref):
    acc = _dot(x_ref[...], w_ref[...])
    for g in range(o_ref.shape[0]):
        o_ref[g] = acc[:, g * LANES:(g + 1) * LANES].astype(o_ref.dtype)


def proj_matmul(xb, w, tm=1024, tn=1024):
    T, K = xb.shape
    N = w.shape[1]
    return pl.pallas_call(
        _proj_kernel,
        grid=(T // tm, N // tn),
        in_specs=[pl.BlockSpec((tm, K), lambda i, j: (i, 0)),
                  pl.BlockSpec((K, tn), lambda i, j: (0, j))],
        out_specs=pl.BlockSpec((tn // LANES, tm, LANES), lambda i, j: (j, i, 0)),
        out_shape=jax.ShapeDtypeStruct((N // LANES, T, LANES), F32),
        name="proj_matmul",
        compiler_params=_cparams(("parallel", "parallel")),
    )(xb, w)


def _mm_kernel(x_ref, w_ref, o_ref):
    o_ref[...] = _dot(x_ref[...], w_ref[...]).astype(o_ref.dtype)


def matmul(xb, w, tm, tn, out_dtype=F32):
    T, K = xb.shape
    N = w.shape[1]
    return pl.pallas_call(
        _mm_kernel,
        grid=(T // tm, N // tn),
        in_specs=[pl.BlockSpec((tm, K), lambda i, j: (i, 0)),
                  pl.BlockSpec((K, tn), lambda i, j: (0, j))],
        out_specs=pl.BlockSpec((tm, tn), lambda i, j: (i, j)),
        out_shape=jax.ShapeDtypeStruct((T, N), out_dtype),
        name="matmul",
        compiler_params=_cparams(("parallel", "parallel")),
    )(xb, w)


def _mix_out_kernel(ym_ref, ya_ref, yg_ref, w_ref, o_ref):
    parts = ([ym_ref[g] for g in range(ym_ref.shape[0])] + [ya_ref[g] for g in range(ya_ref.shape[0])]
             + [yg_ref[g] for g in range(yg_ref.shape[0])])
    y = jnp.concatenate(parts, axis=-1)
    o_ref[...] = _dot(y, w_ref[...])


def mix_out_matmul(ym, ya, yg, w, tm=512, tn=1024):
    T = ym.shape[1]
    N = w.shape[1]

    def gspec(a):
        return pl.BlockSpec((a.shape[0], tm, LANES), lambda i, j: (0, i, 0))

    return pl.pallas_call(
        _mix_out_kernel,
        grid=(T // tm, N // tn),
        in_specs=[gspec(ym), gspec(ya), gspec(yg), pl.BlockSpec((w.shape[0], tn), lambda i, j: (0, j))],
        out_specs=pl.BlockSpec((tm, tn), lambda i, j: (i, j)),
        out_shape=jax.ShapeDtypeStruct((T, N), F32),
        name="mix_out_matmul",
        compiler_params=_cparams(("parallel", "parallel")),
    )(ym, ya, yg, w)


def _gates_kernel(x_ref, w_ref, wt_ref, bc_ref, br_ref, pc_ref, cc_ref, pr_ref, cr_ref, *, chunk):
    x = x_ref[...]
    tm = x.shape[0]
    pc = _dot(x, w_ref[...]) + bc_ref[...]
    pr = _dot_nt(wt_ref[...], x) + br_ref[...]
    pc_ref[...] = pc
    pr_ref[...] = pr
    ls_c = _log_sigmoid(pc)
    ls_r = _log_sigmoid(pr)
    row = lax.broadcasted_iota(jnp.int32, (chunk, chunk), 0)
    col = lax.broadcasted_iota(jnp.int32, (chunk, chunk), 1)
    lower = (row >= col).astype(BF16)
    upper = (row <= col).astype(BF16)
    lane = lax.broadcasted_iota(jnp.int32, (chunk, LANES), 1)
    sub = lax.broadcasted_iota(jnp.int32, (LANES, chunk), 0)
    bwd_c = (lane >= GATE_F_B) & (lane < GATE_F_B + MLSTM_HEADS)
    bwd_r = (sub >= GATE_F_B) & (sub < GATE_F_B + MLSTM_HEADS)
    for c in range(tm // chunk):
        sl = slice(c * chunk, (c + 1) * chunk)
        lc = ls_c[sl]
        cc_ref[sl, :] = jnp.where(bwd_c, _tri_dot(upper, lc), _tri_dot(lower, lc))
        lr = ls_r[:, sl]
        cr_ref[:, sl] = jnp.where(bwd_r, _dot_tri(lr, lower), _dot_tri(lr, upper))


def gates_project(xb, w_small, w_small_t, bias_c, bias_r, chunk, tm=1024):
    T, K = xb.shape
    return pl.pallas_call(
        functools.partial(_gates_kernel, chunk=chunk),
        grid=(T // tm,),
        in_specs=[pl.BlockSpec((tm, K), lambda i: (i, 0)),
                  pl.BlockSpec((K, LANES), lambda i: (0, 0)),
                  pl.BlockSpec((LANES, K), lambda i: (0, 0)),
                  pl.BlockSpec((1, LANES), lambda i: (0, 0)),
                  pl.BlockSpec((LANES, 1), lambda i: (0, 0))],
        out_specs=[pl.BlockSpec((tm, LANES), lambda i: (i, 0)),
                   pl.BlockSpec((tm, LANES), lambda i: (i, 0)),
                   pl.BlockSpec((LANES, tm), lambda i: (0, i)),
                   pl.BlockSpec((LANES, tm), lambda i: (0, i))],
        out_shape=[jax.ShapeDtypeStruct((T, LANES), F32), jax.ShapeDtypeStruct((T, LANES), F32),
                   jax.ShapeDtypeStruct((LANES, T), F32), jax.ShapeDtypeStruct((LANES, T), F32)],
        name="gates_project",
        compiler_params=_cparams(("parallel",)),
    )(xb, w_small, w_small_t, bias_c, bias_r)


def _mlstm_kernel(*refs, reverse, final):
    if final:
        (q_ref, k_ref, v_ref, pc_ref, cc_ref, pr_ref, cr_ref, hf_ref, o_ref, g_ref,
         out_ref, ct_s, n_s, m_s) = refs
    else:
        q_ref, k_ref, v_ref, pc_ref, cc_ref, pr_ref, cr_ref, out_ref, ct_s, n_s, m_s = refs
    c = pl.program_id(1)
    H, L = q_ref.shape[0], q_ref.shape[1]

    @pl.when(c == 0)
    def _():
        ct_s[...] = jnp.zeros(ct_s.shape, F32)
        n_s[...] = jnp.zeros(n_s.shape, F32)
        m_s[...] = jnp.zeros(m_s.shape, F32)

    pc = pc_ref[...]
    cc = cc_ref[...]
    lane = lax.broadcasted_iota(jnp.int32, (L, LANES), 1)
    t_idx = lax.broadcasted_iota(jnp.int32, (L, L), 0)
    s_idx = lax.broadcasted_iota(jnp.int32, (L, L), 1)
    causal = (s_idx >= t_idx) if reverse else (s_idx <= t_idx)

    for h in range(H):
        qf = q_ref[h]
        kf = k_ref[h] * (HEAD_DIM ** -0.5)
        qb = qf.astype(BF16)
        kb = kf.astype(BF16)
        vb = v_ref[h].astype(BF16)

        icol = (GATE_I_B if reverse else GATE_I_F) + h
        fcol = (GATE_F_B if reverse else GATE_F_F) + h
        i_col = jnp.sum(jnp.where(lane == icol, pc, 0.0), axis=-1, keepdims=True)
        b_col = jnp.sum(jnp.where(lane == fcol, cc, 0.0), axis=-1, keepdims=True)
        i_row = pr_ref[icol:icol + 1, :]
        b_row = cr_ref[fcol:fcol + 1, :]
        r_row = i_row - b_row
        r_col = i_col - b_col
        m_prev = m_s[h][:, 0:1]

        r_mat = jnp.where(causal, r_row, -jnp.inf)
        m_row = jnp.maximum(jnp.max(r_mat, axis=-1, keepdims=True), m_prev)
        p = jnp.exp(r_mat - m_row) * _dot_nt(qb, kb)
        w_inter = jnp.exp(m_prev - m_row)
        ct = ct_s[h]
        num = w_inter * _dot(qb, ct.astype(BF16)) + _dot(p.astype(BF16), vb)
        nq = w_inter * jnp.sum(qf * n_s[h], axis=-1, keepdims=True) + jnp.sum(p, axis=-1, keepdims=True)
        den = jnp.maximum(jnp.abs(nq), jnp.exp(-(b_col + m_row)))
        hout = num / den

        g = b_row[:, 0:1] if reverse else b_row[:, L - 1:L]
        r_max = jnp.max(r_row, axis=-1, keepdims=True)
        m_loc = g + r_max
        m_new = jnp.maximum(g + m_prev, m_loc)
        s_old = jnp.exp(g + m_prev - m_new)
        s_new = jnp.exp(m_loc - m_new)
        kw = kf * jnp.exp(r_col - r_max)
        ct_loc = _dot_tn(kw.astype(BF16), vb)
        ct_s[h] = s_old * ct + s_new * ct_loc
        n_s[h] = s_old * n_s[h] + s_new * jnp.sum(kw, axis=0, keepdims=True)
        m_s[h] = jnp.broadcast_to(m_new, m_s.shape[1:])

        if final:
            hs = hf_ref[h] + hout
            mu = jnp.mean(hs, axis=-1, keepdims=True)
            var = jnp.mean(jnp.square(hs - mu), axis=-1, keepdims=True)
            hn = (hs - mu) * lax.rsqrt(var + HEAD_NORM_EPS) * g_ref[h]
            out_ref[h] = (_sigmoid(o_ref[h]) * hn).astype(out_ref.dtype)
        else:
            out_ref[h] = hout


def mlstm_direction(proj, gates, B, S, reverse, hf=None, norm_g=None):
    pc, cc, pr, cr = gates
    L = MLSTM_CHUNK
    nc = S // L
    T = B * S
    H = MLSTM_HEADS
    final = hf is not None

    def tok(b, c):
        return b * nc + (nc - 1 - c if reverse else c)

    def gspec(goff):
        return pl.BlockSpec((H, L, LANES), lambda b, c: (goff // H, tok(b, c), 0))

    in_specs = [gspec(G_MQ), gspec(G_MK), gspec(G_MV),
                pl.BlockSpec((L, LANES), lambda b, c: (tok(b, c), 0)),
                pl.BlockSpec((L, LANES), lambda b, c: (tok(b, c), 0)),
                pl.BlockSpec((LANES, L), lambda b, c: (0, tok(b, c))),
                pl.BlockSpec((LANES, L), lambda b, c: (0, tok(b, c)))]
    args = [proj, proj, proj, pc, cc, pr, cr]
    if final:
        in_specs += [gspec(0), gspec(G_MO), pl.BlockSpec((H, 1, LANES), lambda b, c: (0, 0, 0))]
        args += [hf, proj, norm_g]
    return pl.pallas_call(
        functools.partial(_mlstm_kernel, reverse=reverse, final=final),
        grid=(B, nc),
        in_specs=in_specs,
        out_specs=gspec(0),
        out_shape=jax.ShapeDtypeStruct((H, T, LANES), BF16 if final else F32),
        scratch_shapes=[pltpu.VMEM((H, HEAD_DIM, HEAD_DIM), F32), pltpu.VMEM((H, 1, HEAD_DIM), F32),
                        pltpu.VMEM((H, 1, LANES), F32)],
        name="mlstm_bwd" if reverse else "mlstm_fwd",
        compiler_params=_cparams(("arbitrary", "arbitrary")),
    )(*args)


def _gla_kernel(*refs, reverse, final):
    if final:
        (q_ref, k_ref, v_ref, pc_ref, wa_ref, ab_ref, of_ref, r_ref, g_ref, out_ref, st_s) = refs
    else:
        q_ref, k_ref, v_ref, pc_ref, wa_ref, ab_ref, out_ref, st_s = refs
    c = pl.program_id(1)
    NG, TS = q_ref.shape[0], q_ref.shape[1]
    L = GLA_CHUNK
    n_sub = TS // L

    @pl.when(c == 0)
    def _():
        st_s[...] = jnp.zeros(st_s.shape, F32)

    row = lax.broadcasted_iota(jnp.int32, (TS, TS), 0)
    col = lax.broadcasted_iota(jnp.int32, (TS, TS), 1)
    same_chunk = (row // L) == (col // L)
    causal = same_chunk & ((col >= row) if reverse else (col <= row))
    cum_mat = causal.astype(BF16)
    lane = lax.broadcasted_iota(jnp.int32, (TS, LANES), 1)
    head_mask = [lane < GLA_HEAD_K, lane >= GLA_HEAD_K]
    pcb = pc_ref[...].astype(BF16)
    order = list(range(n_sub - 1, -1, -1)) if reverse else list(range(n_sub))

    for gi in range(NG):
        la = _log_sigmoid(_dot(pcb, wa_ref[gi]) + ab_ref[gi]) * (1.0 / GLA_TAU)
        b = _tri_dot(cum_mat, la)

        def per_chunk_rows(idx, b=b):
            return jnp.concatenate([jnp.broadcast_to(b[s * L + idx:s * L + idx + 1, :], (L, LANES))
                                    for s in range(n_sub)], axis=0)

        g_full = per_chunk_rows(0 if reverse else L - 1)
        b_mid = per_chunk_rows(L // 2 - 1 if reverse else L // 2)
        qf = q_ref[gi] * (GLA_HEAD_K ** -0.5)
        kf = k_ref[gi]
        qd = qf * jnp.exp(b - b_mid)
        kd = (kf * jnp.exp(b_mid - b)).astype(BF16)
        kg = kf * jnp.exp(g_full - b)
        qe = qf * jnp.exp(b)

        for hh in range(2):
            hd = 2 * gi + hh
            vb = v_ref[hd].astype(BF16)
            a = jnp.where(causal, _dot_nt(jnp.where(head_mask[hh], qd, 0.0).astype(BF16), kd), 0.0)
            o_intra = _dot(a.astype(BF16), vb)
            kg_h = jnp.where(head_mask[hh], kg, 0.0).astype(BF16)
            qe_h = jnp.where(head_mask[hh], qe, 0.0).astype(BF16)
            st = st_s[hd]
            o_inter = [None] * n_sub
            for s in order:
                sl = slice(s * L, (s + 1) * L)
                o_inter[s] = _dot_nt(qe_h[sl], st.astype(BF16))
                decay = jnp.exp(b[s * L:s * L + 1, :] if reverse else b[s * L + L - 1:s * L + L, :])
                st = st * decay + _dot_tn(vb[sl], kg_h[sl])
            st_s[hd] = st
            o = o_intra + jnp.concatenate(o_inter, axis=0)
            if final:
                ot = of_ref[hd] + o
                on = ot * lax.rsqrt(jnp.mean(jnp.square(ot), axis=-1, keepdims=True) + HEAD_NORM_EPS) * g_ref[hd]
                rr = r_ref[hd]
                out_ref[hd] = (on * (rr * _sigmoid(rr))).astype(out_ref.dtype)
            else:
                out_ref[hd] = o


def gla_direction(proj, pc, wa, a_bias, B, S, reverse, of=None, norm_g=None):
    TS = GLA_STEP
    nc = S // TS
    T = B * S
    NG = GLA_HEADS // 2
    final = of is not None
    d = 1 if reverse else 0

    def tok(b, c):
        return b * nc + (nc - 1 - c if reverse else c)

    def gspec(goff, n):
        return pl.BlockSpec((n, TS, LANES), lambda b, c: (goff // n, tok(b, c), 0))

    in_specs = [gspec(G_GQ, NG), gspec(G_GK, NG), gspec(G_GV, GLA_HEADS),
                pl.BlockSpec((TS, LANES), lambda b, c: (tok(b, c), 0)),
                pl.BlockSpec((None, NG, LANES, LANES), lambda b, c: (d, 0, 0, 0)),
                pl.BlockSpec((None, NG, 1, LANES), lambda b, c: (d, 0, 0, 0))]
    args = [proj, proj, proj, pc, wa, a_bias]
    if final:
        in_specs += [gspec(0, GLA_HEADS), gspec(G_GR, GLA_HEADS),
                     pl.BlockSpec((GLA_HEADS, 1, LANES), lambda b, c: (0, 0, 0))]
        args += [of, proj, norm_g]
    return pl.pallas_call(
        functools.partial(_gla_kernel, reverse=reverse, final=final),
        grid=(B, nc),
        in_specs=in_specs,
        out_specs=gspec(0, GLA_HEADS),
        out_shape=jax.ShapeDtypeStruct((GLA_HEADS, T, LANES), BF16 if final else F32),
        scratch_shapes=[pltpu.VMEM((GLA_HEADS, HEAD_DIM, LANES), F32)],
        name="gla_bwd" if reverse else "gla_fwd",
        compiler_params=_cparams(("arbitrary", "arbitrary")),
    )(*args)


def _rope(t, cos2, sin2):
    return t * cos2 + pltpu.roll(t, shift=HEAD_DIM // 2, axis=1) * sin2


def _attn_kernel(q_ref, kp_ref, km_ref, kn_ref, vp_ref, vm_ref, vn_ref,
                 cp_ref, cm_ref, cn_ref, sp_ref, sm_ref, sn_ref, out_ref,
                 q_s, k_s, v_s, acc_s, m_s, l_s, *, seq_len):
    i = pl.program_id(1)
    TQ, HALO, QB, KB = ATTN_TILE, ATTN_HALO, ATTN_QB, ATTN_KB
    t0 = i * TQ

    q_s[...] = _rope(q_ref[...], cm_ref[...], sm_ref[...]) * (HEAD_DIM ** -0.5)
    k_s[0:HALO, :] = _rope(kp_ref[...], cp_ref[...], sp_ref[...])
    k_s[HALO:HALO + TQ, :] = _rope(km_ref[...], cm_ref[...], sm_ref[...])
    k_s[HALO + TQ:, :] = _rope(kn_ref[...], cn_ref[...], sn_ref[...])
    v_s[0:HALO, :] = vp_ref[...]
    v_s[HALO:HALO + TQ, :] = vm_ref[...]
    v_s[HALO + TQ:, :] = vn_ref[...]

    qi = lax.broadcasted_iota(jnp.int32, (QB, KB), 0)
    ki = lax.broadcasted_iota(jnp.int32, (QB, KB), 1)
    band = (ki >= qi) & (ki <= qi + 2 * ATTN_REACH)
    kcol = lax.broadcasted_iota(jnp.int32, (1, KB), 1)

    def block(branch, dil, q_start, k_start):
        def ds(start, n):
            return pl.ds(start, n) if dil == 1 else pl.ds(start, n, stride=dil)
        qb = q_s[ds(q_start, QB), :].astype(BF16)
        kb = k_s[ds(k_start, KB), :].astype(BF16)
        vb = v_s[ds(k_start, KB), :].astype(BF16)
        s = _dot_nt(qb, kb)
        kpos = t0 - HALO + k_start + dil * kcol
        s = jnp.where(band & (kpos >= 0) & (kpos < seq_len), s, -jnp.inf)
        m = jnp.max(s, axis=-1, keepdims=True)
        e = jnp.exp(s - m)
        acc_s[branch, ds(q_start, QB), :] = _dot(e.astype(BF16), vb)
        m_s[branch, ds(q_start, QB), :] = jnp.broadcast_to(m, (QB, LANES))
        l_s[branch, ds(q_start, QB), :] = jnp.broadcast_to(jnp.sum(e, axis=-1, keepdims=True), (QB, LANES))

    for branch, dil in enumerate(ATTN_DILATIONS):
        span = QB * dil
        n_outer = TQ // span

        def body(it, carry, branch=branch, dil=dil, span=span):
            base = (it // dil) * span
            r = it % dil
            q_start = base + r
            block(branch, dil, q_start, HALO + q_start - ATTN_REACH * dil)
            return carry

        lax.fori_loop(0, n_outer * dil, body, 0, unroll=ATTN_UNROLL)

    m_all = jnp.maximum(jnp.maximum(m_s[0], m_s[1]), m_s[2])
    num = jnp.zeros((TQ, LANES), F32)
    den = jnp.zeros((TQ, LANES), F32)
    for branch in range(len(ATTN_DILATIONS)):
        w = jnp.exp(m_s[branch] - m_all)
        num = num + w * acc_s[branch]
        den = den + w * l_s[branch]
    out_ref[...] = (num / den).astype(out_ref.dtype)


def dilated_attention(proj, cos2, sin2, B, S):
    TQ, HALO = ATTN_TILE, ATTN_HALO
    nt = S // TQ
    T = B * S
    hpt = TQ // HALO
    n_halo = T // HALO
    nb = len(ATTN_DILATIONS)

    def main(goff):
        return pl.BlockSpec((None, TQ, LANES), lambda b, i, h: (goff + h, b * nt + i, 0))

    def prev(goff):
        return pl.BlockSpec((None, HALO, LANES),
                            lambda b, i, h: (goff + h, jnp.maximum((b * nt + i) * hpt - 1, 0), 0))

    def nxt(goff):
        return pl.BlockSpec((None, HALO, LANES),
                            lambda b, i, h: (goff + h, jnp.minimum((b * nt + i + 1) * hpt, n_halo - 1), 0))

    t_prev = pl.BlockSpec((HALO, LANES), lambda b, i, h: (jnp.maximum(i * hpt - 1, 0), 0))
    t_main = pl.BlockSpec((TQ, LANES), lambda b, i, h: (i, 0))
    t_next = pl.BlockSpec((HALO, LANES), lambda b, i, h: (jnp.minimum((i + 1) * hpt, S // HALO - 1), 0))

    return pl.pallas_call(
        functools.partial(_attn_kernel, seq_len=S),
        grid=(B, nt, ATTN_HEADS),
        in_specs=[main(G_AQ), prev(G_AK), main(G_AK), nxt(G_AK), prev(G_AV), main(G_AV), nxt(G_AV),
                  t_prev, t_main, t_next, t_prev, t_main, t_next],
        out_specs=main(0),
        out_shape=jax.ShapeDtypeStruct((ATTN_HEADS, T, LANES), BF16),
        scratch_shapes=[pltpu.VMEM((TQ, LANES), F32), pltpu.VMEM((TQ + 2 * HALO, LANES), F32),
                        pltpu.VMEM((TQ + 2 * HALO, LANES), F32), pltpu.VMEM((nb, TQ, LANES), F32),
                        pltpu.VMEM((nb, TQ, LANES), F32), pltpu.VMEM((nb, TQ, LANES), F32)],
        name="dilated_attention",
        compiler_params=_cparams(("parallel", "parallel", "arbitrary")),
    )(proj, proj, proj, proj, proj, proj, proj, cos2, cos2, cos2, sin2, sin2, sin2)


def _ln_kernel(x_ref, y_ref, g_ref, b_ref, o_ref, ob_ref):
    z = DEEPNORM_ALPHA * x_ref[...] + y_ref[...]
    mu = jnp.mean(z, axis=-1, keepdims=True)
    zc = z - mu
    var = jnp.mean(jnp.square(zc), axis=-1, keepdims=True)
    out = zc * lax.rsqrt(var + LN_EPS) * g_ref[...] + b_ref[...]
    o_ref[...] = out
    ob_ref[...] = out.astype(BF16)


def residual_layer_norm(x, y, g, b, tr=256):
    T, D = x.shape
    row = pl.BlockSpec((tr, D), lambda i: (i, 0))
    vec = pl.BlockSpec((1, D), lambda i: (0, 0))
    return pl.pallas_call(
        _ln_kernel,
        grid=(T // tr,),
        in_specs=[row, row, vec, vec],
        out_specs=[row, row],
        out_shape=[jax.ShapeDtypeStruct((T, D), F32), jax.ShapeDtypeStruct((T, D), BF16)],
        name="residual_layer_norm",
        compiler_params=_cparams(("parallel",)),
    )(x, y, g, b)


def _conv_gate_kernel(gp_ref, gm_ref, gn_ref, vp_ref, vm_ref, vn_ref, wg_ref, wv_ref, bg_ref, bv_ref, o_ref,
                      *, n_tiles):
    i = pl.program_id(1)
    ts = gm_ref.shape[0]
    rows = lax.broadcasted_iota(jnp.int32, gm_ref.shape, 0)
    has_prev = (i > 0).astype(F32)
    has_next = (i < n_tiles - 1).astype(F32)

    def conv(p_ref, m_ref, n_ref, w_ref, b_ref):
        u = m_ref[...]
        before = p_ref[p_ref.shape[0] - 1:, :] * has_prev
        after = n_ref[0:1, :] * has_next
        u_prev = jnp.where(rows == 0, before, pltpu.roll(u, shift=1, axis=0))
        u_next = jnp.where(rows == ts - 1, after, pltpu.roll(u, shift=ts - 1, axis=0))
        return b_ref[...] + u_prev * w_ref[0:1, :] + u * w_ref[1:2, :] + u_next * w_ref[2:3, :]

    gate = conv(gp_ref, gm_ref, gn_ref, wg_ref, bg_ref)
    val = conv(vp_ref, vm_ref, vn_ref, wv_ref, bv_ref)
    o_ref[...] = (gate * _sigmoid(gate) * val).astype(o_ref.dtype)


def conv_gate(u, conv_w, conv_b, B, S, ts=1024, tf=256):
    T = B * S
    nt = S // ts
    nf = D_FF // tf
    rb = ts // 8
    n_rb = T // 8

    def main(off):
        return pl.BlockSpec((ts, tf), lambda b, i, j: (b * nt + i, off + j))

    def prev(off):
        return pl.BlockSpec((8, tf), lambda b, i, j: (jnp.maximum((b * nt + i) * rb - 1, 0), off + j))

    def nxt(off):
        return pl.BlockSpec((8, tf), lambda b, i, j: (jnp.minimum((b * nt + i + 1) * rb, n_rb - 1), off + j))

    def wspec(off, r):
        return pl.BlockSpec((r, tf), lambda b, i, j: (0, off + j))

    return pl.pallas_call(
        functools.partial(_conv_gate_kernel, n_tiles=nt),
        grid=(B, nt, nf),
        in_specs=[prev(0), main(0), nxt(0), prev(nf), main(nf), nxt(nf),
                  wspec(0, 3), wspec(nf, 3), wspec(0, 1), wspec(nf, 1)],
        out_specs=pl.BlockSpec((ts, tf), lambda b, i, j: (b * nt + i, j)),
        out_shape=jax.ShapeDtypeStruct((T, D_FF), BF16),
        name="conv_gate",
        compiler_params=_cparams(("parallel", "parallel", "parallel")),
    )(u, u, u, u, u, u, conv_w, conv_w, conv_b, conv_b)


def _ple_kernel(xb_ref, wg_ref, p_ref, wp_ref, x_ref, o_ref, ob_ref):
    gate = _sigmoid(_dot(xb_ref[...], wg_ref[...]))
    emb = _dot(p_ref[...].astype(BF16), wp_ref[...])
    out = x_ref[...] + gate * emb
    o_ref[...] = out
    ob_ref[...] = out.astype(BF16)


def ple_update(x, xb, p, w_gate, w_ple, tm=1024, tn=512):
    T, D = x.shape
    return pl.pallas_call(
        _ple_kernel,
        grid=(T // tm, D // tn),
        in_specs=[pl.BlockSpec((tm, D), lambda i, j: (i, 0)),
                  pl.BlockSpec((D, tn), lambda i, j: (0, j)),
                  pl.BlockSpec((tm, PLE_DIM), lambda i, j: (i, 0)),
                  pl.BlockSpec((PLE_DIM, tn), lambda i, j: (0, j)),
                  pl.BlockSpec((tm, tn), lambda i, j: (i, j))],
        out_specs=[pl.BlockSpec((tm, tn), lambda i, j: (i, j)), pl.BlockSpec((tm, tn), lambda i, j: (i, j))],
        out_shape=[jax.ShapeDtypeStruct((T, D), F32), jax.ShapeDtypeStruct((T, D), BF16)],
        name="ple_update",
        compiler_params=_cparams(("parallel", "parallel")),
    )(xb, w_gate, p, w_ple, x)


def _regroup_columns(a):
    mw = MLSTM_HEADS * HEAD_DIM
    aw = ATTN_HEADS * HEAD_DIM
    gkw = GLA_HEADS * GLA_HEAD_K
    gvw = GLA_HEADS * HEAD_DIM
    sizes = [mw] * 4 + [4 * MLSTM_HEADS] + [aw] * 3 + [gkw] * 2 + [gvw] * 2 + [GLA_RANK] * 2
    offs = [0]
    for sz in sizes:
        offs.append(offs[-1] + sz)
    seg = [a[..., offs[i]:offs[i + 1]] for i in range(len(sizes))]
    mq, mk, mv, mo, mgate, aq, ak, av, gq, gk, gv, gr, ga_f, ga_b = seg
    big = jnp.concatenate([gv, gr, gq, gk, aq, mq, mk, mv, mo, ak, av], axis=-1)
    small = jnp.concatenate([mgate, ga_f, ga_b], axis=-1)
    return big, small


def _prepare_layer_params(w_in, mlstm_gate_b, mlstm_norm_g, gla_w_a2, gla_a_b, gla_norm_g, w_out,
                          ln1_g, ln1_b, w_up, conv_w, conv_b, w_down, ln2_g, ln2_b, w_ple, w_ple_gate):
    n_gate = 4 * MLSTM_HEADS
    w_big, w_small = _regroup_columns(w_in)
    w_big = w_big.astype(BF16)
    w_small = jnp.pad(w_small, ((0, 0), (0, LANES - w_small.shape[1]))).astype(BF16)
    bias = jnp.pad(mlstm_gate_b.reshape(-1), (0, LANES - n_gate)).astype(F32)
    wa = jnp.zeros((2, GLA_HEADS // 2, LANES, LANES), F32)
    a2 = gla_w_a2.reshape(2, GLA_RANK, GLA_HEADS // 2, LANES).transpose(0, 2, 1, 3)
    wa = wa.at[0, :, GATE_A_F:GATE_A_F + GLA_RANK, :].set(a2[0])
    wa = wa.at[1, :, GATE_A_B:GATE_A_B + GLA_RANK, :].set(a2[1])
    return dict(
        w_big=w_big, w_small=w_small, w_small_t=w_small.T,
        bias_c=bias.reshape(1, LANES), bias_r=bias.reshape(LANES, 1),
        mlstm_norm_g=mlstm_norm_g.reshape(MLSTM_HEADS, 1, HEAD_DIM),
        wa=wa.astype(BF16), a_bias=gla_a_b.reshape(2, GLA_HEADS // 2, 1, LANES),
        gla_norm_g=gla_norm_g.reshape(GLA_HEADS, 1, HEAD_DIM),
        w_out=w_out.astype(BF16), ln1_g=ln1_g.reshape(1, -1), ln1_b=ln1_b.reshape(1, -1),
        w_up=w_up.astype(BF16), conv_w=conv_w, conv_b=conv_b.reshape(1, -1), w_down=w_down.astype(BF16),
        ln2_g=ln2_g.reshape(1, -1), ln2_b=ln2_b.reshape(1, -1),
        w_ple=w_ple.astype(BF16), w_ple_gate=w_ple_gate.astype(BF16))


def _rope_tables(S):
    half = HEAD_DIM // 2
    inv = ROPE_THETA ** (-jnp.arange(half, dtype=F32) / half)
    ang = jnp.arange(S, dtype=F32)[:, None] * inv[None, :]
    cos, sin = jnp.cos(ang), jnp.sin(ang)
    return jnp.concatenate([cos, cos], axis=-1), jnp.concatenate([-sin, sin], axis=-1)


def _encoder_layer(x, xb, p_i, prm, tables, B, S):
    proj = proj_matmul(xb, prm['w_big'])
    gates = gates_project(xb, prm['w_small'], prm['w_small_t'], prm['bias_c'], prm['bias_r'], MLSTM_CHUNK)
    hf = mlstm_direction(proj, gates, B, S, reverse=False)
    y_m = mlstm_direction(proj, gates, B, S, reverse=True, hf=hf, norm_g=prm['mlstm_norm_g'])
    y_a = dilated_attention(proj, tables[0], tables[1], B, S)
    of = gla_direction(proj, gates[0], prm['wa'], prm['a_bias'], B, S, reverse=False)
    y_g = gla_direction(proj, gates[0], prm['wa'], prm['a_bias'], B, S, reverse=True, of=of,
                        norm_g=prm['gla_norm_g'])
    mix = mix_out_matmul(y_m, y_a, y_g, prm['w_out'])
    x, xb = residual_layer_norm(x, mix, prm['ln1_g'], prm['ln1_b'])
    u = matmul(xb, prm['w_up'], tm=1024, tn=512)
    hmid = conv_gate(u, prm['conv_w'], prm['conv_b'], B, S)
    ffn = matmul(hmid, prm['w_down'], tm=512, tn=512)
    x, xb = residual_layer_norm(x, ffn, prm['ln2_g'], prm['ln2_b'])
    return ple_update(x, xb, p_i, prm['w_ple_gate'], prm['w_ple'])


def _trunk(x, p, layer_params):
    B, S, D = x.shape
    tables = _rope_tables(S)
    x = x.reshape(B * S, D)
    xb = x.astype(BF16)
    for i, prm in enumerate(layer_params):
        x, xb = _encoder_layer(x, xb, p[i].reshape(B * S, -1), prm, tables, B, S)
    return x.reshape(B, S, D)


def kernel(x_prompt, x_sample, p_prompt, p_sample, w_in, mlstm_gate_b, mlstm_norm_g, gla_w_a2, gla_a_b,
           gla_norm_g, w_out, ln1_g, ln1_b, w_up, conv_w, conv_b, w_down, ln2_g, ln2_b, w_ple, w_ple_gate):
    weights = (w_in, mlstm_gate_b, mlstm_norm_g, gla_w_a2, gla_a_b, gla_norm_g, w_out, ln1_g, ln1_b,
               w_up, conv_w, conv_b, w_down, ln2_g, ln2_b, w_ple, w_ple_gate)
    layer_params = [_prepare_layer_params(*(w[i] for w in weights)) for i in range(w_in.shape[0])]
    return (_trunk(x_prompt, p_prompt, layer_params), _trunk(x_sample, p_sample, layer_params))
```

```python
import functools

import jax
import jax.numpy as jnp
from jax import lax
from jax.experimental import pallas as pl
from jax.experimental.pallas import tpu as pltpu

F32 = jnp.float32
BF16 = jnp.bfloat16

D_MODEL = 4096
HEAD_DIM = 128
MLSTM_HEADS = 8
ATTN_HEADS = 12
GLA_HEADS = 12
GLA_HEAD_K = 64
GLA_RANK = 16
GLA_TAU = 16.0
GLA_CHUNK = 64
ATTN_REACH = 64
ATTN_DILATIONS = (1, 4, 16)
ROPE_THETA = 10000.0
D_FF = 11008
PLE_DIM = 256
LN_EPS = 1e-5
HEAD_NORM_EPS = 1e-6
DEPTH = 2
DEEPNORM_ALPHA = (2 * DEPTH) ** 0.25

LANES = 128
VMEM_LIMIT = 56 * 1024 * 1024

G_GV, G_GR, G_GQ, G_GK = 0, 12, 24, 30
G_AQ = 36
G_MQ, G_MK, G_MV, G_MO = 48, 56, 64, 72
G_AK, G_AV = 80, 92
N_PROJ_GROUPS = 104
GATE_I_F, GATE_F_F, GATE_I_B, GATE_F_B, GATE_A_F, GATE_A_B = 0, 8, 16, 24, 32, 48

MLSTM_CHUNK = 256
GLA_STEP = 256
ATTN_TILE = 2048
ATTN_HALO = ATTN_REACH * ATTN_DILATIONS[-1]
ATTN_QB = 128
ATTN_KB = ATTN_QB + 2 * ATTN_REACH
ATTN_UNROLL = 8


def _cparams(sem):
    return pltpu.CompilerParams(dimension_semantics=sem, vmem_limit_bytes=VMEM_LIMIT)


def _log_sigmoid(x):
    return jnp.minimum(x, 0.0) - jnp.log1p(jnp.exp(-jnp.abs(x)))


def _sigmoid(x):
    return 1.0 / (1.0 + jnp.exp(-x))


def _dot(a, b):
    return jnp.dot(a, b, preferred_element_type=F32)


def _dot_nt(a, b):
    return lax.dot_general(a, b, (((1,), (1,)), ((), ())), preferred_element_type=F32)


def _dot_tn(a, b):
    return lax.dot_general(a, b, (((0,), (0,)), ((), ())), preferred_element_type=F32)


def _split3(x):
    hi = x.astype(BF16)
    r1 = x - hi.astype(F32)
    mid = r1.astype(BF16)
    lo = (r1 - mid.astype(F32)).astype(BF16)
    return hi, mid, lo


def _tri_dot(tri, x):
    hi, mid, lo = _split3(x)
    return _dot(tri, hi) + _dot(tri, mid) + _dot(tri, lo)


def _dot_tri(x, tri):
    hi, mid, lo = _split3(x)
    return _dot(hi, tri) + _dot(mid, tri) + _dot(lo, tri)


def _proj_kernel(x_ref, w_ref, o_ref):
    acc = _dot(x_ref[...], w_ref[...])
    for g in range(o_ref.shape[0]):
        o_ref[g] = acc[:, g * LANES:(g + 1) * LANES].astype(o_ref.dtype)


def proj_matmul(xb, w, tm=1024, tn=1024):
    T, K = xb.shape
    N = w.shape[1]
    return pl.pallas_call(
        _proj_kernel,
        grid=(T // tm, N // tn),
        in_specs=[pl.BlockSpec((tm, K), lambda i, j: (i, 0)),
                  pl.BlockSpec((K, tn), lambda i, j: (0, j))],
        out_specs=pl.BlockSpec((tn // LANES, tm, LANES), lambda i, j: (j, i, 0)),
        out_shape=jax.ShapeDtypeStruct((N // LANES, T, LANES), F32),
        name="proj_matmul",
        compiler_params=_cparams(("parallel", "parallel")),
    )(xb, w)


def _mm_kernel(x_ref, w_ref, o_ref):
    o_ref[...] = _dot(x_ref[...], w_ref[...]).astype(o_ref.dtype)


def matmul(xb, w, tm, tn, out_dtype=F32):
    T, K = xb.shape
    N = w.shape[1]
    return pl.pallas_call(
        _mm_kernel,
        grid=(T // tm, N // tn),
        in_specs=[pl.BlockSpec((tm, K), lambda i, j: (i, 0)),
                  pl.BlockSpec((K, tn), lambda i, j: (0, j))],
        out_specs=pl.BlockSpec((tm, tn), lambda i, j: (i, j)),
        out_shape=jax.ShapeDtypeStruct((T, N), out_dtype),
        name="matmul",
        compiler_params=_cparams(("parallel", "parallel")),
    )(xb, w)


def _mm_residual_kernel(x_ref, w_ref, r_ref, o_ref):
    o_ref[...] = DEEPNORM_ALPHA * r_ref[...] + _dot(x_ref[...], w_ref[...])


def matmul_residual(xb, w, res, tm, tn):
    T, K = xb.shape
    N = w.shape[1]
    return pl.pallas_call(
        _mm_residual_kernel,
        grid=(T // tm, N // tn),
        in_specs=[pl.BlockSpec((tm, K), lambda i, j: (i, 0)),
                  pl.BlockSpec((K, tn), lambda i, j: (0, j)),
                  pl.BlockSpec((tm, tn), lambda i, j: (i, j))],
        out_specs=pl.BlockSpec((tm, tn), lambda i, j: (i, j)),
        out_shape=jax.ShapeDtypeStruct((T, N), F32),
        name="matmul_residual",
        compiler_params=_cparams(("parallel", "parallel")),
    )(xb, w, res)


def _mix_out_kernel(ym_ref, ya_ref, yg_ref, w_ref, o_ref):
    parts = ([ym_ref[g] for g in range(ym_ref.shape[0])] + [ya_ref[g] for g in range(ya_ref.shape[0])]
             + [yg_ref[g] for g in range(yg_ref.shape[0])])
    y = jnp.concatenate(parts, axis=-1)
    o_ref[...] = _dot(y, w_ref[...])


def mix_out_matmul(ym, ya, yg, w, tm=512, tn=1024):
    T = ym.shape[1]
    N = w.shape[1]

    def gspec(a):
        return pl.BlockSpec((a.shape[0], tm, LANES), lambda i, j: (0, i, 0))

    return pl.pallas_call(
        _mix_out_kernel,
        grid=(T // tm, N // tn),
        in_specs=[gspec(ym), gspec(ya), gspec(yg), pl.BlockSpec((w.shape[0], tn), lambda i, j: (0, j))],
        out_specs=pl.BlockSpec((tm, tn), lambda i, j: (i, j)),
        out_shape=jax.ShapeDtypeStruct((T, N), F32),
        name="mix_out_matmul",
        compiler_params=_cparams(("parallel", "parallel")),
    )(ym, ya, yg, w)


def _gates_kernel(x_ref, w_ref, wt_ref, bc_ref, br_ref, pc_ref, cc_ref, pr_ref, cr_ref, *, chunk):
    x = x_ref[...]
    tm = x.shape[0]
    pc = _dot(x, w_ref[...]) + bc_ref[...]
    pr = _dot_nt(wt_ref[...], x) + br_ref[...]
    pc_ref[...] = pc
    pr_ref[...] = pr
    ls_c = _log_sigmoid(pc)
    ls_r = _log_sigmoid(pr)
    row = lax.broadcasted_iota(jnp.int32, (chunk, chunk), 0)
    col = lax.broadcasted_iota(jnp.int32, (chunk, chunk), 1)
    lower = (row >= col).astype(BF16)
    upper = (row <= col).astype(BF16)
    lane = lax.broadcasted_iota(jnp.int32, (chunk, LANES), 1)
    sub = lax.broadcasted_iota(jnp.int32, (LANES, chunk), 0)
    bwd_c = (lane >= GATE_F_B) & (lane < GATE_F_B + MLSTM_HEADS)
    bwd_r = (sub >= GATE_F_B) & (sub < GATE_F_B + MLSTM_HEADS)
    for c in range(tm // chunk):
        sl = slice(c * chunk, (c + 1) * chunk)
        lc = ls_c[sl]
        cc_ref[sl, :] = jnp.where(bwd_c, _tri_dot(upper, lc), _tri_dot(lower, lc))
        lr = ls_r[:, sl]
        cr_ref[:, sl] = jnp.where(bwd_r, _dot_tri(lr, lower), _dot_tri(lr, upper))


def gates_project(xb, w_small, w_small_t, bias_c, bias_r, chunk, tm=1024):
    T, K = xb.shape
    return pl.pallas_call(
        functools.partial(_gates_kernel, chunk=chunk),
        grid=(T // tm,),
        in_specs=[pl.BlockSpec((tm, K), lambda i: (i, 0)),
                  pl.BlockSpec((K, LANES), lambda i: (0, 0)),
                  pl.BlockSpec((LANES, K), lambda i: (0, 0)),
                  pl.BlockSpec((1, LANES), lambda i: (0, 0)),
                  pl.BlockSpec((LANES, 1), lambda i: (0, 0))],
        out_specs=[pl.BlockSpec((tm, LANES), lambda i: (i, 0)),
                   pl.BlockSpec((tm, LANES), lambda i: (i, 0)),
                   pl.BlockSpec((LANES, tm), lambda i: (0, i)),
                   pl.BlockSpec((LANES, tm), lambda i: (0, i))],
        out_shape=[jax.ShapeDtypeStruct((T, LANES), F32), jax.ShapeDtypeStruct((T, LANES), F32),
                   jax.ShapeDtypeStruct((LANES, T), F32), jax.ShapeDtypeStruct((LANES, T), F32)],
        name="gates_project",
        compiler_params=_cparams(("parallel",)),
    )(xb, w_small, w_small_t, bias_c, bias_r)


def _mlstm_kernel(*refs, reverse, final):
    if final:
        (q_ref, k_ref, v_ref, pc_ref, cc_ref, pr_ref, cr_ref, hf_ref, o_ref, g_ref,
         out_ref, ct_s, n_s, m_s) = refs
    else:
        q_ref, k_ref, v_ref, pc_ref, cc_ref, pr_ref, cr_ref, out_ref, ct_s, n_s, m_s = refs
    c = pl.program_id(1)
    H, L = q_ref.shape[0], q_ref.shape[1]

    @pl.when(c == 0)
    def _():
        ct_s[...] = jnp.zeros(ct_s.shape, F32)
        n_s[...] = jnp.zeros(n_s.shape, F32)
        m_s[...] = jnp.zeros(m_s.shape, F32)

    pc = pc_ref[...]
    cc = cc_ref[...]
    lane = lax.broadcasted_iota(jnp.int32, (L, LANES), 1)
    t_idx = lax.broadcasted_iota(jnp.int32, (L, L), 0)
    s_idx = lax.broadcasted_iota(jnp.int32, (L, L), 1)
    causal = (s_idx >= t_idx) if reverse else (s_idx <= t_idx)

    for h in range(H):
        qf = q_ref[h]
        kf = k_ref[h] * (HEAD_DIM ** -0.5)
        qb = qf.astype(BF16)
        kb = kf.astype(BF16)
        vb = v_ref[h].astype(BF16)

        icol = (GATE_I_B if reverse else GATE_I_F) + h
        fcol = (GATE_F_B if reverse else GATE_F_F) + h
        i_col = jnp.sum(jnp.where(lane == icol, pc, 0.0), axis=-1, keepdims=True)
        b_col = jnp.sum(jnp.where(lane == fcol, cc, 0.0), axis=-1, keepdims=True)
        i_row = pr_ref[icol:icol + 1, :]
        b_row = cr_ref[fcol:fcol + 1, :]
        r_row = i_row - b_row
        r_col = i_col - b_col
        m_prev = m_s[h][:, 0:1]

        r_mat = jnp.where(causal, r_row, -jnp.inf)
        m_row = jnp.maximum(jnp.max(r_mat, axis=-1, keepdims=True), m_prev)
        p = jnp.exp(r_mat - m_row) * _dot_nt(qb, kb)
        w_inter = jnp.exp(m_prev - m_row)
        ct = ct_s[h]
        num = w_inter * _dot(qb, ct.astype(BF16)) + _dot(p.astype(BF16), vb)
        nq = w_inter * jnp.sum(qf * n_s[h], axis=-1, keepdims=True) + jnp.sum(p, axis=-1, keepdims=True)
        den = jnp.maximum(jnp.abs(nq), jnp.exp(-(b_col + m_row)))
        hout = num / den

        g = b_row[:, 0:1] if reverse else b_row[:, L - 1:L]
        r_max = jnp.max(r_row, axis=-1, keepdims=True)
        m_loc = g + r_max
        m_new = jnp.maximum(g + m_prev, m_loc)
        s_old = jnp.exp(g + m_prev - m_new)
        s_new = jnp.exp(m_loc - m_new)
        kw = kf * jnp.exp(r_col - r_max)
        ct_loc = _dot_tn(kw.astype(BF16), vb)
        ct_s[h] = s_old * ct + s_new * ct_loc
        n_s[h] = s_old * n_s[h] + s_new * jnp.sum(kw, axis=0, keepdims=True)
        m_s[h] = jnp.broadcast_to(m_new, m_s.shape[1:])

        if final:
            hs = hf_ref[h] + hout
            mu = jnp.mean(hs, axis=-1, keepdims=True)
            var = jnp.mean(jnp.square(hs - mu), axis=-1, keepdims=True)
            hn = (hs - mu) * lax.rsqrt(var + HEAD_NORM_EPS) * g_ref[h]
            out_ref[h] = (_sigmoid(o_ref[h]) * hn).astype(out_ref.dtype)
        else:
            out_ref[h] = hout


def mlstm_direction(proj, gates, B, S, reverse, hf=None, norm_g=None):
    pc, cc, pr, cr = gates
    L = MLSTM_CHUNK
    nc = S // L
    T = B * S
    H = MLSTM_HEADS
    final = hf is not None

    def tok(b, c):
        return b * nc + (nc - 1 - c if reverse else c)

    def gspec(goff):
        return pl.BlockSpec((H, L, LANES), lambda b, c: (goff // H, tok(b, c), 0))

    in_specs = [gspec(G_MQ), gspec(G_MK), gspec(G_MV),
                pl.BlockSpec((L, LANES), lambda b, c: (tok(b, c), 0)),
                pl.BlockSpec((L, LANES), lambda b, c: (tok(b, c), 0)),
                pl.BlockSpec((LANES, L), lambda b, c: (0, tok(b, c))),
                pl.BlockSpec((LANES, L), lambda b, c: (0, tok(b, c)))]
    args = [proj, proj, proj, pc, cc, pr, cr]
    if final:
        in_specs += [gspec(0), gspec(G_MO), pl.BlockSpec((H, 1, LANES), lambda b, c: (0, 0, 0))]
        args += [hf, proj, norm_g]
    return pl.pallas_call(
        functools.partial(_mlstm_kernel, reverse=reverse, final=final),
        grid=(B, nc),
        in_specs=in_specs,
        out_specs=gspec(0),
        out_shape=jax.ShapeDtypeStruct((H, T, LANES), BF16 if final else F32),
        scratch_shapes=[pltpu.VMEM((H, HEAD_DIM, HEAD_DIM), F32), pltpu.VMEM((H, 1, HEAD_DIM), F32),
                        pltpu.VMEM((H, 1, LANES), F32)],
        name="mlstm_bwd" if reverse else "mlstm_fwd",
        compiler_params=_cparams(("arbitrary", "arbitrary")),
    )(*args)


def _gla_kernel(*refs, reverse, final):
    if final:
        (q_ref, k_ref, v_ref, pc_ref, wa_ref, ab_ref, of_ref, r_ref, g_ref, out_ref, st_s) = refs
    else:
        q_ref, k_ref, v_ref, pc_ref, wa_ref, ab_ref, out_ref, st_s = refs
    c = pl.program_id(1)
    NG, TS = q_ref.shape[0], q_ref.shape[1]
    L = GLA_CHUNK
    n_sub = TS // L

    @pl.when(c == 0)
    def _():
        st_s[...] = jnp.zeros(st_s.shape, F32)

    row = lax.broadcasted_iota(jnp.int32, (TS, TS), 0)
    col = lax.broadcasted_iota(jnp.int32, (TS, TS), 1)
    same_chunk = (row // L) == (col // L)
    causal = same_chunk & ((col >= row) if reverse else (col <= row))
    cum_mat = causal.astype(BF16)
    lane = lax.broadcasted_iota(jnp.int32, (TS, LANES), 1)
    head_mask = [lane < GLA_HEAD_K, lane >= GLA_HEAD_K]
    pcb = pc_ref[...].astype(BF16)
    order = list(range(n_sub - 1, -1, -1)) if reverse else list(range(n_sub))

    for gi in range(NG):
        la = _log_sigmoid(_dot(pcb, wa_ref[gi]) + ab_ref[gi]) * (1.0 / GLA_TAU)
        b = _tri_dot(cum_mat, la)

        def per_chunk_rows(idx, b=b):
            return jnp.concatenate([jnp.broadcast_to(b[s * L + idx:s * L + idx + 1, :], (L, LANES))
                                    for s in range(n_sub)], axis=0)

        g_full = per_chunk_rows(0 if reverse else L - 1)
        b_mid = per_chunk_rows(L // 2 - 1 if reverse else L // 2)
        qf = q_ref[gi] * (GLA_HEAD_K ** -0.5)
        kf = k_ref[gi]
        qd = qf * jnp.exp(b - b_mid)
        kd = (kf * jnp.exp(b_mid - b)).astype(BF16)
        kg = kf * jnp.exp(g_full - b)
        qe = qf * jnp.exp(b)

        for hh in range(2):
            hd = 2 * gi + hh
            vb = v_ref[hd].astype(BF16)
            a = jnp.where(causal, _dot_nt(jnp.where(head_mask[hh], qd, 0.0).astype(BF16), kd), 0.0)
            o_intra = _dot(a.astype(BF16), vb)
            kg_h = jnp.where(head_mask[hh], kg, 0.0).astype(BF16)
            qe_h = jnp.where(head_mask[hh], qe, 0.0).astype(BF16)
            st = st_s[hd]
            o_inter = [None] * n_sub
            for s in order:
                sl = slice(s * L, (s + 1) * L)
                o_inter[s] = _dot_nt(qe_h[sl], st.astype(BF16))
                decay = jnp.exp(b[s * L:s * L + 1, :] if reverse else b[s * L + L - 1:s * L + L, :])
                st = st * decay + _dot_tn(vb[sl], kg_h[sl])
            st_s[hd] = st
            o = o_intra + jnp.concatenate(o_inter, axis=0)
            if final:
                ot = of_ref[hd] + o
                on = ot * lax.rsqrt(jnp.mean(jnp.square(ot), axis=-1, keepdims=True) + HEAD_NORM_EPS) * g_ref[hd]
                rr = r_ref[hd]
                out_ref[hd] = (on * (rr * _sigmoid(rr))).astype(out_ref.dtype)
            else:
                out_ref[hd] = o


def gla_direction(proj, pc, wa, a_bias, B, S, reverse, of=None, norm_g=None):
    TS = GLA_STEP
    nc = S // TS
    T = B * S
    NG = GLA_HEADS // 2
    final = of is not None
    d = 1 if reverse else 0

    def tok(b, c):
        return b * nc + (nc - 1 - c if reverse else c)

    def gspec(goff, n):
        return pl.BlockSpec((n, TS, LANES), lambda b, c: (goff // n, tok(b, c), 0))

    in_specs = [gspec(G_GQ, NG), gspec(G_GK, NG), gspec(G_GV, GLA_HEADS),
                pl.BlockSpec((TS, LANES), lambda b, c: (tok(b, c), 0)),
                pl.BlockSpec((None, NG, LANES, LANES), lambda b, c: (d, 0, 0, 0)),
                pl.BlockSpec((None, NG, 1, LANES), lambda b, c: (d, 0, 0, 0))]
    args = [proj, proj, proj, pc, wa, a_bias]
    if final:
        in_specs += [gspec(0, GLA_HEADS), gspec(G_GR, GLA_HEADS),
                     pl.BlockSpec((GLA_HEADS, 1, LANES), lambda b, c: (0, 0, 0))]
        args += [of, proj, norm_g]
    return pl.pallas_call(
        functools.partial(_gla_kernel, reverse=reverse, final=final),
        grid=(B, nc),
        in_specs=in_specs,
        out_specs=gspec(0, GLA_HEADS),
        out_shape=jax.ShapeDtypeStruct((GLA_HEADS, T, LANES), BF16 if final else F32),
        scratch_shapes=[pltpu.VMEM((GLA_HEADS, HEAD_DIM, LANES), F32)],
        name="gla_bwd" if reverse else "gla_fwd",
        compiler_params=_cparams(("arbitrary", "arbitrary")),
    )(*args)


def _rope(t, cos2, sin2):
    return t * cos2 + pltpu.roll(t, shift=HEAD_DIM // 2, axis=1) * sin2


def _attn_kernel(q_ref, kp_ref, km_ref, kn_ref, vp_ref, vm_ref, vn_ref,
                 cp_ref, cm_ref, cn_ref, sp_ref, sm_ref, sn_ref, out_ref,
                 q_s, k_s, v_s, acc_s, m_s, l_s, *, seq_len):
    i = pl.program_id(1)
    TQ, HALO, QB, KB = ATTN_TILE, ATTN_HALO, ATTN_QB, ATTN_KB
    t0 = i * TQ

    q_s[...] = _rope(q_ref[...], cm_ref[...], sm_ref[...]) * (HEAD_DIM ** -0.5)
    k_s[0:HALO, :] = _rope(kp_ref[...], cp_ref[...], sp_ref[...])
    k_s[HALO:HALO + TQ, :] = _rope(km_ref[...], cm_ref[...], sm_ref[...])
    k_s[HALO + TQ:, :] = _rope(kn_ref[...], cn_ref[...], sn_ref[...])
    v_s[0:HALO, :] = vp_ref[...]
    v_s[HALO:HALO + TQ, :] = vm_ref[...]
    v_s[HALO + TQ:, :] = vn_ref[...]

    qi = lax.broadcasted_iota(jnp.int32, (QB, KB), 0)
    ki = lax.broadcasted_iota(jnp.int32, (QB, KB), 1)
    band = (ki >= qi) & (ki <= qi + 2 * ATTN_REACH)
    kcol = lax.broadcasted_iota(jnp.int32, (1, KB), 1)

    def block(branch, dil, q_start, k_start):
        def ds(start, n):
            return pl.ds(start, n) if dil == 1 else pl.ds(start, n, stride=dil)
        qb = q_s[ds(q_start, QB), :].astype(BF16)
        kb = k_s[ds(k_start, KB), :].astype(BF16)
        vb = v_s[ds(k_start, KB), :].astype(BF16)
        s = _dot_nt(qb, kb)
        kpos = t0 - HALO + k_start + dil * kcol
        s = jnp.where(band & (kpos >= 0) & (kpos < seq_len), s, -jnp.inf)
        m = jnp.max(s, axis=-1, keepdims=True)
        e = jnp.exp(s - m)
        acc_s[branch, ds(q_start, QB), :] = _dot(e.astype(BF16), vb)
        m_s[branch, ds(q_start, QB), :] = jnp.broadcast_to(m, (QB, LANES))
        l_s[branch, ds(q_start, QB), :] = jnp.broadcast_to(jnp.sum(e, axis=-1, keepdims=True), (QB, LANES))

    for branch, dil in enumerate(ATTN_DILATIONS):
        span = QB * dil
        n_outer = TQ // span

        def body(it, carry, branch=branch, dil=dil, span=span):
            base = (it // dil) * span
            r = it % dil
            q_start = base + r
            block(branch, dil, q_start, HALO + q_start - ATTN_REACH * dil)
            return carry

        lax.fori_loop(0, n_outer * dil, body, 0, unroll=ATTN_UNROLL)

    m_all = jnp.maximum(jnp.maximum(m_s[0], m_s[1]), m_s[2])
    num = jnp.zeros((TQ, LANES), F32)
    den = jnp.zeros((TQ, LANES), F32)
    for branch in range(len(ATTN_DILATIONS)):
        w = jnp.exp(m_s[branch] - m_all)
        num = num + w * acc_s[branch]
        den = den + w * l_s[branch]
    out_ref[...] = (num / den).astype(out_ref.dtype)


def dilated_attention(proj, cos2, sin2, B, S):
    TQ, HALO = ATTN_TILE, ATTN_HALO
    nt = S // TQ
    T = B * S
    hpt = TQ // HALO
    n_halo = T // HALO
    nb = len(ATTN_DILATIONS)

    def main(goff):
        return pl.BlockSpec((None, TQ, LANES), lambda b, i, h: (goff + h, b * nt + i, 0))

    def prev(goff):
        return pl.BlockSpec((None, HALO, LANES),
                            lambda b, i, h: (goff + h, jnp.maximum((b * nt + i) * hpt - 1, 0), 0))

    def nxt(goff):
        return pl.BlockSpec((None, HALO, LANES),
                            lambda b, i, h: (goff + h, jnp.minimum((b * nt + i + 1) * hpt, n_halo - 1), 0))

    t_prev = pl.BlockSpec((HALO, LANES), lambda b, i, h: (jnp.maximum(i * hpt - 1, 0), 0))
    t_main = pl.BlockSpec((TQ, LANES), lambda b, i, h: (i, 0))
    t_next = pl.BlockSpec((HALO, LANES), lambda b, i, h: (jnp.minimum((i + 1) * hpt, S // HALO - 1), 0))

    return pl.pallas_call(
        functools.partial(_attn_kernel, seq_len=S),
        grid=(B, nt, ATTN_HEADS),
        in_specs=[main(G_AQ), prev(G_AK), main(G_AK), nxt(G_AK), prev(G_AV), main(G_AV), nxt(G_AV),
                  t_prev, t_main, t_next, t_prev, t_main, t_next],
        out_specs=main(0),
        out_shape=jax.ShapeDtypeStruct((ATTN_HEADS, T, LANES), BF16),
        scratch_shapes=[pltpu.VMEM((TQ, LANES), F32), pltpu.VMEM((TQ + 2 * HALO, LANES), F32),
                        pltpu.VMEM((TQ + 2 * HALO, LANES), F32), pltpu.VMEM((nb, TQ, LANES), F32),
                        pltpu.VMEM((nb, TQ, LANES), F32), pltpu.VMEM((nb, TQ, LANES), F32)],
        name="dilated_attention",
        compiler_params=_cparams(("parallel", "parallel", "arbitrary")),
    )(proj, proj, proj, proj, proj, proj, proj, cos2, cos2, cos2, sin2, sin2, sin2)


def _ln_kernel(x_ref, y_ref, g_ref, b_ref, o_ref, ob_ref):
    z = DEEPNORM_ALPHA * x_ref[...] + y_ref[...]
    mu = jnp.mean(z, axis=-1, keepdims=True)
    zc = z - mu
    var = jnp.mean(jnp.square(zc), axis=-1, keepdims=True)
    out = zc * lax.rsqrt(var + LN_EPS) * g_ref[...] + b_ref[...]
    o_ref[...] = out
    ob_ref[...] = out.astype(BF16)


def residual_layer_norm(x, y, g, b, tr=256):
    T, D = x.shape
    row = pl.BlockSpec((tr, D), lambda i: (i, 0))
    vec = pl.BlockSpec((1, D), lambda i: (0, 0))
    return pl.pallas_call(
        _ln_kernel,
        grid=(T // tr,),
        in_specs=[row, row, vec, vec],
        out_specs=[row, row],
        out_shape=[jax.ShapeDtypeStruct((T, D), F32), jax.ShapeDtypeStruct((T, D), BF16)],
        name="residual_layer_norm",
        compiler_params=_cparams(("parallel",)),
    )(x, y, g, b)


FFN_HALO = 16


def _up_conv_gate_kernel(xp_ref, xm_ref, xn_ref, wg_ref, wv_ref, cwg_ref, cwv_ref, cbg_ref, cbv_ref, o_ref,
                         xs_ref, *, n_tiles):
    i = pl.program_id(1)
    j = pl.program_id(2)
    tm = xm_ref.shape[0]
    hr = xp_ref.shape[0]

    @pl.when(j == 0)
    def _():
        xs_ref[0:hr, :] = xp_ref[...] * (i > 0).astype(BF16)
        xs_ref[hr:hr + tm, :] = xm_ref[...]
        xs_ref[hr + tm:, :] = xn_ref[...] * (i < n_tiles - 1).astype(BF16)

    xs = xs_ref[...]

    def conv(w_ref, cw_ref, cb_ref):
        u = _dot(xs, w_ref[...])
        return (cb_ref[...] + u[hr - 1:hr - 1 + tm] * cw_ref[0:1, :] + u[hr:hr + tm] * cw_ref[1:2, :]
                + u[hr + 1:hr + 1 + tm] * cw_ref[2:3, :])

    gate = conv(wg_ref, cwg_ref, cbg_ref)
    val = conv(wv_ref, cwv_ref, cbv_ref)
    o_ref[...] = (gate * _sigmoid(gate) * val).astype(o_ref.dtype)


def up_conv_gate(xb, w_up, conv_w, conv_b, B, S, tm=1024, tf=256):
    T, K = xb.shape
    nt = S // tm
    nf = D_FF // tf
    hb = tm // FFN_HALO
    n_hb = T // FFN_HALO

    def wspec(off, r):
        return pl.BlockSpec((r, tf), lambda b, i, j: (0, off + j))

    return pl.pallas_call(
        functools.partial(_up_conv_gate_kernel, n_tiles=nt),
        grid=(B, nt, nf),
        in_specs=[pl.BlockSpec((FFN_HALO, K), lambda b, i, j: (jnp.maximum((b * nt + i) * hb - 1, 0), 0)),
                  pl.BlockSpec((tm, K), lambda b, i, j: (b * nt + i, 0)),
                  pl.BlockSpec((FFN_HALO, K), lambda b, i, j: (jnp.minimum((b * nt + i + 1) * hb, n_hb - 1), 0)),
                  wspec(0, K), wspec(nf, K), wspec(0, 3), wspec(nf, 3), wspec(0, 1), wspec(nf, 1)],
        out_specs=pl.BlockSpec((tm, tf), lambda b, i, j: (b * nt + i, j)),
        out_shape=jax.ShapeDtypeStruct((T, D_FF), BF16),
        scratch_shapes=[pltpu.VMEM((tm + 2 * FFN_HALO, K), BF16)],
        name="up_conv_gate",
        compiler_params=_cparams(("parallel", "parallel", "arbitrary")),
    )(xb, xb, xb, w_up, w_up, conv_w, conv_w, conv_b, conv_b)


def _ln_ple_kernel(z_ref, g_ref, b_ref, wg_ref, p_ref, wp_ref, o_ref, ob_ref, x_s, xb_s):
    j = pl.program_id(1)
    tn = wg_ref.shape[1]

    @pl.when(j == 0)
    def _():
        z = z_ref[...]
        mu = jnp.mean(z, axis=-1, keepdims=True)
        zc = z - mu
        var = jnp.mean(jnp.square(zc), axis=-1, keepdims=True)
        x = zc * lax.rsqrt(var + LN_EPS) * g_ref[...] + b_ref[...]
        xb_s[...] = x.astype(BF16)
        for jj in range(x_s.shape[0]):
            x_s[jj] = x[:, jj * tn:(jj + 1) * tn]

    gate = _sigmoid(_dot(xb_s[...], wg_ref[...]))
    emb = _dot(p_ref[...].astype(BF16), wp_ref[...])
    out = x_s[j] + gate * emb
    o_ref[...] = out
    ob_ref[...] = out.astype(BF16)


def ln_ple_update(z, g, b, p, w_gate, w_ple, tm=512, tn=512):
    T, D = z.shape
    vec = pl.BlockSpec((1, D), lambda i, j: (0, 0))
    return pl.pallas_call(
        _ln_ple_kernel,
        grid=(T // tm, D // tn),
        in_specs=[pl.BlockSpec((tm, D), lambda i, j: (i, 0)), vec, vec,
                  pl.BlockSpec((D, tn), lambda i, j: (0, j)),
                  pl.BlockSpec((tm, PLE_DIM), lambda i, j: (i, 0)),
                  pl.BlockSpec((PLE_DIM, tn), lambda i, j: (0, j))],
        out_specs=[pl.BlockSpec((tm, tn), lambda i, j: (i, j)), pl.BlockSpec((tm, tn), lambda i, j: (i, j))],
        out_shape=[jax.ShapeDtypeStruct((T, D), F32), jax.ShapeDtypeStruct((T, D), BF16)],
        scratch_shapes=[pltpu.VMEM((D // tn, tm, tn), F32), pltpu.VMEM((tm, D), BF16)],
        name="ln_ple_update",
        compiler_params=_cparams(("parallel", "arbitrary")),
    )(z, g, b, w_gate, p, w_ple)


def _regroup_columns(a):
    mw = MLSTM_HEADS * HEAD_DIM
    aw = ATTN_HEADS * HEAD_DIM
    gkw = GLA_HEADS * GLA_HEAD_K
    gvw = GLA_HEADS * HEAD_DIM
    sizes = [mw] * 4 + [4 * MLSTM_HEADS] + [aw] * 3 + [gkw] * 2 + [gvw] * 2 + [GLA_RANK] * 2
    offs = [0]
    for sz in sizes:
        offs.append(offs[-1] + sz)
    seg = [a[..., offs[i]:offs[i + 1]] for i in range(len(sizes))]
    mq, mk, mv, mo, mgate, aq, ak, av, gq, gk, gv, gr, ga_f, ga_b = seg
    big = jnp.concatenate([gv, gr, gq, gk, aq, mq, mk, mv, mo, ak, av], axis=-1)
    small = jnp.concatenate([mgate, ga_f, ga_b], axis=-1)
    return big, small


def _prepare_layer_params(w_in, mlstm_gate_b, mlstm_norm_g, gla_w_a2, gla_a_b, gla_norm_g, w_out,
                          ln1_g, ln1_b, w_up, conv_w, conv_b, w_down, ln2_g, ln2_b, w_ple, w_ple_gate):
    n_gate = 4 * MLSTM_HEADS
    w_big, w_small = _regroup_columns(w_in)
    w_big = w_big.astype(BF16)
    w_small = jnp.pad(w_small, ((0, 0), (0, LANES - w_small.shape[1]))).astype(BF16)
    bias = jnp.pad(mlstm_gate_b.reshape(-1), (0, LANES - n_gate)).astype(F32)
    wa = jnp.zeros((2, GLA_HEADS // 2, LANES, LANES), F32)
    a2 = gla_w_a2.reshape(2, GLA_RANK, GLA_HEADS // 2, LANES).transpose(0, 2, 1, 3)
    wa = wa.at[0, :, GATE_A_F:GATE_A_F + GLA_RANK, :].set(a2[0])
    wa = wa.at[1, :, GATE_A_B:GATE_A_B + GLA_RANK, :].set(a2[1])
    return dict(
        w_big=w_big, w_small=w_small, w_small_t=w_small.T,
        bias_c=bias.reshape(1, LANES), bias_r=bias.reshape(LANES, 1),
        mlstm_norm_g=mlstm_norm_g.reshape(MLSTM_HEADS, 1, HEAD_DIM),
        wa=wa.astype(BF16), a_bias=gla_a_b.reshape(2, GLA_HEADS // 2, 1, LANES),
        gla_norm_g=gla_norm_g.reshape(GLA_HEADS, 1, HEAD_DIM),
        w_out=w_out.astype(BF16), ln1_g=ln1_g.reshape(1, -1), ln1_b=ln1_b.reshape(1, -1),
        w_up=w_up.astype(BF16), conv_w=conv_w, conv_b=conv_b.reshape(1, -1), w_down=w_down.astype(BF16),
        ln2_g=ln2_g.reshape(1, -1), ln2_b=ln2_b.reshape(1, -1),
        w_ple=w_ple.astype(BF16), w_ple_gate=w_ple_gate.astype(BF16))


def _rope_tables(S):
    half = HEAD_DIM // 2
    inv = ROPE_THETA ** (-jnp.arange(half, dtype=F32) / half)
    ang = jnp.arange(S, dtype=F32)[:, None] * inv[None, :]
    cos, sin = jnp.cos(ang), jnp.sin(ang)
    return jnp.concatenate([cos, cos], axis=-1), jnp.concatenate([-sin, sin], axis=-1)


def _encoder_layer(x, xb, p_i, prm, tables, B, S):
    proj = proj_matmul(xb, prm['w_big'])
    gates = gates_project(xb, prm['w_small'], prm['w_small_t'], prm['bias_c'], prm['bias_r'], MLSTM_CHUNK)
    hf = mlstm_direction(proj, gates, B, S, reverse=False)
    y_m = mlstm_direction(proj, gates, B, S, reverse=True, hf=hf, norm_g=prm['mlstm_norm_g'])
    y_a = dilated_attention(proj, tables[0], tables[1], B, S)
    of = gla_direction(proj, gates[0], prm['wa'], prm['a_bias'], B, S, reverse=False)
    y_g = gla_direction(proj, gates[0], prm['wa'], prm['a_bias'], B, S, reverse=True, of=of,
                        norm_g=prm['gla_norm_g'])
    mix = mix_out_matmul(y_m, y_a, y_g, prm['w_out'])
    x, xb = residual_layer_norm(x, mix, prm['ln1_g'], prm['ln1_b'])
    hmid = up_conv_gate(xb, prm['w_up'], prm['conv_w'], prm['conv_b'], B, S)
    z = matmul_residual(hmid, prm['w_down'], x, tm=512, tn=512)
    return ln_ple_update(z, prm['ln2_g'], prm['ln2_b'], p_i, prm['w_ple_gate'], prm['w_ple'])


def _trunk(x, p, layer_params):
    B, S, D = x.shape
    tables = _rope_tables(S)
    x = x.reshape(B * S, D)
    xb = x.astype(BF16)
    for i, prm in enumerate(layer_params):
        x, xb = _encoder_layer(x, xb, p[i].reshape(B * S, -1), prm, tables, B, S)
    return x.reshape(B, S, D)


def kernel(x_prompt, x_sample, p_prompt, p_sample, w_in, mlstm_gate_b, mlstm_norm_g, gla_w_a2, gla_a_b,
           gla_norm_g, w_out, ln1_g, ln1_b, w_up, conv_w, conv_b, w_down, ln2_g, ln2_b, w_ple, w_ple_gate):
    weights = (w_in, mlstm_gate_b, mlstm_norm_g, gla_w_a2, gla_a_b, gla_norm_g, w_out, ln1_g, ln1_b,
               w_up, conv_w, conv_b, w_down, ln2_g, ln2_b, w_ple, w_ple_gate)
    layer_params = [_prepare_layer_params(*(w[i] for w in weights)) for i in range(w_in.shape[0])]
    return (_trunk(x_prompt, p_prompt, layer_params), _trunk(x_sample, p_sample, layer_params))
```

```python
import functools

import jax
import jax.numpy as jnp
from jax import lax
from jax.experimental import pallas as pl
from jax.experimental.pallas import tpu as pltpu

F32 = jnp.float32
BF16 = jnp.bfloat16

D_MODEL = 4096
HEAD_DIM = 128
MLSTM_HEADS = 8
ATTN_HEADS = 12
GLA_HEADS = 12
GLA_HEAD_K = 64
GLA_RANK = 16
GLA_TAU = 16.0
GLA_CHUNK = 64
ATTN_REACH = 64
ATTN_DILATIONS = (1, 4, 16)
ROPE_THETA = 10000.0
D_FF = 11008
PLE_DIM = 256
LN_EPS = 1e-5
HEAD_NORM_EPS = 1e-6
DEPTH = 2
DEEPNORM_ALPHA = (2 * DEPTH) ** 0.25

LANES = 128
VMEM_LIMIT = 56 * 1024 * 1024

G_GV, G_GR, G_GQ, G_GK = 0, 12, 24, 30
G_AQ = 36
G_MQ, G_MK, G_MV, G_MO = 48, 56, 64, 72
G_AK, G_AV = 80, 92
N_PROJ_GROUPS = 104
GATE_I_F, GATE_F_F, GATE_I_B, GATE_F_B, GATE_A_F, GATE_A_B = 0, 8, 16, 24, 32, 48

MLSTM_CHUNK = 256
GLA_STEP = 256
ATTN_TILE = 2048
ATTN_HALO = ATTN_REACH * ATTN_DILATIONS[-1]
ATTN_QB = 128
ATTN_KB = ATTN_QB + 2 * ATTN_REACH
ATTN_UNROLL = 8


def _cparams(sem):
    return pltpu.CompilerParams(dimension_semantics=sem, vmem_limit_bytes=VMEM_LIMIT)


def _log_sigmoid(x):
    return jnp.minimum(x, 0.0) - jnp.log1p(jnp.exp(-jnp.abs(x)))


def _sigmoid(x):
    return 1.0 / (1.0 + jnp.exp(-x))


def _dot(a, b):
    return jnp.dot(a, b, preferred_element_type=F32)


def _dot_nt(a, b):
    return lax.dot_general(a, b, (((1,), (1,)), ((), ())), preferred_element_type=F32)


def _dot_tn(a, b):
    return lax.dot_general(a, b, (((0,), (0,)), ((), ())), preferred_element_type=F32)


def _split3(x):
    hi = x.astype(BF16)
    r1 = x - hi.astype(F32)
    mid = r1.astype(BF16)
    lo = (r1 - mid.astype(F32)).astype(BF16)
    return hi, mid, lo


def _tri_dot(tri, x):
    hi, mid, lo = _split3(x)
    return _dot(tri, hi) + _dot(tri, mid) + _dot(tri, lo)


def _dot_tri(x, tri):
    hi, mid, lo = _split3(x)
    return _dot(hi, tri) + _dot(mid, tri) + _dot(lo, tri)


def _proj_kernel(x_ref, w_ref, o_ref):
    acc = _dot(x_ref[...], w_ref[...])
    for g in range(o_ref.shape[0]):
        o_ref[g] = acc[:, g * LANES:(g + 1) * LANES].astype(o_ref.dtype)


def proj_matmul(xb, w, tm=1024, tn=1024):
    T, K = xb.shape
    N = w.shape[1]
    return pl.pallas_call(
        _proj_kernel,
        grid=(T // tm, N // tn),
        in_specs=[pl.BlockSpec((tm, K), lambda i, j: (i, 0)),
                  pl.BlockSpec((K, tn), lambda i, j: (0, j))],
        out_specs=pl.BlockSpec((tn // LANES, tm, LANES), lambda i, j: (j, i, 0)),
        out_shape=jax.ShapeDtypeStruct((N // LANES, T, LANES), F32),
        name="proj_matmul",
        compiler_params=_cparams(("parallel", "parallel")),
    )(xb, w)


def _mm_kernel(x_ref, w_ref, o_ref):
    o_ref[...] = _dot(x_ref[...], w_ref[...]).astype(o_ref.dtype)


def matmul(xb, w, tm, tn, out_dtype=F32):
    T, K = xb.shape
    N = w.shape[1]
    return pl.pallas_call(
        _mm_kernel,
        grid=(T // tm, N // tn),
        in_specs=[pl.BlockSpec((tm, K), lambda i, j: (i, 0)),
                  pl.BlockSpec((K, tn), lambda i, j: (0, j))],
        out_specs=pl.BlockSpec((tm, tn), lambda i, j: (i, j)),
        out_shape=jax.ShapeDtypeStruct((T, N), out_dtype),
        name="matmul",
        compiler_params=_cparams(("parallel", "parallel")),
    )(xb, w)


def _mm_residual_kernel(x_ref, w_ref, r_ref, o_ref):
    o_ref[...] = DEEPNORM_ALPHA * r_ref[...] + _dot(x_ref[...], w_ref[...])


def matmul_residual(xb, w, res, tm, tn):
    T, K = xb.shape
    N = w.shape[1]
    return pl.pallas_call(
        _mm_residual_kernel,
        grid=(T // tm, N // tn),
        in_specs=[pl.BlockSpec((tm, K), lambda i, j: (i, 0)),
                  pl.BlockSpec((K, tn), lambda i, j: (0, j)),
                  pl.BlockSpec((tm, tn), lambda i, j: (i, j))],
        out_specs=pl.BlockSpec((tm, tn), lambda i, j: (i, j)),
        out_shape=jax.ShapeDtypeStruct((T, N), F32),
        name="matmul_residual",
        compiler_params=_cparams(("parallel", "parallel")),
    )(xb, w, res)


def _mix_out_kernel(ym_ref, ya_ref, yg_ref, w_ref, o_ref):
    parts = ([ym_ref[g] for g in range(ym_ref.shape[0])] + [ya_ref[g] for g in range(ya_ref.shape[0])]
             + [yg_ref[g] for g in range(yg_ref.shape[0])])
    y = jnp.concatenate(parts, axis=-1)
    o_ref[...] = _dot(y, w_ref[...])


def mix_out_matmul(ym, ya, yg, w, tm=512, tn=1024):
    T = ym.shape[1]
    N = w.shape[1]

    def gspec(a):
        return pl.BlockSpec((a.shape[0], tm, LANES), lambda i, j: (0, i, 0))

    return pl.pallas_call(
        _mix_out_kernel,
        grid=(T // tm, N // tn),
        in_specs=[gspec(ym), gspec(ya), gspec(yg), pl.BlockSpec((w.shape[0], tn), lambda i, j: (0, j))],
        out_specs=pl.BlockSpec((tm, tn), lambda i, j: (i, j)),
        out_shape=jax.ShapeDtypeStruct((T, N), F32),
        name="mix_out_matmul",
        compiler_params=_cparams(("parallel", "parallel")),
    )(ym, ya, yg, w)


def _gates_kernel(x_ref, w_ref, wt_ref, bc_ref, br_ref, pc_ref, cc_ref, pr_ref, cr_ref, *, chunk):
    x = x_ref[...]
    tm = x.shape[0]
    pc = _dot(x, w_ref[...]) + bc_ref[...]
    pr = _dot_nt(wt_ref[...], x) + br_ref[...]
    pc_ref[...] = pc
    pr_ref[...] = pr
    ls_c = _log_sigmoid(pc)
    ls_r = _log_sigmoid(pr)
    row = lax.broadcasted_iota(jnp.int32, (chunk, chunk), 0)
    col = lax.broadcasted_iota(jnp.int32, (chunk, chunk), 1)
    lower = (row >= col).astype(BF16)
    upper = (row <= col).astype(BF16)
    lane = lax.broadcasted_iota(jnp.int32, (chunk, LANES), 1)
    sub = lax.broadcasted_iota(jnp.int32, (LANES, chunk), 0)
    bwd_c = (lane >= GATE_F_B) & (lane < GATE_F_B + MLSTM_HEADS)
    bwd_r = (sub >= GATE_F_B) & (sub < GATE_F_B + MLSTM_HEADS)
    for c in range(tm // chunk):
        sl = slice(c * chunk, (c + 1) * chunk)
        lc = ls_c[sl]
        cc_ref[sl, :] = jnp.where(bwd_c, _tri_dot(upper, lc), _tri_dot(lower, lc))
        lr = ls_r[:, sl]
        cr_ref[:, sl] = jnp.where(bwd_r, _dot_tri(lr, lower), _dot_tri(lr, upper))


def gates_project(xb, w_small, w_small_t, bias_c, bias_r, chunk, tm=1024):
    T, K = xb.shape
    return pl.pallas_call(
        functools.partial(_gates_kernel, chunk=chunk),
        grid=(T // tm,),
        in_specs=[pl.BlockSpec((tm, K), lambda i: (i, 0)),
                  pl.BlockSpec((K, LANES), lambda i: (0, 0)),
                  pl.BlockSpec((LANES, K), lambda i: (0, 0)),
                  pl.BlockSpec((1, LANES), lambda i: (0, 0)),
                  pl.BlockSpec((LANES, 1), lambda i: (0, 0))],
        out_specs=[pl.BlockSpec((tm, LANES), lambda i: (i, 0)),
                   pl.BlockSpec((tm, LANES), lambda i: (i, 0)),
                   pl.BlockSpec((LANES, tm), lambda i: (0, i)),
                   pl.BlockSpec((LANES, tm), lambda i: (0, i))],
        out_shape=[jax.ShapeDtypeStruct((T, LANES), F32), jax.ShapeDtypeStruct((T, LANES), F32),
                   jax.ShapeDtypeStruct((LANES, T), F32), jax.ShapeDtypeStruct((LANES, T), F32)],
        name="gates_project",
        compiler_params=_cparams(("parallel",)),
    )(xb, w_small, w_small_t, bias_c, bias_r)


def _mlstm_kernel(*refs, reverse, final):
    if final:
        (q_ref, k_ref, v_ref, pc_ref, cc_ref, pr_ref, cr_ref, hf_ref, o_ref, g_ref,
         out_ref, ct_s, n_s, m_s) = refs
    else:
        q_ref, k_ref, v_ref, pc_ref, cc_ref, pr_ref, cr_ref, out_ref, ct_s, n_s, m_s = refs
    c = pl.program_id(1)
    H, L = q_ref.shape[0], q_ref.shape[1]

    @pl.when(c == 0)
    def _():
        ct_s[...] = jnp.zeros(ct_s.shape, F32)
        n_s[...] = jnp.zeros(n_s.shape, F32)
        m_s[...] = jnp.zeros(m_s.shape, F32)

    pc = pc_ref[...]
    cc = cc_ref[...]
    lane = lax.broadcasted_iota(jnp.int32, (L, LANES), 1)
    t_idx = lax.broadcasted_iota(jnp.int32, (L, L), 0)
    s_idx = lax.broadcasted_iota(jnp.int32, (L, L), 1)
    causal = (s_idx >= t_idx) if reverse else (s_idx <= t_idx)

    heads = range(H)
    icol = [(GATE_I_B if reverse else GATE_I_F) + h for h in heads]
    fcol = [(GATE_F_B if reverse else GATE_F_F) + h for h in heads]
    qf = [q_ref[h] for h in heads]
    kf = [k_ref[h] * (HEAD_DIM ** -0.5) for h in heads]
    qb = [x.astype(BF16) for x in qf]
    kb = [x.astype(BF16) for x in kf]
    vb = [v_ref[h].astype(BF16) for h in heads]
    ct = [ct_s[h] for h in heads]
    n_prev = [n_s[h] for h in heads]
    m_prev = [m_s[h][:, 0:1] for h in heads]

    i_col = [jnp.sum(jnp.where(lane == icol[h], pc, 0.0), axis=-1, keepdims=True) for h in heads]
    b_col = [jnp.sum(jnp.where(lane == fcol[h], cc, 0.0), axis=-1, keepdims=True) for h in heads]
    r_col = [i_col[h] - b_col[h] for h in heads]
    b_row = [cr_ref[fcol[h]:fcol[h] + 1, :] for h in heads]
    r_row = [pr_ref[icol[h]:icol[h] + 1, :] - b_row[h] for h in heads]

    r_mat = [jnp.where(causal, r_row[h], -jnp.inf) for h in heads]
    m_row = [jnp.maximum(jnp.max(r_mat[h], axis=-1, keepdims=True), m_prev[h]) for h in heads]
    scores = [_dot_nt(qb[h], kb[h]) for h in heads]
    inter = [_dot(qb[h], ct[h].astype(BF16)) for h in heads]
    p = [jnp.exp(r_mat[h] - m_row[h]) * scores[h] for h in heads]
    w_inter = [jnp.exp(m_prev[h] - m_row[h]) for h in heads]
    pv = [_dot(p[h].astype(BF16), vb[h]) for h in heads]
    nq = [w_inter[h] * jnp.sum(qf[h] * n_prev[h], axis=-1, keepdims=True)
          + jnp.sum(p[h], axis=-1, keepdims=True) for h in heads]
    den = [jnp.maximum(jnp.abs(nq[h]), jnp.exp(-(b_col[h] + m_row[h]))) for h in heads]
    hout = [(w_inter[h] * inter[h] + pv[h]) / den[h] for h in heads]

    g = [b_row[h][:, 0:1] if reverse else b_row[h][:, L - 1:L] for h in heads]
    r_max = [jnp.max(r_row[h], axis=-1, keepdims=True) for h in heads]
    m_loc = [g[h] + r_max[h] for h in heads]
    m_new = [jnp.maximum(g[h] + m_prev[h], m_loc[h]) for h in heads]
    s_old = [jnp.exp(g[h] + m_prev[h] - m_new[h]) for h in heads]
    s_new = [jnp.exp(m_loc[h] - m_new[h]) for h in heads]
    kw = [kf[h] * jnp.exp(r_col[h] - r_max[h]) for h in heads]
    ct_loc = [_dot_tn(kw[h].astype(BF16), vb[h]) for h in heads]
    for h in heads:
        ct_s[h] = s_old[h] * ct[h] + s_new[h] * ct_loc[h]
        n_s[h] = s_old[h] * n_prev[h] + s_new[h] * jnp.sum(kw[h], axis=0, keepdims=True)
        m_s[h] = jnp.broadcast_to(m_new[h], m_s.shape[1:])

    for h in heads:
        if final:
            hs = hf_ref[h] + hout[h]
            mu = jnp.mean(hs, axis=-1, keepdims=True)
            var = jnp.mean(jnp.square(hs - mu), axis=-1, keepdims=True)
            hn = (hs - mu) * lax.rsqrt(var + HEAD_NORM_EPS) * g_ref[h]
            out_ref[h] = (_sigmoid(o_ref[h]) * hn).astype(out_ref.dtype)
        else:
            out_ref[h] = hout[h]


def mlstm_direction(proj, gates, B, S, reverse, hf=None, norm_g=None):
    pc, cc, pr, cr = gates
    L = MLSTM_CHUNK
    nc = S // L
    T = B * S
    H = MLSTM_HEADS
    final = hf is not None

    def tok(b, c):
        return b * nc + (nc - 1 - c if reverse else c)

    def gspec(goff):
        return pl.BlockSpec((H, L, LANES), lambda b, c: (goff // H, tok(b, c), 0))

    in_specs = [gspec(G_MQ), gspec(G_MK), gspec(G_MV),
                pl.BlockSpec((L, LANES), lambda b, c: (tok(b, c), 0)),
                pl.BlockSpec((L, LANES), lambda b, c: (tok(b, c), 0)),
                pl.BlockSpec((LANES, L), lambda b, c: (0, tok(b, c))),
                pl.BlockSpec((LANES, L), lambda b, c: (0, tok(b, c)))]
    args = [proj, proj, proj, pc, cc, pr, cr]
    if final:
        in_specs += [gspec(0), gspec(G_MO), pl.BlockSpec((H, 1, LANES), lambda b, c: (0, 0, 0))]
        args += [hf, proj, norm_g]
    return pl.pallas_call(
        functools.partial(_mlstm_kernel, reverse=reverse, final=final),
        grid=(B, nc),
        in_specs=in_specs,
        out_specs=gspec(0),
        out_shape=jax.ShapeDtypeStruct((H, T, LANES), BF16 if final else F32),
        scratch_shapes=[pltpu.VMEM((H, HEAD_DIM, HEAD_DIM), F32), pltpu.VMEM((H, 1, HEAD_DIM), F32),
                        pltpu.VMEM((H, 1, LANES), F32)],
        name="mlstm_bwd" if reverse else "mlstm_fwd",
        compiler_params=_cparams(("arbitrary", "arbitrary")),
    )(*args)


def _gla_kernel(*refs, reverse, final):
    if final:
        (q_ref, k_ref, v_ref, pc_ref, wa_ref, ab_ref, of_ref, r_ref, g_ref, out_ref, st_s) = refs
    else:
        q_ref, k_ref, v_ref, pc_ref, wa_ref, ab_ref, out_ref, st_s = refs
    c = pl.program_id(1)
    NG, TS = q_ref.shape[0], q_ref.shape[1]
    L = GLA_CHUNK
    n_sub = TS // L

    @pl.when(c == 0)
    def _():
        st_s[...] = jnp.zeros(st_s.shape, F32)

    row = lax.broadcasted_iota(jnp.int32, (TS, TS), 0)
    col = lax.broadcasted_iota(jnp.int32, (TS, TS), 1)
    same_chunk = (row // L) == (col // L)
    causal = same_chunk & ((col >= row) if reverse else (col <= row))
    cum_mat = causal.astype(BF16)
    lane = lax.broadcasted_iota(jnp.int32, (TS, LANES), 1)
    head_mask = [lane < GLA_HEAD_K, lane >= GLA_HEAD_K]
    pcb = pc_ref[...].astype(BF16)
    order = list(range(n_sub - 1, -1, -1)) if reverse else list(range(n_sub))
    pairs = range(NG)
    heads = range(2 * NG)


    la = [_log_sigmoid(_dot(pcb, wa_ref[gi]) + ab_ref[gi]) * (1.0 / GLA_TAU) for gi in pairs]
    split = [_split3(x) for x in la]
    b = [_dot(cum_mat, hi) + _dot(cum_mat, mid) + _dot(cum_mat, lo) for hi, mid, lo in split]

    def per_chunk_rows(x, idx):
        return jnp.concatenate([jnp.broadcast_to(x[s * L + idx:s * L + idx + 1, :], (L, LANES))
                                for s in range(n_sub)], axis=0)

    g_full = [per_chunk_rows(x, 0 if reverse else L - 1) for x in b]
    b_mid = [per_chunk_rows(x, L // 2 - 1 if reverse else L // 2) for x in b]
    qf = [q_ref[gi] * (GLA_HEAD_K ** -0.5) for gi in pairs]
    kf = [k_ref[gi] for gi in pairs]
    qd = [qf[gi] * jnp.exp(b[gi] - b_mid[gi]) for gi in pairs]
    kd = [(kf[gi] * jnp.exp(b_mid[gi] - b[gi])).astype(BF16) for gi in pairs]
    kg = [kf[gi] * jnp.exp(g_full[gi] - b[gi]) for gi in pairs]
    qe = [qf[gi] * jnp.exp(b[gi]) for gi in pairs]
    decay = [[jnp.exp(b[gi][s * L:s * L + 1, :] if reverse else b[gi][s * L + L - 1:s * L + L, :])
              for s in range(n_sub)] for gi in pairs]

    vb = [v_ref[hd].astype(BF16) for hd in heads]
    qd_h = [jnp.where(head_mask[hd % 2], qd[hd // 2], 0.0).astype(BF16) for hd in heads]
    kg_h = [jnp.where(head_mask[hd % 2], kg[hd // 2], 0.0).astype(BF16) for hd in heads]
    qe_h = [jnp.where(head_mask[hd % 2], qe[hd // 2], 0.0).astype(BF16) for hd in heads]
    a = [_dot_nt(qd_h[hd], kd[hd // 2]) for hd in heads]
    a = [jnp.where(causal, x, 0.0).astype(BF16) for x in a]
    o_intra = [_dot(a[hd], vb[hd]) for hd in heads]

    st = [st_s[hd] for hd in heads]
    o_inter = [[None] * n_sub for _ in heads]
    for s in order:
        sl = slice(s * L, (s + 1) * L)
        for hd in heads:
            o_inter[hd][s] = _dot_nt(qe_h[hd][sl], st[hd].astype(BF16))
        st_loc = [_dot_tn(vb[hd][sl], kg_h[hd][sl]) for hd in heads]
        st = [st[hd] * decay[hd // 2][s] + st_loc[hd] for hd in heads]
    for hd in heads:
        st_s[hd] = st[hd]

    for hd in heads:
        o = o_intra[hd] + jnp.concatenate(o_inter[hd], axis=0)
        if final:
            ot = of_ref[hd] + o
            on = ot * lax.rsqrt(jnp.mean(jnp.square(ot), axis=-1, keepdims=True) + HEAD_NORM_EPS) * g_ref[hd]
            rr = r_ref[hd]
            out_ref[hd] = (on * (rr * _sigmoid(rr))).astype(out_ref.dtype)
        else:
            out_ref[hd] = o


def gla_direction(proj, pc, wa, a_bias, B, S, reverse, of=None, norm_g=None):
    TS = GLA_STEP
    nc = S // TS
    T = B * S
    NG = GLA_HEADS // 2
    final = of is not None
    d = 1 if reverse else 0

    def tok(b, c):
        return b * nc + (nc - 1 - c if reverse else c)

    def gspec(goff, n):
        return pl.BlockSpec((n, TS, LANES), lambda b, c: (goff // n, tok(b, c), 0))

    in_specs = [gspec(G_GQ, NG), gspec(G_GK, NG), gspec(G_GV, GLA_HEADS),
                pl.BlockSpec((TS, LANES), lambda b, c: (tok(b, c), 0)),
                pl.BlockSpec((None, NG, LANES, LANES), lambda b, c: (d, 0, 0, 0)),
                pl.BlockSpec((None, NG, 1, LANES), lambda b, c: (d, 0, 0, 0))]
    args = [proj, proj, proj, pc, wa, a_bias]
    if final:
        in_specs += [gspec(0, GLA_HEADS), gspec(G_GR, GLA_HEADS),
                     pl.BlockSpec((GLA_HEADS, 1, LANES), lambda b, c: (0, 0, 0))]
        args += [of, proj, norm_g]
    return pl.pallas_call(
        functools.partial(_gla_kernel, reverse=reverse, final=final),
        grid=(B, nc),
        in_specs=in_specs,
        out_specs=gspec(0, GLA_HEADS),
        out_shape=jax.ShapeDtypeStruct((GLA_HEADS, T, LANES), BF16 if final else F32),
        scratch_shapes=[pltpu.VMEM((GLA_HEADS, HEAD_DIM, LANES), F32)],
        name="gla_bwd" if reverse else "gla_fwd",
        compiler_params=_cparams(("arbitrary", "arbitrary")),
    )(*args)


def _rope(t, cos2, sin2):
    return t * cos2 + pltpu.roll(t, shift=HEAD_DIM // 2, axis=1) * sin2


def _attn_kernel(q_ref, kp_ref, km_ref, kn_ref, vp_ref, vm_ref, vn_ref,
                 cp_ref, cm_ref, cn_ref, sp_ref, sm_ref, sn_ref, out_ref,
                 q_s, k_s, v_s, acc_s, m_s, l_s, *, seq_len):
    i = pl.program_id(1)
    TQ, HALO, QB, KB = ATTN_TILE, ATTN_HALO, ATTN_QB, ATTN_KB
    t0 = i * TQ

    q_s[...] = _rope(q_ref[...], cm_ref[...], sm_ref[...]) * (HEAD_DIM ** -0.5)
    k_s[0:HALO, :] = _rope(kp_ref[...], cp_ref[...], sp_ref[...])
    k_s[HALO:HALO + TQ, :] = _rope(km_ref[...], cm_ref[...], sm_ref[...])
    k_s[HALO + TQ:, :] = _rope(kn_ref[...], cn_ref[...], sn_ref[...])
    v_s[0:HALO, :] = vp_ref[...]
    v_s[HALO:HALO + TQ, :] = vm_ref[...]
    v_s[HALO + TQ:, :] = vn_ref[...]

    qi = lax.broadcasted_iota(jnp.int32, (QB, KB), 0)
    ki = lax.broadcasted_iota(jnp.int32, (QB, KB), 1)
    band = (ki >= qi) & (ki <= qi + 2 * ATTN_REACH)
    kcol = lax.broadcasted_iota(jnp.int32, (1, KB), 1)

    def block_group(branch, dil, q_starts):
        def ds(start, n):
            return pl.ds(start, n) if dil == 1 else pl.ds(start, n, stride=dil)
        n = range(len(q_starts))
        k_starts = [HALO + qs - ATTN_REACH * dil for qs in q_starts]
        qb = [q_s[ds(q_starts[u], QB), :].astype(BF16) for u in n]
        kb = [k_s[ds(k_starts[u], KB), :].astype(BF16) for u in n]
        vb = [v_s[ds(k_starts[u], KB), :].astype(BF16) for u in n]
        s = [_dot_nt(qb[u], kb[u]) for u in n]
        kpos = [t0 - HALO + k_starts[u] + dil * kcol for u in n]
        s = [jnp.where(band & (kpos[u] >= 0) & (kpos[u] < seq_len), s[u], -jnp.inf) for u in n]
        m = [jnp.max(s[u], axis=-1, keepdims=True) for u in n]
        e = [jnp.exp(s[u] - m[u]) for u in n]
        acc = [_dot(e[u].astype(BF16), vb[u]) for u in n]
        for u in n:
            acc_s[branch, ds(q_starts[u], QB), :] = acc[u]
            m_s[branch, ds(q_starts[u], QB), :] = jnp.broadcast_to(m[u], (QB, LANES))
            l_s[branch, ds(q_starts[u], QB), :] = jnp.broadcast_to(jnp.sum(e[u], axis=-1, keepdims=True),
                                                                    (QB, LANES))

    for branch, dil in enumerate(ATTN_DILATIONS):
        span = QB * dil
        n_blocks = (TQ // span) * dil

        def body(g, carry, branch=branch, dil=dil, span=span):
            q_starts = []
            for u in range(ATTN_UNROLL):
                it = g * ATTN_UNROLL + u
                q_starts.append((it // dil) * span + it % dil)
            block_group(branch, dil, q_starts)
            return carry

        lax.fori_loop(0, n_blocks // ATTN_UNROLL, body, 0)

    m_all = jnp.maximum(jnp.maximum(m_s[0], m_s[1]), m_s[2])
    num = jnp.zeros((TQ, LANES), F32)
    den = jnp.zeros((TQ, LANES), F32)
    for branch in range(len(ATTN_DILATIONS)):
        w = jnp.exp(m_s[branch] - m_all)
        num = num + w * acc_s[branch]
        den = den + w * l_s[branch]
    out_ref[...] = (num / den).astype(out_ref.dtype)


def dilated_attention(proj, cos2, sin2, B, S):
    TQ, HALO = ATTN_TILE, ATTN_HALO
    nt = S // TQ
    T = B * S
    hpt = TQ // HALO
    n_halo = T // HALO
    nb = len(ATTN_DILATIONS)

    def main(goff):
        return pl.BlockSpec((None, TQ, LANES), lambda b, i, h: (goff + h, b * nt + i, 0))

    def prev(goff):
        return pl.BlockSpec((None, HALO, LANES),
                            lambda b, i, h: (goff + h, jnp.maximum((b * nt + i) * hpt - 1, 0), 0))

    def nxt(goff):
        return pl.BlockSpec((None, HALO, LANES),
                            lambda b, i, h: (goff + h, jnp.minimum((b * nt + i + 1) * hpt, n_halo - 1), 0))

    t_prev = pl.BlockSpec((HALO, LANES), lambda b, i, h: (jnp.maximum(i * hpt - 1, 0), 0))
    t_main = pl.BlockSpec((TQ, LANES), lambda b, i, h: (i, 0))
    t_next = pl.BlockSpec((HALO, LANES), lambda b, i, h: (jnp.minimum((i + 1) * hpt, S // HALO - 1), 0))

    return pl.pallas_call(
        functools.partial(_attn_kernel, seq_len=S),
        grid=(B, nt, ATTN_HEADS),
        in_specs=[main(G_AQ), prev(G_AK), main(G_AK), nxt(G_AK), prev(G_AV), main(G_AV), nxt(G_AV),
                  t_prev, t_main, t_next, t_prev, t_main, t_next],
        out_specs=main(0),
        out_shape=jax.ShapeDtypeStruct((ATTN_HEADS, T, LANES), BF16),
        scratch_shapes=[pltpu.VMEM((TQ, LANES), F32), pltpu.VMEM((TQ + 2 * HALO, LANES), F32),
                        pltpu.VMEM((TQ + 2 * HALO, LANES), F32), pltpu.VMEM((nb, TQ, LANES), F32),
                        pltpu.VMEM((nb, TQ, LANES), F32), pltpu.VMEM((nb, TQ, LANES), F32)],
        name="dilated_attention",
        compiler_params=_cparams(("parallel", "parallel", "arbitrary")),
    )(proj, proj, proj, proj, proj, proj, proj, cos2, cos2, cos2, sin2, sin2, sin2)


def _ln_kernel(x_ref, y_ref, g_ref, b_ref, o_ref, ob_ref):
    z = DEEPNORM_ALPHA * x_ref[...] + y_ref[...]
    mu = jnp.mean(z, axis=-1, keepdims=True)
    zc = z - mu
    var = jnp.mean(jnp.square(zc), axis=-1, keepdims=True)
    out = zc * lax.rsqrt(var + LN_EPS) * g_ref[...] + b_ref[...]
    o_ref[...] = out
    ob_ref[...] = out.astype(BF16)


def residual_layer_norm(x, y, g, b, tr=256):
    T, D = x.shape
    row = pl.BlockSpec((tr, D), lambda i: (i, 0))
    vec = pl.BlockSpec((1, D), lambda i: (0, 0))
    return pl.pallas_call(
        _ln_kernel,
        grid=(T // tr,),
        in_specs=[row, row, vec, vec],
        out_specs=[row, row],
        out_shape=[jax.ShapeDtypeStruct((T, D), F32), jax.ShapeDtypeStruct((T, D), BF16)],
        name="residual_layer_norm",
        compiler_params=_cparams(("parallel",)),
    )(x, y, g, b)


FFN_HALO = 16
FFN_ROW_CHUNKS = 8


def _up_conv_gate_kernel(xp_ref, xm_ref, xn_ref, wg_ref, wv_ref, cwg_ref, cwv_ref, cbg_ref, cbv_ref, o_ref,
                         xs_ref, u_s, *, n_tiles):
    i = pl.program_id(1)
    j = pl.program_id(2)
    tm = xm_ref.shape[0]
    hr = xp_ref.shape[0]
    n_chunks = FFN_ROW_CHUNKS
    cr = tm // n_chunks

    @pl.when(j == 0)
    def _():
        xs_ref[0:hr, :] = xp_ref[...] * (i > 0).astype(BF16)
        xs_ref[hr:hr + tm, :] = xm_ref[...]
        xs_ref[hr + tm:, :] = xn_ref[...] * (i < n_tiles - 1).astype(BF16)

    def project(c):
        r0 = 0 if c == 0 else hr + c * cr
        r1 = tm + 2 * hr if c == n_chunks - 1 else hr + (c + 1) * cr
        u_s[0, r0:r1, :] = _dot(xs_ref[r0:r1, :], wg_ref[...])
        u_s[1, r0:r1, :] = _dot(xs_ref[r0:r1, :], wv_ref[...])

    def finish(c):
        def conv(half, cw_ref, cb_ref):
            base = hr + c * cr
            return (cb_ref[...] + u_s[half, base - 1:base - 1 + cr, :] * cw_ref[0:1, :]
                    + u_s[half, base:base + cr, :] * cw_ref[1:2, :]
                    + u_s[half, base + 1:base + 1 + cr, :] * cw_ref[2:3, :])
        gate = conv(0, cwg_ref, cbg_ref)
        val = conv(1, cwv_ref, cbv_ref)
        o_ref[c * cr:(c + 1) * cr, :] = (gate * _sigmoid(gate) * val).astype(o_ref.dtype)

    project(0)
    for c in range(1, n_chunks):
        project(c)
        finish(c - 1)
    finish(n_chunks - 1)


def up_conv_gate(xb, w_up, conv_w, conv_b, B, S, tm=1024, tf=256):
    T, K = xb.shape
    nt = S // tm
    nf = D_FF // tf
    hb = tm // FFN_HALO
    n_hb = T // FFN_HALO

    def wspec(off, r):
        return pl.BlockSpec((r, tf), lambda b, i, j: (0, off + j))

    return pl.pallas_call(
        functools.partial(_up_conv_gate_kernel, n_tiles=nt),
        grid=(B, nt, nf),
        in_specs=[pl.BlockSpec((FFN_HALO, K), lambda b, i, j: (jnp.maximum((b * nt + i) * hb - 1, 0), 0)),
                  pl.BlockSpec((tm, K), lambda b, i, j: (b * nt + i, 0)),
                  pl.BlockSpec((FFN_HALO, K), lambda b, i, j: (jnp.minimum((b * nt + i + 1) * hb, n_hb - 1), 0)),
                  wspec(0, K), wspec(nf, K), wspec(0, 3), wspec(nf, 3), wspec(0, 1), wspec(nf, 1)],
        out_specs=pl.BlockSpec((tm, tf), lambda b, i, j: (b * nt + i, j)),
        out_shape=jax.ShapeDtypeStruct((T, D_FF), BF16),
        scratch_shapes=[pltpu.VMEM((tm + 2 * FFN_HALO, K), BF16),
                        pltpu.VMEM((2, tm + 2 * FFN_HALO, tf), F32)],
        name="up_conv_gate",
        compiler_params=_cparams(("parallel", "parallel", "arbitrary")),
    )(xb, xb, xb, w_up, w_up, conv_w, conv_w, conv_b, conv_b)


def _ln_ple_kernel(z_ref, g_ref, b_ref, wg_ref, p_ref, wp_ref, o_ref, ob_ref, x_s, xb_s):
    j = pl.program_id(1)
    tn = wg_ref.shape[1]

    @pl.when(j == 0)
    def _():
        z = z_ref[...]
        mu = jnp.mean(z, axis=-1, keepdims=True)
        zc = z - mu
        var = jnp.mean(jnp.square(zc), axis=-1, keepdims=True)
        x = zc * lax.rsqrt(var + LN_EPS) * g_ref[...] + b_ref[...]
        xb_s[...] = x.astype(BF16)
        for jj in range(x_s.shape[0]):
            x_s[jj] = x[:, jj * tn:(jj + 1) * tn]

    gate = _sigmoid(_dot(xb_s[...], wg_ref[...]))
    emb = _dot(p_ref[...].astype(BF16), wp_ref[...])
    out = x_s[j] + gate * emb
    o_ref[...] = out
    ob_ref[...] = out.astype(BF16)


def ln_ple_update(z, g, b, p, w_gate, w_ple, tm=512, tn=512):
    T, D = z.shape
    vec = pl.BlockSpec((1, D), lambda i, j: (0, 0))
    return pl.pallas_call(
        _ln_ple_kernel,
        grid=(T // tm, D // tn),
        in_specs=[pl.BlockSpec((tm, D), lambda i, j: (i, 0)), vec, vec,
                  pl.BlockSpec((D, tn), lambda i, j: (0, j)),
                  pl.BlockSpec((tm, PLE_DIM), lambda i, j: (i, 0)),
                  pl.BlockSpec((PLE_DIM, tn), lambda i, j: (0, j))],
        out_specs=[pl.BlockSpec((tm, tn), lambda i, j: (i, j)), pl.BlockSpec((tm, tn), lambda i, j: (i, j))],
        out_shape=[jax.ShapeDtypeStruct((T, D), F32), jax.ShapeDtypeStruct((T, D), BF16)],
        scratch_shapes=[pltpu.VMEM((D // tn, tm, tn), F32), pltpu.VMEM((tm, D), BF16)],
        name="ln_ple_update",
        compiler_params=_cparams(("parallel", "arbitrary")),
    )(z, g, b, w_gate, p, w_ple)


def _regroup_columns(a):
    mw = MLSTM_HEADS * HEAD_DIM
    aw = ATTN_HEADS * HEAD_DIM
    gkw = GLA_HEADS * GLA_HEAD_K
    gvw = GLA_HEADS * HEAD_DIM
    sizes = [mw] * 4 + [4 * MLSTM_HEADS] + [aw] * 3 + [gkw] * 2 + [gvw] * 2 + [GLA_RANK] * 2
    offs = [0]
    for sz in sizes:
        offs.append(offs[-1] + sz)
    seg = [a[..., offs[i]:offs[i + 1]] for i in range(len(sizes))]
    mq, mk, mv, mo, mgate, aq, ak, av, gq, gk, gv, gr, ga_f, ga_b = seg
    big = jnp.concatenate([gv, gr, gq, gk, aq, mq, mk, mv, mo, ak, av], axis=-1)
    small = jnp.concatenate([mgate, ga_f, ga_b], axis=-1)
    return big, small


def _prepare_layer_params(w_in, mlstm_gate_b, mlstm_norm_g, gla_w_a2, gla_a_b, gla_norm_g, w_out,
                          ln1_g, ln1_b, w_up, conv_w, conv_b, w_down, ln2_g, ln2_b, w_ple, w_ple_gate):
    n_gate = 4 * MLSTM_HEADS
    w_big, w_small = _regroup_columns(w_in)
    w_big = w_big.astype(BF16)
    w_small = jnp.pad(w_small, ((0, 0), (0, LANES - w_small.shape[1]))).astype(BF16)
    bias = jnp.pad(mlstm_gate_b.reshape(-1), (0, LANES - n_gate)).astype(F32)
    wa = jnp.zeros((2, GLA_HEADS // 2, LANES, LANES), F32)
    a2 = gla_w_a2.reshape(2, GLA_RANK, GLA_HEADS // 2, LANES).transpose(0, 2, 1, 3)
    wa = wa.at[0, :, GATE_A_F:GATE_A_F + GLA_RANK, :].set(a2[0])
    wa = wa.at[1, :, GATE_A_B:GATE_A_B + GLA_RANK, :].set(a2[1])
    return dict(
        w_big=w_big, w_small=w_small, w_small_t=w_small.T,
        bias_c=bias.reshape(1, LANES), bias_r=bias.reshape(LANES, 1),
        mlstm_norm_g=mlstm_norm_g.reshape(MLSTM_HEADS, 1, HEAD_DIM),
        wa=wa.astype(BF16), a_bias=gla_a_b.reshape(2, GLA_HEADS // 2, 1, LANES),
        gla_norm_g=gla_norm_g.reshape(GLA_HEADS, 1, HEAD_DIM),
        w_out=w_out.astype(BF16), ln1_g=ln1_g.reshape(1, -1), ln1_b=ln1_b.reshape(1, -1),
        w_up=w_up.astype(BF16), conv_w=conv_w, conv_b=conv_b.reshape(1, -1), w_down=w_down.astype(BF16),
        ln2_g=ln2_g.reshape(1, -1), ln2_b=ln2_b.reshape(1, -1),
        w_ple=w_ple.astype(BF16), w_ple_gate=w_ple_gate.astype(BF16))


def _rope_tables(S):
    half = HEAD_DIM // 2
    inv = ROPE_THETA ** (-jnp.arange(half, dtype=F32) / half)
    ang = jnp.arange(S, dtype=F32)[:, None] * inv[None, :]
    cos, sin = jnp.cos(ang), jnp.sin(ang)
    return jnp.concatenate([cos, cos], axis=-1), jnp.concatenate([-sin, sin], axis=-1)


def _encoder_layer(x, xb, p_i, prm, tables, B, S):
    proj = proj_matmul(xb, prm['w_big'])
    gates = gates_project(xb, prm['w_small'], prm['w_small_t'], prm['bias_c'], prm['bias_r'], MLSTM_CHUNK)
    hf = mlstm_direction(proj, gates, B, S, reverse=False)
    y_m = mlstm_direction(proj, gates, B, S, reverse=True, hf=hf, norm_g=prm['mlstm_norm_g'])
    y_a = dilated_attention(proj, tables[0], tables[1], B, S)
    of = gla_direction(proj, gates[0], prm['wa'], prm['a_bias'], B, S, reverse=False)
    y_g = gla_direction(proj, gates[0], prm['wa'], prm['a_bias'], B, S, reverse=True, of=of,
                        norm_g=prm['gla_norm_g'])
    mix = mix_out_matmul(y_m, y_a, y_g, prm['w_out'])
    x, xb = residual_layer_norm(x, mix, prm['ln1_g'], prm['ln1_b'])
    hmid = up_conv_gate(xb, prm['w_up'], prm['conv_w'], prm['conv_b'], B, S)
    z = matmul_residual(hmid, prm['w_down'], x, tm=512, tn=512)
    return ln_ple_update(z, prm['ln2_g'], prm['ln2_b'], p_i, prm['w_ple_gate'], prm['w_ple'])


def _trunk(x, p, layer_params):
    B, S, D = x.shape
    tables = _rope_tables(S)
    x = x.reshape(B * S, D)
    xb = x.astype(BF16)
    for i, prm in enumerate(layer_params):
        x, xb = _encoder_layer(x, xb, p[i].reshape(B * S, -1), prm, tables, B, S)
    return x.reshape(B, S, D)


def kernel(x_prompt, x_sample, p_prompt, p_sample, w_in, mlstm_gate_b, mlstm_norm_g, gla_w_a2, gla_a_b,
           gla_norm_g, w_out, ln1_g, ln1_b, w_up, conv_w, conv_b, w_down, ln2_g, ln2_b, w_ple, w_ple_gate):
    weights = (w_in, mlstm_gate_b, mlstm_norm_g, gla_w_a2, gla_a_b, gla_norm_g, w_out, ln1_g, ln1_b,
               w_up, conv_w, conv_b, w_down, ln2_g, ln2_b, w_ple, w_ple_gate)
    layer_params = [_prepare_layer_params(*(w[i] for w in weights)) for i in range(w_in.shape[0])]
    return (_trunk(x_prompt, p_prompt, layer_params), _trunk(x_sample, p_sample, layer_params))
```

```python
import functools

import jax
import jax.numpy as jnp
from jax import lax
from jax.experimental import pallas as pl
from jax.experimental.pallas import tpu as pltpu

F32 = jnp.float32
BF16 = jnp.bfloat16

D_MODEL = 4096
HEAD_DIM = 128
MLSTM_HEADS = 8
ATTN_HEADS = 12
GLA_HEADS = 12
GLA_HEAD_K = 64
GLA_RANK = 16
GLA_TAU = 16.0
GLA_CHUNK = 64
ATTN_REACH = 64
ATTN_DILATIONS = (1, 4, 16)
ROPE_THETA = 10000.0
D_FF = 11008
PLE_DIM = 256
LN_EPS = 1e-5
HEAD_NORM_EPS = 1e-6
DEPTH = 2
DEEPNORM_ALPHA = (2 * DEPTH) ** 0.25

LANES = 128
VMEM_LIMIT = 56 * 1024 * 1024

G_GV, G_GR, G_GQ, G_GK = 0, 12, 24, 30
G_AQ = 36
G_MQ, G_MK, G_MV, G_MO = 48, 56, 64, 72
G_AK, G_AV = 80, 92
N_PROJ_GROUPS = 104
GATE_I_F, GATE_F_F, GATE_I_B, GATE_F_B, GATE_A_F, GATE_A_B = 0, 8, 16, 24, 32, 48

MLSTM_CHUNK = 256
GLA_STEP = 256
ATTN_TILE = 2048
ATTN_HALO = ATTN_REACH * ATTN_DILATIONS[-1]
ATTN_QB = 128
ATTN_KB = ATTN_QB + 2 * ATTN_REACH
ATTN_UNROLL = 8


def _cparams(sem):
    return pltpu.CompilerParams(dimension_semantics=sem, vmem_limit_bytes=VMEM_LIMIT)


def _log_sigmoid(x):
    return jnp.minimum(x, 0.0) - jnp.log1p(jnp.exp(-jnp.abs(x)))


def _sigmoid(x):
    return 1.0 / (1.0 + jnp.exp(-x))


def _dot(a, b):
    return jnp.dot(a, b, preferred_element_type=F32)


def _dot_nt(a, b):
    return lax.dot_general(a, b, (((1,), (1,)), ((), ())), preferred_element_type=F32)


def _dot_tn(a, b):
    return lax.dot_general(a, b, (((0,), (0,)), ((), ())), preferred_element_type=F32)


def _split3(x):
    hi = x.astype(BF16)
    r1 = x - hi.astype(F32)
    mid = r1.astype(BF16)
    lo = (r1 - mid.astype(F32)).astype(BF16)
    return hi, mid, lo


def _tri_dot(tri, x):
    hi, mid, lo = _split3(x)
    return _dot(tri, hi) + _dot(tri, mid) + _dot(tri, lo)


def _dot_tri(x, tri):
    hi, mid, lo = _split3(x)
    return _dot(hi, tri) + _dot(mid, tri) + _dot(lo, tri)


def _proj_kernel(x_ref, w_ref, o_ref):
    acc = _dot(x_ref[...], w_ref[...])
    for g in range(o_ref.shape[0]):
        o_ref[g] = acc[:, g * LANES:(g + 1) * LANES].astype(o_ref.dtype)


def proj_matmul(xb, w, tm=1024, tn=1024):
    T, K = xb.shape
    N = w.shape[1]
    return pl.pallas_call(
        _proj_kernel,
        grid=(T // tm, N // tn),
        in_specs=[pl.BlockSpec((tm, K), lambda i, j: (i, 0)),
                  pl.BlockSpec((K, tn), lambda i, j: (0, j))],
        out_specs=pl.BlockSpec((tn // LANES, tm, LANES), lambda i, j: (j, i, 0)),
        out_shape=jax.ShapeDtypeStruct((N // LANES, T, LANES), F32),
        name="proj_matmul",
        compiler_params=_cparams(("parallel", "parallel")),
    )(xb, w)


def _mm_kernel(x_ref, w_ref, o_ref):
    o_ref[...] = _dot(x_ref[...], w_ref[...]).astype(o_ref.dtype)


def matmul(xb, w, tm, tn, out_dtype=F32):
    T, K = xb.shape
    N = w.shape[1]
    return pl.pallas_call(
        _mm_kernel,
        grid=(T // tm, N // tn),
        in_specs=[pl.BlockSpec((tm, K), lambda i, j: (i, 0)),
                  pl.BlockSpec((K, tn), lambda i, j: (0, j))],
        out_specs=pl.BlockSpec((tm, tn), lambda i, j: (i, j)),
        out_shape=jax.ShapeDtypeStruct((T, N), out_dtype),
        name="matmul",
        compiler_params=_cparams(("parallel", "parallel")),
    )(xb, w)


def _mm_residual_kernel(x_ref, w_ref, r_ref, o_ref):
    o_ref[...] = DEEPNORM_ALPHA * r_ref[...] + _dot(x_ref[...], w_ref[...])


def matmul_residual(xb, w, res, tm, tn):
    T, K = xb.shape
    N = w.shape[1]
    return pl.pallas_call(
        _mm_residual_kernel,
        grid=(T // tm, N // tn),
        in_specs=[pl.BlockSpec((tm, K), lambda i, j: (i, 0)),
                  pl.BlockSpec((K, tn), lambda i, j: (0, j)),
                  pl.BlockSpec((tm, tn), lambda i, j: (i, j))],
        out_specs=pl.BlockSpec((tm, tn), lambda i, j: (i, j)),
        out_shape=jax.ShapeDtypeStruct((T, N), F32),
        name="matmul_residual",
        compiler_params=_cparams(("parallel", "parallel")),
    )(xb, w, res)


def _mix_out_ln_kernel(ym_ref, ya_ref, yg_ref, w_ref, x_ref, g_ref, b_ref, o_ref, ob_ref):
    j = pl.program_id(1)
    tn = w_ref.shape[1]
    d = o_ref.shape[1]
    n_slabs = d // tn
    parts = ([ym_ref[g] for g in range(ym_ref.shape[0])] + [ya_ref[g] for g in range(ya_ref.shape[0])]
             + [yg_ref[g] for g in range(yg_ref.shape[0])])
    y = jnp.concatenate(parts, axis=-1)
    o_ref[:, pl.ds(pl.multiple_of(j * tn, tn), tn)] = DEEPNORM_ALPHA * x_ref[...] + _dot(y, w_ref[...])

    @pl.when(j == n_slabs - 1)
    def _():
        slabs = [slice(s * tn, (s + 1) * tn) for s in range(n_slabs)]
        mu = sum(jnp.sum(o_ref[:, c], axis=-1, keepdims=True) for c in slabs) * (1.0 / d)
        var = sum(jnp.sum(jnp.square(o_ref[:, c] - mu), axis=-1, keepdims=True) for c in slabs) * (1.0 / d)
        inv = lax.rsqrt(var + LN_EPS)
        for c in slabs:
            out = (o_ref[:, c] - mu) * inv * g_ref[:, c] + b_ref[:, c]
            o_ref[:, c] = out
            ob_ref[:, c] = out.astype(BF16)


def mix_out_layer_norm(ym, ya, yg, w, x, g, b, tm=512, tn=512):
    T = ym.shape[1]
    N = w.shape[1]

    def gspec(a):
        return pl.BlockSpec((a.shape[0], tm, LANES), lambda i, j: (0, i, 0))

    vec = pl.BlockSpec((1, N), lambda i, j: (0, 0))
    row = pl.BlockSpec((tm, N), lambda i, j: (i, 0))
    return pl.pallas_call(
        _mix_out_ln_kernel,
        grid=(T // tm, N // tn),
        in_specs=[gspec(ym), gspec(ya), gspec(yg), pl.BlockSpec((w.shape[0], tn), lambda i, j: (0, j)),
                  pl.BlockSpec((tm, tn), lambda i, j: (i, j)), vec, vec],
        out_specs=[row, row],
        out_shape=[jax.ShapeDtypeStruct((T, N), F32), jax.ShapeDtypeStruct((T, N), BF16)],
        name="mix_out_layer_norm",
        compiler_params=_cparams(("parallel", "arbitrary")),
    )(ym, ya, yg, w, x, g, b)


def _gates_kernel(x_ref, w_ref, wt_ref, bc_ref, br_ref, pc_ref, cc_ref, pr_ref, cr_ref, *, chunk):
    x = x_ref[...]
    tm = x.shape[0]
    pc = _dot(x, w_ref[...]) + bc_ref[...]
    pr = _dot_nt(wt_ref[...], x) + br_ref[...]
    pc_ref[...] = pc
    pr_ref[...] = pr
    ls_c = _log_sigmoid(pc)
    ls_r = _log_sigmoid(pr)
    row = lax.broadcasted_iota(jnp.int32, (chunk, chunk), 0)
    col = lax.broadcasted_iota(jnp.int32, (chunk, chunk), 1)
    lower = (row >= col).astype(BF16)
    upper = (row <= col).astype(BF16)
    lane = lax.broadcasted_iota(jnp.int32, (chunk, LANES), 1)
    sub = lax.broadcasted_iota(jnp.int32, (LANES, chunk), 0)
    bwd_c = (lane >= GATE_F_B) & (lane < GATE_F_B + MLSTM_HEADS)
    bwd_r = (sub >= GATE_F_B) & (sub < GATE_F_B + MLSTM_HEADS)
    for c in range(tm // chunk):
        sl = slice(c * chunk, (c + 1) * chunk)
        lc = ls_c[sl]
        cc_ref[sl, :] = jnp.where(bwd_c, _tri_dot(upper, lc), _tri_dot(lower, lc))
        lr = ls_r[:, sl]
        cr_ref[:, sl] = jnp.where(bwd_r, _dot_tri(lr, lower), _dot_tri(lr, upper))


def gates_project(xb, w_small, w_small_t, bias_c, bias_r, chunk, tm=1024):
    T, K = xb.shape
    return pl.pallas_call(
        functools.partial(_gates_kernel, chunk=chunk),
        grid=(T // tm,),
        in_specs=[pl.BlockSpec((tm, K), lambda i: (i, 0)),
                  pl.BlockSpec((K, LANES), lambda i: (0, 0)),
                  pl.BlockSpec((LANES, K), lambda i: (0, 0)),
                  pl.BlockSpec((1, LANES), lambda i: (0, 0)),
                  pl.BlockSpec((LANES, 1), lambda i: (0, 0))],
        out_specs=[pl.BlockSpec((tm, LANES), lambda i: (i, 0)),
                   pl.BlockSpec((tm, LANES), lambda i: (i, 0)),
                   pl.BlockSpec((LANES, tm), lambda i: (0, i)),
                   pl.BlockSpec((LANES, tm), lambda i: (0, i))],
        out_shape=[jax.ShapeDtypeStruct((T, LANES), F32), jax.ShapeDtypeStruct((T, LANES), F32),
                   jax.ShapeDtypeStruct((LANES, T), F32), jax.ShapeDtypeStruct((LANES, T), F32)],
        name="gates_project",
        compiler_params=_cparams(("parallel",)),
    )(xb, w_small, w_small_t, bias_c, bias_r)


def _mlstm_kernel(*refs, reverse, final):
    if final:
        (q_ref, k_ref, v_ref, pc_ref, cc_ref, pr_ref, cr_ref, hf_ref, o_ref, g_ref,
         out_ref, ct_s, n_s, m_s) = refs
    else:
        q_ref, k_ref, v_ref, pc_ref, cc_ref, pr_ref, cr_ref, out_ref, ct_s, n_s, m_s = refs
    c = pl.program_id(1)
    H, L = q_ref.shape[0], q_ref.shape[1]

    @pl.when(c == 0)
    def _():
        ct_s[...] = jnp.zeros(ct_s.shape, F32)
        n_s[...] = jnp.zeros(n_s.shape, F32)
        m_s[...] = jnp.zeros(m_s.shape, F32)

    pc = pc_ref[...]
    cc = cc_ref[...]
    lane = lax.broadcasted_iota(jnp.int32, (L, LANES), 1)
    t_idx = lax.broadcasted_iota(jnp.int32, (L, L), 0)
    s_idx = lax.broadcasted_iota(jnp.int32, (L, L), 1)
    causal = (s_idx >= t_idx) if reverse else (s_idx <= t_idx)

    heads = range(H)
    icol = [(GATE_I_B if reverse else GATE_I_F) + h for h in heads]
    fcol = [(GATE_F_B if reverse else GATE_F_F) + h for h in heads]
    qf = [q_ref[h] for h in heads]
    kf = [k_ref[h] * (HEAD_DIM ** -0.5) for h in heads]
    qb = [x.astype(BF16) for x in qf]
    kb = [x.astype(BF16) for x in kf]
    vb = [v_ref[h].astype(BF16) for h in heads]
    ct = [ct_s[h] for h in heads]
    n_prev = [n_s[h] for h in heads]
    m_prev = [m_s[h][:, 0:1] for h in heads]

    i_col = [jnp.sum(jnp.where(lane == icol[h], pc, 0.0), axis=-1, keepdims=True) for h in heads]
    b_col = [jnp.sum(jnp.where(lane == fcol[h], cc, 0.0), axis=-1, keepdims=True) for h in heads]
    r_col = [i_col[h] - b_col[h] for h in heads]
    b_row = [cr_ref[fcol[h]:fcol[h] + 1, :] for h in heads]
    r_row = [pr_ref[icol[h]:icol[h] + 1, :] - b_row[h] for h in heads]

    r_mat = [jnp.where(causal, r_row[h], -jnp.inf) for h in heads]
    m_row = [jnp.maximum(jnp.max(r_mat[h], axis=-1, keepdims=True), m_prev[h]) for h in heads]
    scores = [_dot_nt(qb[h], kb[h]) for h in heads]
    inter = [_dot(qb[h], ct[h].astype(BF16)) for h in heads]
    p = [jnp.exp(r_mat[h] - m_row[h]) * scores[h] for h in heads]
    w_inter = [jnp.exp(m_prev[h] - m_row[h]) for h in heads]
    pv = [_dot(p[h].astype(BF16), vb[h]) for h in heads]
    nq = [w_inter[h] * jnp.sum(qf[h] * n_prev[h], axis=-1, keepdims=True)
          + jnp.sum(p[h], axis=-1, keepdims=True) for h in heads]
    den = [jnp.maximum(jnp.abs(nq[h]), jnp.exp(-(b_col[h] + m_row[h]))) for h in heads]
    hout = [(w_inter[h] * inter[h] + pv[h]) / den[h] for h in heads]

    g = [b_row[h][:, 0:1] if reverse else b_row[h][:, L - 1:L] for h in heads]
    r_max = [jnp.max(r_row[h], axis=-1, keepdims=True) for h in heads]
    m_loc = [g[h] + r_max[h] for h in heads]
    m_new = [jnp.maximum(g[h] + m_prev[h], m_loc[h]) for h in heads]
    s_old = [jnp.exp(g[h] + m_prev[h] - m_new[h]) for h in heads]
    s_new = [jnp.exp(m_loc[h] - m_new[h]) for h in heads]
    kw = [kf[h] * jnp.exp(r_col[h] - r_max[h]) for h in heads]
    ct_loc = [_dot_tn(kw[h].astype(BF16), vb[h]) for h in heads]
    for h in heads:
        ct_s[h] = s_old[h] * ct[h] + s_new[h] * ct_loc[h]
        n_s[h] = s_old[h] * n_prev[h] + s_new[h] * jnp.sum(kw[h], axis=0, keepdims=True)
        m_s[h] = jnp.broadcast_to(m_new[h], m_s.shape[1:])

    for h in heads:
        if final:
            hs = hf_ref[h] + hout[h]
            mu = jnp.mean(hs, axis=-1, keepdims=True)
            var = jnp.mean(jnp.square(hs - mu), axis=-1, keepdims=True)
            hn = (hs - mu) * lax.rsqrt(var + HEAD_NORM_EPS) * g_ref[h]
            out_ref[h] = (_sigmoid(o_ref[h]) * hn).astype(out_ref.dtype)
        else:
            out_ref[h] = hout[h]


def mlstm_direction(proj, gates, B, S, reverse, hf=None, norm_g=None):
    pc, cc, pr, cr = gates
    L = MLSTM_CHUNK
    nc = S // L
    T = B * S
    H = MLSTM_HEADS
    final = hf is not None

    def tok(b, c):
        return b * nc + (nc - 1 - c if reverse else c)

    def gspec(goff):
        return pl.BlockSpec((H, L, LANES), lambda b, c: (goff // H, tok(b, c), 0))

    in_specs = [gspec(G_MQ), gspec(G_MK), gspec(G_MV),
                pl.BlockSpec((L, LANES), lambda b, c: (tok(b, c), 0)),
                pl.BlockSpec((L, LANES), lambda b, c: (tok(b, c), 0)),
                pl.BlockSpec((LANES, L), lambda b, c: (0, tok(b, c))),
                pl.BlockSpec((LANES, L), lambda b, c: (0, tok(b, c)))]
    args = [proj, proj, proj, pc, cc, pr, cr]
    if final:
        in_specs += [gspec(0), gspec(G_MO), pl.BlockSpec((H, 1, LANES), lambda b, c: (0, 0, 0))]
        args += [hf, proj, norm_g]
    return pl.pallas_call(
        functools.partial(_mlstm_kernel, reverse=reverse, final=final),
        grid=(B, nc),
        in_specs=in_specs,
        out_specs=gspec(0),
        out_shape=jax.ShapeDtypeStruct((H, T, LANES), BF16 if final else F32),
        scratch_shapes=[pltpu.VMEM((H, HEAD_DIM, HEAD_DIM), F32), pltpu.VMEM((H, 1, HEAD_DIM), F32),
                        pltpu.VMEM((H, 1, LANES), F32)],
        name="mlstm_bwd" if reverse else "mlstm_fwd",
        compiler_params=_cparams(("arbitrary", "arbitrary")),
    )(*args)


def _gla_kernel(*refs, reverse, final):
    if final:
        (q_ref, k_ref, v_ref, pc_ref, wa_ref, ab_ref, of_ref, r_ref, g_ref, out_ref, st_s) = refs
    else:
        q_ref, k_ref, v_ref, pc_ref, wa_ref, ab_ref, out_ref, st_s = refs
    c = pl.program_id(1)
    NG, TS = q_ref.shape[0], q_ref.shape[1]
    L = GLA_CHUNK
    n_sub = TS // L

    @pl.when(c == 0)
    def _():
        st_s[...] = jnp.zeros(st_s.shape, F32)

    row = lax.broadcasted_iota(jnp.int32, (TS, TS), 0)
    col = lax.broadcasted_iota(jnp.int32, (TS, TS), 1)
    same_chunk = (row // L) == (col // L)
    causal = same_chunk & ((col >= row) if reverse else (col <= row))
    cum_mat = causal.astype(BF16)
    lane = lax.broadcasted_iota(jnp.int32, (TS, LANES), 1)
    head_mask = [lane < GLA_HEAD_K, lane >= GLA_HEAD_K]
    pcb = pc_ref[...].astype(BF16)
    order = list(range(n_sub - 1, -1, -1)) if reverse else list(range(n_sub))
    pairs = range(NG)
    heads = range(2 * NG)


    la = [_log_sigmoid(_dot(pcb, wa_ref[gi]) + ab_ref[gi]) * (1.0 / GLA_TAU) for gi in pairs]
    split = [_split3(x) for x in la]
    b = [_dot(cum_mat, hi) + _dot(cum_mat, mid) + _dot(cum_mat, lo) for hi, mid, lo in split]

    def per_chunk_rows(x, idx):
        return jnp.concatenate([jnp.broadcast_to(x[s * L + idx:s * L + idx + 1, :], (L, LANES))
                                for s in range(n_sub)], axis=0)

    g_full = [per_chunk_rows(x, 0 if reverse else L - 1) for x in b]
    b_mid = [per_chunk_rows(x, L // 2 - 1 if reverse else L // 2) for x in b]
    qf = [q_ref[gi] * (GLA_HEAD_K ** -0.5) for gi in pairs]
    kf = [k_ref[gi] for gi in pairs]
    qd = [qf[gi] * jnp.exp(b[gi] - b_mid[gi]) for gi in pairs]
    kd = [(kf[gi] * jnp.exp(b_mid[gi] - b[gi])).astype(BF16) for gi in pairs]
    kg = [kf[gi] * jnp.exp(g_full[gi] - b[gi]) for gi in pairs]
    qe = [qf[gi] * jnp.exp(b[gi]) for gi in pairs]
    decay = [[jnp.exp(b[gi][s * L:s * L + 1, :] if reverse else b[gi][s * L + L - 1:s * L + L, :])
              for s in range(n_sub)] for gi in pairs]

    vb = [v_ref[hd].astype(BF16) for hd in heads]
    qd_h = [jnp.where(head_mask[hd % 2], qd[hd // 2], 0.0).astype(BF16) for hd in heads]
    kg_h = [jnp.where(head_mask[hd % 2], kg[hd // 2], 0.0).astype(BF16) for hd in heads]
    qe_h = [jnp.where(head_mask[hd % 2], qe[hd // 2], 0.0).astype(BF16) for hd in heads]
    a = [_dot_nt(qd_h[hd], kd[hd // 2]) for hd in heads]
    a = [jnp.where(causal, x, 0.0).astype(BF16) for x in a]
    o_intra = [_dot(a[hd], vb[hd]) for hd in heads]

    st = [st_s[hd] for hd in heads]
    o_inter = [[None] * n_sub for _ in heads]
    for s in order:
        sl = slice(s * L, (s + 1) * L)
        for hd in heads:
            o_inter[hd][s] = _dot_nt(qe_h[hd][sl], st[hd].astype(BF16))
        st_loc = [_dot_tn(vb[hd][sl], kg_h[hd][sl]) for hd in heads]
        st = [st[hd] * decay[hd // 2][s] + st_loc[hd] for hd in heads]
    for hd in heads:
        st_s[hd] = st[hd]

    for hd in heads:
        o = o_intra[hd] + jnp.concatenate(o_inter[hd], axis=0)
        if final:
            ot = of_ref[hd] + o
            on = ot * lax.rsqrt(jnp.mean(jnp.square(ot), axis=-1, keepdims=True) + HEAD_NORM_EPS) * g_ref[hd]
            rr = r_ref[hd]
            out_ref[hd] = (on * (rr * _sigmoid(rr))).astype(out_ref.dtype)
        else:
            out_ref[hd] = o


def gla_direction(proj, pc, wa, a_bias, B, S, reverse, of=None, norm_g=None):
    TS = GLA_STEP
    nc = S // TS
    T = B * S
    NG = GLA_HEADS // 2
    final = of is not None
    d = 1 if reverse else 0

    def tok(b, c):
        return b * nc + (nc - 1 - c if reverse else c)

    def gspec(goff, n):
        return pl.BlockSpec((n, TS, LANES), lambda b, c: (goff // n, tok(b, c), 0))

    in_specs = [gspec(G_GQ, NG), gspec(G_GK, NG), gspec(G_GV, GLA_HEADS),
                pl.BlockSpec((TS, LANES), lambda b, c: (tok(b, c), 0)),
                pl.BlockSpec((None, NG, LANES, LANES), lambda b, c: (d, 0, 0, 0)),
                pl.BlockSpec((None, NG, 1, LANES), lambda b, c: (d, 0, 0, 0))]
    args = [proj, proj, proj, pc, wa, a_bias]
    if final:
        in_specs += [gspec(0, GLA_HEADS), gspec(G_GR, GLA_HEADS),
                     pl.BlockSpec((GLA_HEADS, 1, LANES), lambda b, c: (0, 0, 0))]
        args += [of, proj, norm_g]
    return pl.pallas_call(
        functools.partial(_gla_kernel, reverse=reverse, final=final),
        grid=(B, nc),
        in_specs=in_specs,
        out_specs=gspec(0, GLA_HEADS),
        out_shape=jax.ShapeDtypeStruct((GLA_HEADS, T, LANES), BF16 if final else F32),
        scratch_shapes=[pltpu.VMEM((GLA_HEADS, HEAD_DIM, LANES), F32)],
        name="gla_bwd" if reverse else "gla_fwd",
        compiler_params=_cparams(("arbitrary", "arbitrary")),
    )(*args)


def _rope(t, cos2, sin2):
    return t * cos2 + pltpu.roll(t, shift=HEAD_DIM // 2, axis=1) * sin2


def _attn_kernel(q_ref, kp_ref, km_ref, kn_ref, vp_ref, vm_ref, vn_ref,
                 cp_ref, cm_ref, cn_ref, sp_ref, sm_ref, sn_ref, out_ref,
                 q_s, k_s, v_s, q4_s, k4_s, v4_s, st0_s, st4_s, o_s, *, seq_len):
    i = pl.program_id(1)
    TQ, HALO, QB, KB = ATTN_TILE, ATTN_HALO, ATTN_QB, ATTN_KB
    R = ATTN_DILATIONS[1]
    t0 = i * TQ

    q_s[...] = _rope(q_ref[...], cm_ref[...], sm_ref[...]) * (HEAD_DIM ** -0.5)
    k_s[0:HALO, :] = _rope(kp_ref[...], cp_ref[...], sp_ref[...])
    k_s[HALO:HALO + TQ, :] = _rope(km_ref[...], cm_ref[...], sm_ref[...])
    k_s[HALO + TQ:, :] = _rope(kn_ref[...], cn_ref[...], sn_ref[...])
    v_s[0:HALO, :] = vp_ref[...]
    v_s[HALO:HALO + TQ, :] = vm_ref[...]
    v_s[HALO + TQ:, :] = vn_ref[...]
    for j in range(R):
        q4_s[j] = q_s[pl.ds(j, TQ // R, stride=R), :]
        k4_s[j] = k_s[pl.ds(j, (TQ + 2 * HALO) // R, stride=R), :]
        v4_s[j] = v_s[pl.ds(j, (TQ + 2 * HALO) // R, stride=R), :]

    qi = lax.broadcasted_iota(jnp.int32, (QB, KB), 0)
    ki = lax.broadcasted_iota(jnp.int32, (QB, KB), 1)
    band = (ki >= qi) & (ki <= qi + 2 * ATTN_REACH)
    kcol = lax.broadcasted_iota(jnp.int32, (1, KB), 1)

    def block_group(blocks):
        n = range(len(blocks))
        qb = [blocks[u][0]().astype(BF16) for u in n]
        kb = [blocks[u][1]().astype(BF16) for u in n]
        vb = [blocks[u][2]().astype(BF16) for u in n]
        s = [_dot_nt(qb[u], kb[u]) for u in n]
        kpos = [blocks[u][3] for u in n]
        s = [jnp.where(band & (kpos[u] >= 0) & (kpos[u] < seq_len), s[u], -jnp.inf) for u in n]
        m = [jnp.max(s[u], axis=-1, keepdims=True) for u in n]
        e = [jnp.exp(s[u] - m[u]) for u in n]
        acc = [_dot(e[u].astype(BF16), vb[u]) for u in n]
        for u in n:
            store = blocks[u][4]
            store(0, acc[u])
            store(1, jnp.broadcast_to(m[u], (QB, LANES)))
            store(2, jnp.broadcast_to(jnp.sum(e[u], axis=-1, keepdims=True), (QB, LANES)))

    def dense_block(it):
        q0 = it * QB
        k0 = HALO + q0 - ATTN_REACH

        def store(which, val):
            st0_s[which, pl.ds(q0, QB), :] = val
        return (lambda: q_s[pl.ds(q0, QB), :], lambda: k_s[pl.ds(k0, KB), :], lambda: v_s[pl.ds(k0, KB), :],
                t0 - HALO + k0 + kcol, store)

    def mod4_block(it):
        j = it % R
        a0 = (it // R) * QB
        k0 = HALO // R + a0 - ATTN_REACH

        def store(which, val):
            st4_s[0, which, j, pl.ds(a0, QB), :] = val
        return (lambda: q4_s[j, pl.ds(a0, QB), :], lambda: k4_s[j, pl.ds(k0, KB), :],
                lambda: v4_s[j, pl.ds(k0, KB), :], t0 - HALO + R * (k0 + kcol) + j, store)

    def mod16_block(r):
        j = r % R
        m0 = r // R

        def store(which, val):
            st4_s[1, which, j, pl.ds(m0, QB, stride=R), :] = val
        return (lambda: q4_s[j, pl.ds(m0, QB, stride=R), :], lambda: k4_s[j, pl.ds(m0, KB, stride=R), :],
                lambda: v4_s[j, pl.ds(m0, KB, stride=R), :], t0 - HALO + r + ATTN_DILATIONS[2] * kcol, store)

    for make_block in (dense_block, mod4_block, mod16_block):
        def body(g, carry, make_block=make_block):
            block_group([make_block(g * ATTN_UNROLL + u) for u in range(ATTN_UNROLL)])
            return carry

        lax.fori_loop(0, (TQ // QB) // ATTN_UNROLL, body, 0)

    for j in range(R):
        rows = pl.ds(j, TQ // R, stride=R)
        parts = [(st0_s[0, rows, :], st0_s[1, rows, :], st0_s[2, rows, :]),
                 (st4_s[0, 0, j], st4_s[0, 1, j], st4_s[0, 2, j]),
                 (st4_s[1, 0, j], st4_s[1, 1, j], st4_s[1, 2, j])]
        m_all = jnp.maximum(jnp.maximum(parts[0][1], parts[1][1]), parts[2][1])
        num = jnp.zeros((TQ // R, LANES), F32)
        den = jnp.zeros((TQ // R, LANES), F32)
        for acc, m, l in parts:
            w = jnp.exp(m - m_all)
            num = num + w * acc
            den = den + w * l
        o_s[rows, :] = num / den
    out_ref[...] = o_s[...].astype(out_ref.dtype)


def dilated_attention(proj, cos2, sin2, B, S):
    TQ, HALO = ATTN_TILE, ATTN_HALO
    nt = S // TQ
    T = B * S
    hpt = TQ // HALO
    n_halo = T // HALO
    KT = TQ + 2 * HALO
    R = ATTN_DILATIONS[1]

    def main(goff):
        return pl.BlockSpec((None, TQ, LANES), lambda b, i, h: (goff + h, b * nt + i, 0))

    def prev(goff):
        return pl.BlockSpec((None, HALO, LANES),
                            lambda b, i, h: (goff + h, jnp.maximum((b * nt + i) * hpt - 1, 0), 0))

    def nxt(goff):
        return pl.BlockSpec((None, HALO, LANES),
                            lambda b, i, h: (goff + h, jnp.minimum((b * nt + i + 1) * hpt, n_halo - 1), 0))

    t_prev = pl.BlockSpec((HALO, LANES), lambda b, i, h: (jnp.maximum(i * hpt - 1, 0), 0))
    t_main = pl.BlockSpec((TQ, LANES), lambda b, i, h: (i, 0))
    t_next = pl.BlockSpec((HALO, LANES), lambda b, i, h: (jnp.minimum((i + 1) * hpt, S // HALO - 1), 0))

    return pl.pallas_call(
        functools.partial(_attn_kernel, seq_len=S),
        grid=(B, nt, ATTN_HEADS),
        in_specs=[main(G_AQ), prev(G_AK), main(G_AK), nxt(G_AK), prev(G_AV), main(G_AV), nxt(G_AV),
                  t_prev, t_main, t_next, t_prev, t_main, t_next],
        out_specs=main(0),
        out_shape=jax.ShapeDtypeStruct((ATTN_HEADS, T, LANES), BF16),
        scratch_shapes=[pltpu.VMEM((TQ, LANES), F32), pltpu.VMEM((KT, LANES), F32), pltpu.VMEM((KT, LANES), F32),
                        pltpu.VMEM((R, TQ // R, LANES), F32), pltpu.VMEM((R, KT // R, LANES), F32),
                        pltpu.VMEM((R, KT // R, LANES), F32), pltpu.VMEM((3, TQ, LANES), F32),
                        pltpu.VMEM((2, 3, R, TQ // R, LANES), F32), pltpu.VMEM((TQ, LANES), F32)],
        name="dilated_attention",
        compiler_params=_cparams(("parallel", "parallel", "arbitrary")),
    )(proj, proj, proj, proj, proj, proj, proj, cos2, cos2, cos2, sin2, sin2, sin2)


def _ln_kernel(x_ref, y_ref, g_ref, b_ref, o_ref, ob_ref):
    z = DEEPNORM_ALPHA * x_ref[...] + y_ref[...]
    mu = jnp.mean(z, axis=-1, keepdims=True)
    zc = z - mu
    var = jnp.mean(jnp.square(zc), axis=-1, keepdims=True)
    out = zc * lax.rsqrt(var + LN_EPS) * g_ref[...] + b_ref[...]
    o_ref[...] = out
    ob_ref[...] = out.astype(BF16)


def residual_layer_norm(x, y, g, b, tr=256):
    T, D = x.shape
    row = pl.BlockSpec((tr, D), lambda i: (i, 0))
    vec = pl.BlockSpec((1, D), lambda i: (0, 0))
    return pl.pallas_call(
        _ln_kernel,
        grid=(T // tr,),
        in_specs=[row, row, vec, vec],
        out_specs=[row, row],
        out_shape=[jax.ShapeDtypeStruct((T, D), F32), jax.ShapeDtypeStruct((T, D), BF16)],
        name="residual_layer_norm",
        compiler_params=_cparams(("parallel",)),
    )(x, y, g, b)


FFN_HALO = 16

def _up_conv_gate_kernel(xp_ref, xm_ref, xn_ref, wg_ref, wv_ref, cwg_ref, cwv_ref, cbg_ref, cbv_ref, o_ref,
                         xs_ref, *, n_tiles):
    i = pl.program_id(1)
    j = pl.program_id(2)
    tm = xm_ref.shape[0]
    hr = xp_ref.shape[0]

    @pl.when(j == 0)
    def _():
        xs_ref[0:hr, :] = xp_ref[...] * (i > 0).astype(BF16)
        xs_ref[hr:hr + tm, :] = xm_ref[...]
        xs_ref[hr + tm:, :] = xn_ref[...] * (i < n_tiles - 1).astype(BF16)

    xs = xs_ref[...]

    def conv(w_ref, cw_ref, cb_ref):
        u = _dot(xs, w_ref[...])
        return (cb_ref[...] + u[hr - 1:hr - 1 + tm] * cw_ref[0:1, :] + u[hr:hr + tm] * cw_ref[1:2, :]
                + u[hr + 1:hr + 1 + tm] * cw_ref[2:3, :])

    gate = conv(wg_ref, cwg_ref, cbg_ref)
    val = conv(wv_ref, cwv_ref, cbv_ref)
    o_ref[...] = (gate * _sigmoid(gate) * val).astype(o_ref.dtype)


def up_conv_gate(xb, w_up, conv_w, conv_b, B, S, tm=1024, tf=256):
    T, K = xb.shape
    nt = S // tm
    nf = D_FF // tf
    hb = tm // FFN_HALO
    n_hb = T // FFN_HALO

    def wspec(off, r):
        return pl.BlockSpec((r, tf), lambda b, i, j: (0, off + j))

    return pl.pallas_call(
        functools.partial(_up_conv_gate_kernel, n_tiles=nt),
        grid=(B, nt, nf),
        in_specs=[pl.BlockSpec((FFN_HALO, K), lambda b, i, j: (jnp.maximum((b * nt + i) * hb - 1, 0), 0)),
                  pl.BlockSpec((tm, K), lambda b, i, j: (b * nt + i, 0)),
                  pl.BlockSpec((FFN_HALO, K), lambda b, i, j: (jnp.minimum((b * nt + i + 1) * hb, n_hb - 1), 0)),
                  wspec(0, K), wspec(nf, K), wspec(0, 3), wspec(nf, 3), wspec(0, 1), wspec(nf, 1)],
        out_specs=pl.BlockSpec((tm, tf), lambda b, i, j: (b * nt + i, j)),
        out_shape=jax.ShapeDtypeStruct((T, D_FF), BF16),
        scratch_shapes=[pltpu.VMEM((tm + 2 * FFN_HALO, K), BF16)],
        name="up_conv_gate",
        compiler_params=_cparams(("parallel", "parallel", "arbitrary")),
    )(xb, xb, xb, w_up, w_up, conv_w, conv_w, conv_b, conv_b)


def _ln_ple_kernel(z_ref, g_ref, b_ref, wg_ref, p_ref, wp_ref, o_ref, ob_ref, x_s, xb_s):
    j = pl.program_id(1)
    tn = wg_ref.shape[1]

    @pl.when(j == 0)
    def _():
        z = z_ref[...]
        mu = jnp.mean(z, axis=-1, keepdims=True)
        zc = z - mu
        var = jnp.mean(jnp.square(zc), axis=-1, keepdims=True)
        x = zc * lax.rsqrt(var + LN_EPS) * g_ref[...] + b_ref[...]
        xb_s[...] = x.astype(BF16)
        for jj in range(x_s.shape[0]):
            x_s[jj] = x[:, jj * tn:(jj + 1) * tn]

    gate = _sigmoid(_dot(xb_s[...], wg_ref[...]))
    emb = _dot(p_ref[...].astype(BF16), wp_ref[...])
    out = x_s[j] + gate * emb
    o_ref[...] = out
    ob_ref[...] = out.astype(BF16)


def ln_ple_update(z, g, b, p, w_gate, w_ple, tm=512, tn=512):
    T, D = z.shape
    vec = pl.BlockSpec((1, D), lambda i, j: (0, 0))
    return pl.pallas_call(
        _ln_ple_kernel,
        grid=(T // tm, D // tn),
        in_specs=[pl.BlockSpec((tm, D), lambda i, j: (i, 0)), vec, vec,
                  pl.BlockSpec((D, tn), lambda i, j: (0, j)),
                  pl.BlockSpec((tm, PLE_DIM), lambda i, j: (i, 0)),
                  pl.BlockSpec((PLE_DIM, tn), lambda i, j: (0, j))],
        out_specs=[pl.BlockSpec((tm, tn), lambda i, j: (i, j)), pl.BlockSpec((tm, tn), lambda i, j: (i, j))],
        out_shape=[jax.ShapeDtypeStruct((T, D), F32), jax.ShapeDtypeStruct((T, D), BF16)],
        scratch_shapes=[pltpu.VMEM((D // tn, tm, tn), F32), pltpu.VMEM((tm, D), BF16)],
        name="ln_ple_update",
        compiler_params=_cparams(("parallel", "arbitrary")),
    )(z, g, b, w_gate, p, w_ple)


def _regroup_columns(a):
    mw = MLSTM_HEADS * HEAD_DIM
    aw = ATTN_HEADS * HEAD_DIM
    gkw = GLA_HEADS * GLA_HEAD_K
    gvw = GLA_HEADS * HEAD_DIM
    sizes = [mw] * 4 + [4 * MLSTM_HEADS] + [aw] * 3 + [gkw] * 2 + [gvw] * 2 + [GLA_RANK] * 2
    offs = [0]
    for sz in sizes:
        offs.append(offs[-1] + sz)
    seg = [a[..., offs[i]:offs[i + 1]] for i in range(len(sizes))]
    mq, mk, mv, mo, mgate, aq, ak, av, gq, gk, gv, gr, ga_f, ga_b = seg
    big = jnp.concatenate([gv, gr, gq, gk, aq, mq, mk, mv, mo, ak, av], axis=-1)
    small = jnp.concatenate([mgate, ga_f, ga_b], axis=-1)
    return big, small


def _regroup_kernel(src_ref, shift_ref, a_ref, b_ref, o_ref):
    c = pl.program_id(0)
    a = a_ref[...]
    shift = shift_ref[c]
    lane = lax.broadcasted_iota(jnp.int32, a.shape, 1)
    keep = LANES - IN_PROJ_MISALIGN
    shifted = jnp.where(lane < keep, pltpu.roll(a, shift=keep, axis=1), pltpu.roll(b_ref[...], shift=keep, axis=1))
    o_ref[...] = jnp.where(shift > 0, shifted, a).astype(o_ref.dtype)


IN_PROJ_MISALIGN = (4 * MLSTM_HEADS) % LANES


def regroup_input_weights(w_in_all, layer):
    K = w_in_all.shape[1]
    mw = MLSTM_HEADS * HEAD_DIM
    aw = ATTN_HEADS * HEAD_DIM
    gkw = GLA_HEADS * GLA_HEAD_K
    gvw = GLA_HEADS * HEAD_DIM
    o_mq, o_mk, o_mv, o_mo = 0, mw, 2 * mw, 3 * mw
    o_aq = 4 * mw + 4 * MLSTM_HEADS
    o_ak, o_av = o_aq + aw, o_aq + 2 * aw
    o_gq = o_aq + 3 * aw
    o_gk = o_gq + gkw
    o_gv = o_gk + gkw
    o_gr = o_gv + gvw
    order = [(o_gv, gvw), (o_gr, gvw), (o_gq, gkw), (o_gk, gkw), (o_aq, aw), (o_mq, mw), (o_mk, mw),
             (o_mv, mw), (o_mo, mw), (o_ak, aw), (o_av, aw)]
    src, shift = [], []
    for off, width in order:
        for g in range(width // LANES):
            col = off + g * LANES
            assert col % LANES in (0, IN_PROJ_MISALIGN)
            src.append(col // LANES)
            shift.append(col % LANES)
    n_groups = len(src)
    assert n_groups == N_PROJ_GROUPS
    grid_spec = pltpu.PrefetchScalarGridSpec(
        num_scalar_prefetch=2,
        grid=(n_groups,),
        in_specs=[pl.BlockSpec((None, K, LANES), lambda c, src, shift: (layer, 0, src[c])),
                  pl.BlockSpec((None, K, LANES), lambda c, src, shift: (layer, 0, src[c] + 1))],
        out_specs=pl.BlockSpec((K, LANES), lambda c, src, shift: (0, c)))
    return pl.pallas_call(
        _regroup_kernel,
        grid_spec=grid_spec,
        out_shape=jax.ShapeDtypeStruct((K, n_groups * LANES), BF16),
        name="regroup_input_weights",
        compiler_params=_cparams(("parallel",)),
    )(jnp.asarray(src, jnp.int32), jnp.asarray(shift, jnp.int32), w_in_all, w_in_all)


def _prepare_layer_params(layer, w_in_all, mlstm_gate_b, mlstm_norm_g, gla_w_a2, gla_a_b, gla_norm_g, w_out,
                          ln1_g, ln1_b, w_up, conv_w, conv_b, w_down, ln2_g, ln2_b, w_ple, w_ple_gate):
    n_gate = 4 * MLSTM_HEADS
    w_big = regroup_input_weights(w_in_all, layer)
    w_small = _regroup_columns(w_in_all[layer])[1]
    w_small = jnp.pad(w_small, ((0, 0), (0, LANES - w_small.shape[1]))).astype(BF16)
    bias = jnp.pad(mlstm_gate_b.reshape(-1), (0, LANES - n_gate)).astype(F32)
    wa = jnp.zeros((2, GLA_HEADS // 2, LANES, LANES), F32)
    a2 = gla_w_a2.reshape(2, GLA_RANK, GLA_HEADS // 2, LANES).transpose(0, 2, 1, 3)
    wa = wa.at[0, :, GATE_A_F:GATE_A_F + GLA_RANK, :].set(a2[0])
    wa = wa.at[1, :, GATE_A_B:GATE_A_B + GLA_RANK, :].set(a2[1])
    return dict(
        w_big=w_big, w_small=w_small, w_small_t=w_small.T,
        bias_c=bias.reshape(1, LANES), bias_r=bias.reshape(LANES, 1),
        mlstm_norm_g=mlstm_norm_g.reshape(MLSTM_HEADS, 1, HEAD_DIM),
        wa=wa.astype(BF16), a_bias=gla_a_b.reshape(2, GLA_HEADS // 2, 1, LANES),
        gla_norm_g=gla_norm_g.reshape(GLA_HEADS, 1, HEAD_DIM),
        w_out=w_out.astype(BF16), ln1_g=ln1_g.reshape(1, -1), ln1_b=ln1_b.reshape(1, -1),
        w_up=w_up.astype(BF16), conv_w=conv_w, conv_b=conv_b.reshape(1, -1), w_down=w_down.astype(BF16),
        ln2_g=ln2_g.reshape(1, -1), ln2_b=ln2_b.reshape(1, -1),
        w_ple=w_ple.astype(BF16), w_ple_gate=w_ple_gate.astype(BF16))


def _rope_tables(S):
    half = HEAD_DIM // 2
    inv = ROPE_THETA ** (-jnp.arange(half, dtype=F32) / half)
    ang = jnp.arange(S, dtype=F32)[:, None] * inv[None, :]
    cos, sin = jnp.cos(ang), jnp.sin(ang)
    return jnp.concatenate([cos, cos], axis=-1), jnp.concatenate([-sin, sin], axis=-1)


def _encoder_layer(x, xb, p_i, prm, tables, B, S):
    proj = proj_matmul(xb, prm['w_big'])
    gates = gates_project(xb, prm['w_small'], prm['w_small_t'], prm['bias_c'], prm['bias_r'], MLSTM_CHUNK)
    hf = mlstm_direction(proj, gates, B, S, reverse=False)
    y_m = mlstm_direction(proj, gates, B, S, reverse=True, hf=hf, norm_g=prm['mlstm_norm_g'])
    y_a = dilated_attention(proj, tables[0], tables[1], B, S)
    of = gla_direction(proj, gates[0], prm['wa'], prm['a_bias'], B, S, reverse=False)
    y_g = gla_direction(proj, gates[0], prm['wa'], prm['a_bias'], B, S, reverse=True, of=of,
                        norm_g=prm['gla_norm_g'])
    x, xb = mix_out_layer_norm(y_m, y_a, y_g, prm['w_out'], x, prm['ln1_g'], prm['ln1_b'])
    hmid = up_conv_gate(xb, prm['w_up'], prm['conv_w'], prm['conv_b'], B, S)
    z = matmul_residual(hmid, prm['w_down'], x, tm=512, tn=512)
    return ln_ple_update(z, prm['ln2_g'], prm['ln2_b'], p_i, prm['w_ple_gate'], prm['w_ple'])


def _trunk(x, p, layer_params):
    B, S, D = x.shape
    tables = _rope_tables(S)
    x = x.reshape(B * S, D)
    xb = x.astype(BF16)
    for i, prm in enumerate(layer_params):
        x, xb = _encoder_layer(x, xb, p[i].reshape(B * S, -1), prm, tables, B, S)
    return x.reshape(B, S, D)


def kernel(x_prompt, x_sample, p_prompt, p_sample, w_in, mlstm_gate_b, mlstm_norm_g, gla_w_a2, gla_a_b,
           gla_norm_g, w_out, ln1_g, ln1_b, w_up, conv_w, conv_b, w_down, ln2_g, ln2_b, w_ple, w_ple_gate):
    weights = (w_in, mlstm_gate_b, mlstm_norm_g, gla_w_a2, gla_a_b, gla_norm_g, w_out, ln1_g, ln1_b,
               w_up, conv_w, conv_b, w_down, ln2_g, ln2_b, w_ple, w_ple_gate)
    layer_params = [_prepare_layer_params(i, w_in, *(w[i] for w in weights[1:])) for i in range(w_in.shape[0])]
    return (_trunk(x_prompt, p_prompt, layer_params), _trunk(x_sample, p_sample, layer_params))
```

```python
import functools

import jax
import jax.numpy as jnp
from jax import lax
from jax.experimental import pallas as pl
from jax.experimental.pallas import tpu as pltpu

F32 = jnp.float32
BF16 = jnp.bfloat16

D_MODEL = 4096
HEAD_DIM = 128
MLSTM_HEADS = 8
ATTN_HEADS = 12
GLA_HEADS = 12
GLA_HEAD_K = 64
GLA_RANK = 16
GLA_TAU = 16.0
GLA_CHUNK = 64
ATTN_REACH = 64
ATTN_DILATIONS = (1, 4, 16)
ROPE_THETA = 10000.0
D_FF = 11008
PLE_DIM = 256
LN_EPS = 1e-5
HEAD_NORM_EPS = 1e-6
DEPTH = 2
DEEPNORM_ALPHA = (2 * DEPTH) ** 0.25

LANES = 128
VMEM_LIMIT = 56 * 1024 * 1024

G_GV, G_GR, G_GQ, G_GK = 0, 12, 24, 30
G_AQ = 36
G_MQ, G_MK, G_MV, G_MO = 48, 56, 64, 72
G_AK, G_AV = 80, 92
N_PROJ_GROUPS = 104
GATE_I_F, GATE_F_F, GATE_I_B, GATE_F_B, GATE_A_F, GATE_A_B = 0, 8, 16, 24, 32, 48

MLSTM_CHUNK = 256
GLA_STEP = 256
ATTN_TILE = 2048
ATTN_HALO = ATTN_REACH * ATTN_DILATIONS[-1]
ATTN_QB = 128
ATTN_KB = ATTN_QB + 2 * ATTN_REACH
ATTN_UNROLL = 8


def _cparams(sem):
    return pltpu.CompilerParams(dimension_semantics=sem, vmem_limit_bytes=VMEM_LIMIT)


def _log_sigmoid(x):
    return jnp.minimum(x, 0.0) - jnp.log1p(jnp.exp(-jnp.abs(x)))


def _sigmoid(x):
    return 1.0 / (1.0 + jnp.exp(-x))


def _dot(a, b):
    return jnp.dot(a, b, preferred_element_type=F32)


def _dot_nt(a, b):
    return lax.dot_general(a, b, (((1,), (1,)), ((), ())), preferred_element_type=F32)


def _dot_tn(a, b):
    return lax.dot_general(a, b, (((0,), (0,)), ((), ())), preferred_element_type=F32)


def _split3(x):
    hi = x.astype(BF16)
    r1 = x - hi.astype(F32)
    mid = r1.astype(BF16)
    lo = (r1 - mid.astype(F32)).astype(BF16)
    return hi, mid, lo


def _tri_dot(tri, x):
    hi, mid, lo = _split3(x)
    return _dot(tri, hi) + _dot(tri, mid) + _dot(tri, lo)


def _dot_tri(x, tri):
    hi, mid, lo = _split3(x)
    return _dot(hi, tri) + _dot(mid, tri) + _dot(lo, tri)


def _proj_kernel(x_ref, w_ref, o_ref):
    acc = _dot(x_ref[...], w_ref[...])
    for g in range(o_ref.shape[0]):
        o_ref[g] = acc[:, g * LANES:(g + 1) * LANES].astype(o_ref.dtype)


def proj_matmul(xb, w, tm=1024, tn=1024):
    T, K = xb.shape
    N = w.shape[1]
    return pl.pallas_call(
        _proj_kernel,
        grid=(T // tm, N // tn),
        in_specs=[pl.BlockSpec((tm, K), lambda i, j: (i, 0)),
                  pl.BlockSpec((K, tn), lambda i, j: (0, j))],
        out_specs=pl.BlockSpec((tn // LANES, tm, LANES), lambda i, j: (j, i, 0)),
        out_shape=jax.ShapeDtypeStruct((N // LANES, T, LANES), F32),
        name="proj_matmul",
        compiler_params=_cparams(("parallel", "parallel")),
    )(xb, w)


def _mm_kernel(x_ref, w_ref, o_ref):
    o_ref[...] = _dot(x_ref[...], w_ref[...]).astype(o_ref.dtype)


def matmul(xb, w, tm, tn, out_dtype=F32):
    T, K = xb.shape
    N = w.shape[1]
    return pl.pallas_call(
        _mm_kernel,
        grid=(T // tm, N // tn),
        in_specs=[pl.BlockSpec((tm, K), lambda i, j: (i, 0)),
                  pl.BlockSpec((K, tn), lambda i, j: (0, j))],
        out_specs=pl.BlockSpec((tm, tn), lambda i, j: (i, j)),
        out_shape=jax.ShapeDtypeStruct((T, N), out_dtype),
        name="matmul",
        compiler_params=_cparams(("parallel", "parallel")),
    )(xb, w)


def _mm_residual_kernel(x_ref, w_ref, r_ref, o_ref):
    o_ref[...] = DEEPNORM_ALPHA * r_ref[...] + _dot(x_ref[...], w_ref[...])


def matmul_residual(xb, w, layer, res, tm, tn):
    T, K = xb.shape
    N = w.shape[2]
    return pl.pallas_call(
        _mm_residual_kernel,
        grid=(T // tm, N // tn),
        in_specs=[pl.BlockSpec((tm, K), lambda i, j: (i, 0)),
                  pl.BlockSpec((None, K, tn), lambda i, j: (layer, 0, j)),
                  pl.BlockSpec((tm, tn), lambda i, j: (i, j))],
        out_specs=pl.BlockSpec((tm, tn), lambda i, j: (i, j)),
        out_shape=jax.ShapeDtypeStruct((T, N), F32),
        name="matmul_residual",
        compiler_params=_cparams(("parallel", "parallel")),
    )(xb, w, res)


def _mix_out_ln_kernel(ym_ref, ya_ref, yg_ref, w_ref, x_ref, g_ref, b_ref, o_ref, ob_ref):
    j = pl.program_id(1)
    tn = w_ref.shape[1]
    d = o_ref.shape[1]
    n_slabs = d // tn
    parts = ([ym_ref[g] for g in range(ym_ref.shape[0])] + [ya_ref[g] for g in range(ya_ref.shape[0])]
             + [yg_ref[g] for g in range(yg_ref.shape[0])])
    y = jnp.concatenate(parts, axis=-1)
    o_ref[:, pl.ds(pl.multiple_of(j * tn, tn), tn)] = DEEPNORM_ALPHA * x_ref[...] + _dot(y, w_ref[...])

    @pl.when(j == n_slabs - 1)
    def _():
        slabs = [slice(s * tn, (s + 1) * tn) for s in range(n_slabs)]
        mu = sum(jnp.sum(o_ref[:, c], axis=-1, keepdims=True) for c in slabs) * (1.0 / d)
        var = sum(jnp.sum(jnp.square(o_ref[:, c] - mu), axis=-1, keepdims=True) for c in slabs) * (1.0 / d)
        inv = lax.rsqrt(var + LN_EPS)
        for c in slabs:
            out = (o_ref[:, c] - mu) * inv * g_ref[:, c] + b_ref[:, c]
            o_ref[:, c] = out
            ob_ref[:, c] = out.astype(BF16)


def mix_out_layer_norm(ym, ya, yg, w, layer, x, g, b, tm=512, tn=512):
    T = ym.shape[1]
    K, N = w.shape[1], w.shape[2]

    def gspec(a):
        return pl.BlockSpec((a.shape[0], tm, LANES), lambda i, j: (0, i, 0))

    vec = pl.BlockSpec((1, N), lambda i, j: (0, 0))
    row = pl.BlockSpec((tm, N), lambda i, j: (i, 0))
    return pl.pallas_call(
        _mix_out_ln_kernel,
        grid=(T // tm, N // tn),
        in_specs=[gspec(ym), gspec(ya), gspec(yg), pl.BlockSpec((None, K, tn), lambda i, j: (layer, 0, j)),
                  pl.BlockSpec((tm, tn), lambda i, j: (i, j)), vec, vec],
        out_specs=[row, row],
        out_shape=[jax.ShapeDtypeStruct((T, N), F32), jax.ShapeDtypeStruct((T, N), BF16)],
        name="mix_out_layer_norm",
        compiler_params=_cparams(("parallel", "arbitrary")),
    )(ym, ya, yg, w, x, g, b)


def _gates_kernel(x_ref, w_ref, wt_ref, bc_ref, br_ref, pc_ref, cc_ref, pr_ref, cr_ref, *, chunk):
    x = x_ref[...]
    tm = x.shape[0]
    pc = _dot(x, w_ref[...]) + bc_ref[...]
    pr = _dot_nt(wt_ref[...], x) + br_ref[...]
    pc_ref[...] = pc
    pr_ref[...] = pr
    ls_c = _log_sigmoid(pc)
    ls_r = _log_sigmoid(pr)
    row = lax.broadcasted_iota(jnp.int32, (chunk, chunk), 0)
    col = lax.broadcasted_iota(jnp.int32, (chunk, chunk), 1)
    lower = (row >= col).astype(BF16)
    upper = (row <= col).astype(BF16)
    lane = lax.broadcasted_iota(jnp.int32, (chunk, LANES), 1)
    sub = lax.broadcasted_iota(jnp.int32, (LANES, chunk), 0)
    bwd_c = (lane >= GATE_F_B) & (lane < GATE_F_B + MLSTM_HEADS)
    bwd_r = (sub >= GATE_F_B) & (sub < GATE_F_B + MLSTM_HEADS)
    for c in range(tm // chunk):
        sl = slice(c * chunk, (c + 1) * chunk)
        lc = ls_c[sl]
        cc_ref[sl, :] = jnp.where(bwd_c, _tri_dot(upper, lc), _tri_dot(lower, lc))
        lr = ls_r[:, sl]
        cr_ref[:, sl] = jnp.where(bwd_r, _dot_tri(lr, lower), _dot_tri(lr, upper))


def gates_project(xb, w_small, w_small_t, bias_c, bias_r, chunk, tm=1024):
    T, K = xb.shape
    return pl.pallas_call(
        functools.partial(_gates_kernel, chunk=chunk),
        grid=(T // tm,),
        in_specs=[pl.BlockSpec((tm, K), lambda i: (i, 0)),
                  pl.BlockSpec((K, LANES), lambda i: (0, 0)),
                  pl.BlockSpec((LANES, K), lambda i: (0, 0)),
                  pl.BlockSpec((1, LANES), lambda i: (0, 0)),
                  pl.BlockSpec((LANES, 1), lambda i: (0, 0))],
        out_specs=[pl.BlockSpec((tm, LANES), lambda i: (i, 0)),
                   pl.BlockSpec((tm, LANES), lambda i: (i, 0)),
                   pl.BlockSpec((LANES, tm), lambda i: (0, i)),
                   pl.BlockSpec((LANES, tm), lambda i: (0, i))],
        out_shape=[jax.ShapeDtypeStruct((T, LANES), F32), jax.ShapeDtypeStruct((T, LANES), F32),
                   jax.ShapeDtypeStruct((LANES, T), F32), jax.ShapeDtypeStruct((LANES, T), F32)],
        name="gates_project",
        compiler_params=_cparams(("parallel",)),
    )(xb, w_small, w_small_t, bias_c, bias_r)


def _mlstm_kernel(*refs, reverse, final):
    if final:
        (q_ref, k_ref, v_ref, pc_ref, cc_ref, pr_ref, cr_ref, hf_ref, o_ref, g_ref,
         out_ref, ct_s, n_s, m_s) = refs
    else:
        q_ref, k_ref, v_ref, pc_ref, cc_ref, pr_ref, cr_ref, out_ref, ct_s, n_s, m_s = refs
    c = pl.program_id(1)
    H, L = q_ref.shape[0], q_ref.shape[1]

    @pl.when(c == 0)
    def _():
        ct_s[...] = jnp.zeros(ct_s.shape, F32)
        n_s[...] = jnp.zeros(n_s.shape, F32)
        m_s[...] = jnp.zeros(m_s.shape, F32)

    pc = pc_ref[...]
    cc = cc_ref[...]
    lane = lax.broadcasted_iota(jnp.int32, (L, LANES), 1)
    t_idx = lax.broadcasted_iota(jnp.int32, (L, L), 0)
    s_idx = lax.broadcasted_iota(jnp.int32, (L, L), 1)
    causal = (s_idx >= t_idx) if reverse else (s_idx <= t_idx)

    heads = range(H)
    icol = [(GATE_I_B if reverse else GATE_I_F) + h for h in heads]
    fcol = [(GATE_F_B if reverse else GATE_F_F) + h for h in heads]
    qf = [q_ref[h] for h in heads]
    kf = [k_ref[h] * (HEAD_DIM ** -0.5) for h in heads]
    qb = [x.astype(BF16) for x in qf]
    kb = [x.astype(BF16) for x in kf]
    vb = [v_ref[h].astype(BF16) for h in heads]
    ct = [ct_s[h] for h in heads]
    n_prev = [n_s[h] for h in heads]
    m_prev = [m_s[h][:, 0:1] for h in heads]

    i_col = [jnp.sum(jnp.where(lane == icol[h], pc, 0.0), axis=-1, keepdims=True) for h in heads]
    b_col = [jnp.sum(jnp.where(lane == fcol[h], cc, 0.0), axis=-1, keepdims=True) for h in heads]
    r_col = [i_col[h] - b_col[h] for h in heads]
    b_row = [cr_ref[fcol[h]:fcol[h] + 1, :] for h in heads]
    r_row = [pr_ref[icol[h]:icol[h] + 1, :] - b_row[h] for h in heads]

    r_mat = [jnp.where(causal, r_row[h], -jnp.inf) for h in heads]
    m_row = [jnp.maximum(jnp.max(r_mat[h], axis=-1, keepdims=True), m_prev[h]) for h in heads]
    scores = [_dot_nt(qb[h], kb[h]) for h in heads]
    inter = [_dot(qb[h], ct[h].astype(BF16)) for h in heads]
    p = [jnp.exp(r_mat[h] - m_row[h]) * scores[h] for h in heads]
    w_inter = [jnp.exp(m_prev[h] - m_row[h]) for h in heads]
    pv = [_dot(p[h].astype(BF16), vb[h]) for h in heads]
    nq = [w_inter[h] * jnp.sum(qf[h] * n_prev[h], axis=-1, keepdims=True)
          + jnp.sum(p[h], axis=-1, keepdims=True) for h in heads]
    den = [jnp.maximum(jnp.abs(nq[h]), jnp.exp(-(b_col[h] + m_row[h]))) for h in heads]
    hout = [(w_inter[h] * inter[h] + pv[h]) / den[h] for h in heads]

    g = [b_row[h][:, 0:1] if reverse else b_row[h][:, L - 1:L] for h in heads]
    r_max = [jnp.max(r_row[h], axis=-1, keepdims=True) for h in heads]
    m_loc = [g[h] + r_max[h] for h in heads]
    m_new = [jnp.maximum(g[h] + m_prev[h], m_loc[h]) for h in heads]
    s_old = [jnp.exp(g[h] + m_prev[h] - m_new[h]) for h in heads]
    s_new = [jnp.exp(m_loc[h] - m_new[h]) for h in heads]
    kw = [kf[h] * jnp.exp(r_col[h] - r_max[h]) for h in heads]
    ct_loc = [_dot_tn(kw[h].astype(BF16), vb[h]) for h in heads]
    for h in heads:
        ct_s[h] = s_old[h] * ct[h] + s_new[h] * ct_loc[h]
        n_s[h] = s_old[h] * n_prev[h] + s_new[h] * jnp.sum(kw[h], axis=0, keepdims=True)
        m_s[h] = jnp.broadcast_to(m_new[h], m_s.shape[1:])

    for h in heads:
        if final:
            hs = hf_ref[h] + hout[h]
            mu = jnp.mean(hs, axis=-1, keepdims=True)
            var = jnp.mean(jnp.square(hs - mu), axis=-1, keepdims=True)
            hn = (hs - mu) * lax.rsqrt(var + HEAD_NORM_EPS) * g_ref[h]
            out_ref[h] = (_sigmoid(o_ref[h]) * hn).astype(out_ref.dtype)
        else:
            out_ref[h] = hout[h]


def mlstm_direction(proj, gates, B, S, reverse, hf=None, norm_g=None):
    pc, cc, pr, cr = gates
    L = MLSTM_CHUNK
    nc = S // L
    T = B * S
    H = MLSTM_HEADS
    final = hf is not None

    def tok(b, c):
        return b * nc + (nc - 1 - c if reverse else c)

    def gspec(goff):
        return pl.BlockSpec((H, L, LANES), lambda b, c: (goff // H, tok(b, c), 0))

    in_specs = [gspec(G_MQ), gspec(G_MK), gspec(G_MV),
                pl.BlockSpec((L, LANES), lambda b, c: (tok(b, c), 0)),
                pl.BlockSpec((L, LANES), lambda b, c: (tok(b, c), 0)),
                pl.BlockSpec((LANES, L), lambda b, c: (0, tok(b, c))),
                pl.BlockSpec((LANES, L), lambda b, c: (0, tok(b, c)))]
    args = [proj, proj, proj, pc, cc, pr, cr]
    if final:
        in_specs += [gspec(0), gspec(G_MO), pl.BlockSpec((H, 1, LANES), lambda b, c: (0, 0, 0))]
        args += [hf, proj, norm_g]
    return pl.pallas_call(
        functools.partial(_mlstm_kernel, reverse=reverse, final=final),
        grid=(B, nc),
        in_specs=in_specs,
        out_specs=gspec(0),
        out_shape=jax.ShapeDtypeStruct((H, T, LANES), BF16 if final else F32),
        scratch_shapes=[pltpu.VMEM((H, HEAD_DIM, HEAD_DIM), F32), pltpu.VMEM((H, 1, HEAD_DIM), F32),
                        pltpu.VMEM((H, 1, LANES), F32)],
        name="mlstm_bwd" if reverse else "mlstm_fwd",
        compiler_params=_cparams(("arbitrary", "arbitrary")),
    )(*args)


def _gla_kernel(*refs, reverse, final):
    if final:
        (q_ref, k_ref, v_ref, pc_ref, wa_ref, ab_ref, of_ref, r_ref, g_ref, out_ref, st_s) = refs
    else:
        q_ref, k_ref, v_ref, pc_ref, wa_ref, ab_ref, out_ref, st_s = refs
    c = pl.program_id(1)
    NG, TS = q_ref.shape[0], q_ref.shape[1]
    L = GLA_CHUNK
    n_sub = TS // L

    @pl.when(c == 0)
    def _():
        st_s[...] = jnp.zeros(st_s.shape, F32)

    row = lax.broadcasted_iota(jnp.int32, (TS, TS), 0)
    col = lax.broadcasted_iota(jnp.int32, (TS, TS), 1)
    same_chunk = (row // L) == (col // L)
    causal = same_chunk & ((col >= row) if reverse else (col <= row))
    cum_mat = causal.astype(BF16)
    lane = lax.broadcasted_iota(jnp.int32, (TS, LANES), 1)
    head_mask = [lane < GLA_HEAD_K, lane >= GLA_HEAD_K]
    pcb = pc_ref[...].astype(BF16)
    order = list(range(n_sub - 1, -1, -1)) if reverse else list(range(n_sub))
    pairs = range(NG)
    heads = range(2 * NG)


    la = [_log_sigmoid(_dot(pcb, wa_ref[gi]) + ab_ref[gi]) * (1.0 / GLA_TAU) for gi in pairs]
    split = [_split3(x) for x in la]
    b = [_dot(cum_mat, hi) + _dot(cum_mat, mid) + _dot(cum_mat, lo) for hi, mid, lo in split]

    def per_chunk_rows(x, idx):
        return jnp.concatenate([jnp.broadcast_to(x[s * L + idx:s * L + idx + 1, :], (L, LANES))
                                for s in range(n_sub)], axis=0)

    g_full = [per_chunk_rows(x, 0 if reverse else L - 1) for x in b]
    b_mid = [per_chunk_rows(x, L // 2 - 1 if reverse else L // 2) for x in b]
    qf = [q_ref[gi] * (GLA_HEAD_K ** -0.5) for gi in pairs]
    kf = [k_ref[gi] for gi in pairs]
    qd = [qf[gi] * jnp.exp(b[gi] - b_mid[gi]) for gi in pairs]
    kd = [(kf[gi] * jnp.exp(b_mid[gi] - b[gi])).astype(BF16) for gi in pairs]
    kg = [kf[gi] * jnp.exp(g_full[gi] - b[gi]) for gi in pairs]
    qe = [qf[gi] * jnp.exp(b[gi]) for gi in pairs]
    decay = [[jnp.exp(b[gi][s * L:s * L + 1, :] if reverse else b[gi][s * L + L - 1:s * L + L, :])
              for s in range(n_sub)] for gi in pairs]

    vb = [v_ref[hd].astype(BF16) for hd in heads]
    qd_h = [jnp.where(head_mask[hd % 2], qd[hd // 2], 0.0).astype(BF16) for hd in heads]
    kg_h = [jnp.where(head_mask[hd % 2], kg[hd // 2], 0.0).astype(BF16) for hd in heads]
    qe_h = [jnp.where(head_mask[hd % 2], qe[hd // 2], 0.0).astype(BF16) for hd in heads]
    a = [_dot_nt(qd_h[hd], kd[hd // 2]) for hd in heads]
    a = [jnp.where(causal, x, 0.0).astype(BF16) for x in a]
    o_intra = [_dot(a[hd], vb[hd]) for hd in heads]

    st = [st_s[hd] for hd in heads]
    o_inter = [[None] * n_sub for _ in heads]
    for s in order:
        sl = slice(s * L, (s + 1) * L)
        for hd in heads:
            o_inter[hd][s] = _dot_nt(qe_h[hd][sl], st[hd].astype(BF16))
        st_loc = [_dot_tn(vb[hd][sl], kg_h[hd][sl]) for hd in heads]
        st = [st[hd] * decay[hd // 2][s] + st_loc[hd] for hd in heads]
    for hd in heads:
        st_s[hd] = st[hd]

    for hd in heads:
        o = o_intra[hd] + jnp.concatenate(o_inter[hd], axis=0)
        if final:
            ot = of_ref[hd] + o
            on = ot * lax.rsqrt(jnp.mean(jnp.square(ot), axis=-1, keepdims=True) + HEAD_NORM_EPS) * g_ref[hd]
            rr = r_ref[hd]
            out_ref[hd] = (on * (rr * _sigmoid(rr))).astype(out_ref.dtype)
        else:
            out_ref[hd] = o


def gla_direction(proj, pc, wa, a_bias, B, S, reverse, of=None, norm_g=None):
    TS = GLA_STEP
    nc = S // TS
    T = B * S
    NG = GLA_HEADS // 2
    final = of is not None
    d = 1 if reverse else 0

    def tok(b, c):
        return b * nc + (nc - 1 - c if reverse else c)

    def gspec(goff, n):
        return pl.BlockSpec((n, TS, LANES), lambda b, c: (goff // n, tok(b, c), 0))

    in_specs = [gspec(G_GQ, NG), gspec(G_GK, NG), gspec(G_GV, GLA_HEADS),
                pl.BlockSpec((TS, LANES), lambda b, c: (tok(b, c), 0)),
                pl.BlockSpec((None, NG, LANES, LANES), lambda b, c: (d, 0, 0, 0)),
                pl.BlockSpec((None, NG, 1, LANES), lambda b, c: (d, 0, 0, 0))]
    args = [proj, proj, proj, pc, wa, a_bias]
    if final:
        in_specs += [gspec(0, GLA_HEADS), gspec(G_GR, GLA_HEADS),
                     pl.BlockSpec((GLA_HEADS, 1, LANES), lambda b, c: (0, 0, 0))]
        args += [of, proj, norm_g]
    return pl.pallas_call(
        functools.partial(_gla_kernel, reverse=reverse, final=final),
        grid=(B, nc),
        in_specs=in_specs,
        out_specs=gspec(0, GLA_HEADS),
        out_shape=jax.ShapeDtypeStruct((GLA_HEADS, T, LANES), BF16 if final else F32),
        scratch_shapes=[pltpu.VMEM((GLA_HEADS, HEAD_DIM, LANES), F32)],
        name="gla_bwd" if reverse else "gla_fwd",
        compiler_params=_cparams(("arbitrary", "arbitrary")),
    )(*args)


def _rope(t, cos2, sin2):
    return t * cos2 + pltpu.roll(t, shift=HEAD_DIM // 2, axis=1) * sin2


def _attn_kernel(q_ref, kp_ref, km_ref, kn_ref, vp_ref, vm_ref, vn_ref,
                 cp_ref, cm_ref, cn_ref, sp_ref, sm_ref, sn_ref, out_ref,
                 q_s, k_s, v_s, q4_s, k4_s, v4_s, st0_s, st4_s, o_s, *, seq_len):
    i = pl.program_id(1)
    TQ, HALO, QB, KB = ATTN_TILE, ATTN_HALO, ATTN_QB, ATTN_KB
    R = ATTN_DILATIONS[1]
    t0 = i * TQ

    q_s[...] = _rope(q_ref[...], cm_ref[...], sm_ref[...]) * (HEAD_DIM ** -0.5)
    k_s[0:HALO, :] = _rope(kp_ref[...], cp_ref[...], sp_ref[...])
    k_s[HALO:HALO + TQ, :] = _rope(km_ref[...], cm_ref[...], sm_ref[...])
    k_s[HALO + TQ:, :] = _rope(kn_ref[...], cn_ref[...], sn_ref[...])
    v_s[0:HALO, :] = vp_ref[...]
    v_s[HALO:HALO + TQ, :] = vm_ref[...]
    v_s[HALO + TQ:, :] = vn_ref[...]
    for j in range(R):
        q4_s[j] = q_s[pl.ds(j, TQ // R, stride=R), :]
        k4_s[j] = k_s[pl.ds(j, (TQ + 2 * HALO) // R, stride=R), :]
        v4_s[j] = v_s[pl.ds(j, (TQ + 2 * HALO) // R, stride=R), :]

    qi = lax.broadcasted_iota(jnp.int32, (QB, KB), 0)
    ki = lax.broadcasted_iota(jnp.int32, (QB, KB), 1)
    band = (ki >= qi) & (ki <= qi + 2 * ATTN_REACH)
    kcol = lax.broadcasted_iota(jnp.int32, (1, KB), 1)

    def block_group(blocks):
        n = range(len(blocks))
        qb = [blocks[u][0]().astype(BF16) for u in n]
        kb = [blocks[u][1]().astype(BF16) for u in n]
        vb = [blocks[u][2]().astype(BF16) for u in n]
        s = [_dot_nt(qb[u], kb[u]) for u in n]
        kpos = [blocks[u][3] for u in n]
        s = [jnp.where(band & (kpos[u] >= 0) & (kpos[u] < seq_len), s[u], -jnp.inf) for u in n]
        m = [jnp.max(s[u], axis=-1, keepdims=True) for u in n]
        e = [jnp.exp(s[u] - m[u]) for u in n]
        acc = [_dot(e[u].astype(BF16), vb[u]) for u in n]
        for u in n:
            store = blocks[u][4]
            store(0, acc[u])
            store(1, jnp.broadcast_to(m[u], (QB, LANES)))
            store(2, jnp.broadcast_to(jnp.sum(e[u], axis=-1, keepdims=True), (QB, LANES)))

    def dense_block(it):
        q0 = it * QB
        k0 = HALO + q0 - ATTN_REACH

        def store(which, val):
            st0_s[which, pl.ds(q0, QB), :] = val
        return (lambda: q_s[pl.ds(q0, QB), :], lambda: k_s[pl.ds(k0, KB), :], lambda: v_s[pl.ds(k0, KB), :],
                t0 - HALO + k0 + kcol, store)

    def mod4_block(it):
        j = it % R
        a0 = (it // R) * QB
        k0 = HALO // R + a0 - ATTN_REACH

        def store(which, val):
            st4_s[0, which, j, pl.ds(a0, QB), :] = val
        return (lambda: q4_s[j, pl.ds(a0, QB), :], lambda: k4_s[j, pl.ds(k0, KB), :],
                lambda: v4_s[j, pl.ds(k0, KB), :], t0 - HALO + R * (k0 + kcol) + j, store)

    def mod16_block(r):
        j = r % R
        m0 = r // R

        def store(which, val):
            st4_s[1, which, j, pl.ds(m0, QB, stride=R), :] = val
        return (lambda: q4_s[j, pl.ds(m0, QB, stride=R), :], lambda: k4_s[j, pl.ds(m0, KB, stride=R), :],
                lambda: v4_s[j, pl.ds(m0, KB, stride=R), :], t0 - HALO + r + ATTN_DILATIONS[2] * kcol, store)

    for make_block in (dense_block, mod4_block, mod16_block):
        def body(g, carry, make_block=make_block):
            block_group([make_block(g * ATTN_UNROLL + u) for u in range(ATTN_UNROLL)])
            return carry

        lax.fori_loop(0, (TQ // QB) // ATTN_UNROLL, body, 0)

    for j in range(R):
        rows = pl.ds(j, TQ // R, stride=R)
        parts = [(st0_s[0, rows, :], st0_s[1, rows, :], st0_s[2, rows, :]),
                 (st4_s[0, 0, j], st4_s[0, 1, j], st4_s[0, 2, j]),
                 (st4_s[1, 0, j], st4_s[1, 1, j], st4_s[1, 2, j])]
        m_all = jnp.maximum(jnp.maximum(parts[0][1], parts[1][1]), parts[2][1])
        num = jnp.zeros((TQ // R, LANES), F32)
        den = jnp.zeros((TQ // R, LANES), F32)
        for acc, m, l in parts:
            w = jnp.exp(m - m_all)
            num = num + w * acc
            den = den + w * l
        o_s[rows, :] = num / den
    out_ref[...] = o_s[...].astype(out_ref.dtype)


def dilated_attention(proj, cos2, sin2, B, S):
    TQ, HALO = ATTN_TILE, ATTN_HALO
    nt = S // TQ
    T = B * S
    hpt = TQ // HALO
    n_halo = T // HALO
    KT = TQ + 2 * HALO
    R = ATTN_DILATIONS[1]

    def main(goff):
        return pl.BlockSpec((None, TQ, LANES), lambda b, i, h: (goff + h, b * nt + i, 0))

    def prev(goff):
        return pl.BlockSpec((None, HALO, LANES),
                            lambda b, i, h: (goff + h, jnp.maximum((b * nt + i) * hpt - 1, 0), 0))

    def nxt(goff):
        return pl.BlockSpec((None, HALO, LANES),
                            lambda b, i, h: (goff + h, jnp.minimum((b * nt + i + 1) * hpt, n_halo - 1), 0))

    t_prev = pl.BlockSpec((HALO, LANES), lambda b, i, h: (jnp.maximum(i * hpt - 1, 0), 0))
    t_main = pl.BlockSpec((TQ, LANES), lambda b, i, h: (i, 0))
    t_next = pl.BlockSpec((HALO, LANES), lambda b, i, h: (jnp.minimum((i + 1) * hpt, S // HALO - 1), 0))

    return pl.pallas_call(
        functools.partial(_attn_kernel, seq_len=S),
        grid=(B, nt, ATTN_HEADS),
        in_specs=[main(G_AQ), prev(G_AK), main(G_AK), nxt(G_AK), prev(G_AV), main(G_AV), nxt(G_AV),
                  t_prev, t_main, t_next, t_prev, t_main, t_next],
        out_specs=main(0),
        out_shape=jax.ShapeDtypeStruct((ATTN_HEADS, T, LANES), BF16),
        scratch_shapes=[pltpu.VMEM((TQ, LANES), F32), pltpu.VMEM((KT, LANES), F32), pltpu.VMEM((KT, LANES), F32),
                        pltpu.VMEM((R, TQ // R, LANES), F32), pltpu.VMEM((R, KT // R, LANES), F32),
                        pltpu.VMEM((R, KT // R, LANES), F32), pltpu.VMEM((3, TQ, LANES), F32),
                        pltpu.VMEM((2, 3, R, TQ // R, LANES), F32), pltpu.VMEM((TQ, LANES), F32)],
        name="dilated_attention",
        compiler_params=_cparams(("parallel", "parallel", "arbitrary")),
    )(proj, proj, proj, proj, proj, proj, proj, cos2, cos2, cos2, sin2, sin2, sin2)


def _ln_kernel(x_ref, y_ref, g_ref, b_ref, o_ref, ob_ref):
    z = DEEPNORM_ALPHA * x_ref[...] + y_ref[...]
    mu = jnp.mean(z, axis=-1, keepdims=True)
    zc = z - mu
    var = jnp.mean(jnp.square(zc), axis=-1, keepdims=True)
    out = zc * lax.rsqrt(var + LN_EPS) * g_ref[...] + b_ref[...]
    o_ref[...] = out
    ob_ref[...] = out.astype(BF16)


def residual_layer_norm(x, y, g, b, tr=256):
    T, D = x.shape
    row = pl.BlockSpec((tr, D), lambda i: (i, 0))
    vec = pl.BlockSpec((1, D), lambda i: (0, 0))
    return pl.pallas_call(
        _ln_kernel,
        grid=(T // tr,),
        in_specs=[row, row, vec, vec],
        out_specs=[row, row],
        out_shape=[jax.ShapeDtypeStruct((T, D), F32), jax.ShapeDtypeStruct((T, D), BF16)],
        name="residual_layer_norm",
        compiler_params=_cparams(("parallel",)),
    )(x, y, g, b)


FFN_HALO = 16

def _up_conv_gate_kernel(xp_ref, xm_ref, xn_ref, wg_ref, wv_ref, cwg_ref, cwv_ref, cbg_ref, cbv_ref, o_ref,
                         xs_ref, *, n_tiles):
    i = pl.program_id(1)
    j = pl.program_id(2)
    tm = xm_ref.shape[0]
    hr = xp_ref.shape[0]

    @pl.when(j == 0)
    def _():
        xs_ref[0:hr, :] = xp_ref[...] * (i > 0).astype(BF16)
        xs_ref[hr:hr + tm, :] = xm_ref[...]
        xs_ref[hr + tm:, :] = xn_ref[...] * (i < n_tiles - 1).astype(BF16)

    xs = xs_ref[...]

    def conv(w_ref, cw_ref, cb_ref):
        u = _dot(xs, w_ref[...])
        return (cb_ref[...] + u[hr - 1:hr - 1 + tm] * cw_ref[0:1, :] + u[hr:hr + tm] * cw_ref[1:2, :]
                + u[hr + 1:hr + 1 + tm] * cw_ref[2:3, :])

    gate = conv(wg_ref, cwg_ref, cbg_ref)
    val = conv(wv_ref, cwv_ref, cbv_ref)
    o_ref[...] = (gate * _sigmoid(gate) * val).astype(o_ref.dtype)


def up_conv_gate(xb, w_up, layer, conv_w, conv_b, B, S, tm=1024, tf=256):
    T, K = xb.shape
    nt = S // tm
    nf = D_FF // tf
    hb = tm // FFN_HALO
    n_hb = T // FFN_HALO

    def wspec(off, r):
        return pl.BlockSpec((r, tf), lambda b, i, j: (0, off + j))

    def upspec(off):
        return pl.BlockSpec((None, K, tf), lambda b, i, j: (layer, 0, off + j))

    return pl.pallas_call(
        functools.partial(_up_conv_gate_kernel, n_tiles=nt),
        grid=(B, nt, nf),
        in_specs=[pl.BlockSpec((FFN_HALO, K), lambda b, i, j: (jnp.maximum((b * nt + i) * hb - 1, 0), 0)),
                  pl.BlockSpec((tm, K), lambda b, i, j: (b * nt + i, 0)),
                  pl.BlockSpec((FFN_HALO, K), lambda b, i, j: (jnp.minimum((b * nt + i + 1) * hb, n_hb - 1), 0)),
                  upspec(0), upspec(nf), wspec(0, 3), wspec(nf, 3), wspec(0, 1), wspec(nf, 1)],
        out_specs=pl.BlockSpec((tm, tf), lambda b, i, j: (b * nt + i, j)),
        out_shape=jax.ShapeDtypeStruct((T, D_FF), BF16),
        scratch_shapes=[pltpu.VMEM((tm + 2 * FFN_HALO, K), BF16)],
        name="up_conv_gate",
        compiler_params=_cparams(("parallel", "parallel", "arbitrary")),
    )(xb, xb, xb, w_up, w_up, conv_w, conv_w, conv_b, conv_b)


def _ln_ple_kernel(z_ref, g_ref, b_ref, wg_ref, p_ref, wp_ref, o_ref, ob_ref, x_s, xb_s):
    j = pl.program_id(1)
    tn = wg_ref.shape[1]

    @pl.when(j == 0)
    def _():
        z = z_ref[...]
        mu = jnp.mean(z, axis=-1, keepdims=True)
        zc = z - mu
        var = jnp.mean(jnp.square(zc), axis=-1, keepdims=True)
        x = zc * lax.rsqrt(var + LN_EPS) * g_ref[...] + b_ref[...]
        xb_s[...] = x.astype(BF16)
        for jj in range(x_s.shape[0]):
            x_s[jj] = x[:, jj * tn:(jj + 1) * tn]

    gate = _sigmoid(_dot(xb_s[...], wg_ref[...]))
    emb = _dot(p_ref[...].astype(BF16), wp_ref[...])
    out = x_s[j] + gate * emb
    o_ref[...] = out
    ob_ref[...] = out.astype(BF16)


def ln_ple_update(z, g, b, p, w_gate, w_ple, layer, tm=512, tn=512):
    T, D = z.shape
    vec = pl.BlockSpec((1, D), lambda i, j: (0, 0))
    return pl.pallas_call(
        _ln_ple_kernel,
        grid=(T // tm, D // tn),
        in_specs=[pl.BlockSpec((tm, D), lambda i, j: (i, 0)), vec, vec,
                  pl.BlockSpec((None, D, tn), lambda i, j: (layer, 0, j)),
                  pl.BlockSpec((tm, PLE_DIM), lambda i, j: (i, 0)),
                  pl.BlockSpec((None, PLE_DIM, tn), lambda i, j: (layer, 0, j))],
        out_specs=[pl.BlockSpec((tm, tn), lambda i, j: (i, j)), pl.BlockSpec((tm, tn), lambda i, j: (i, j))],
        out_shape=[jax.ShapeDtypeStruct((T, D), F32), jax.ShapeDtypeStruct((T, D), BF16)],
        scratch_shapes=[pltpu.VMEM((D // tn, tm, tn), F32), pltpu.VMEM((tm, D), BF16)],
        name="ln_ple_update",
        compiler_params=_cparams(("parallel", "arbitrary")),
    )(z, g, b, w_gate, p, w_ple)


def _regroup_columns(a):
    mw = MLSTM_HEADS * HEAD_DIM
    aw = ATTN_HEADS * HEAD_DIM
    gkw = GLA_HEADS * GLA_HEAD_K
    gvw = GLA_HEADS * HEAD_DIM
    sizes = [mw] * 4 + [4 * MLSTM_HEADS] + [aw] * 3 + [gkw] * 2 + [gvw] * 2 + [GLA_RANK] * 2
    offs = [0]
    for sz in sizes:
        offs.append(offs[-1] + sz)
    seg = [a[..., offs[i]:offs[i + 1]] for i in range(len(sizes))]
    mq, mk, mv, mo, mgate, aq, ak, av, gq, gk, gv, gr, ga_f, ga_b = seg
    big = jnp.concatenate([gv, gr, gq, gk, aq, mq, mk, mv, mo, ak, av], axis=-1)
    small = jnp.concatenate([mgate, ga_f, ga_b], axis=-1)
    return big, small


def _regroup_kernel(src_a_ref, src_b_ref, shift_ref, a_ref, b_ref, o_ref, small_ref):
    c = pl.program_id(0)
    a = a_ref[...]
    b = b_ref[...]
    shift = shift_ref[c]
    lane = lax.broadcasted_iota(jnp.int32, a.shape, 1)
    keep = LANES - IN_PROJ_MISALIGN
    shifted = jnp.where(lane < keep, pltpu.roll(a, shift=keep, axis=1), pltpu.roll(b, shift=keep, axis=1))

    @pl.when(c < N_PROJ_GROUPS)
    def _():
        o_ref[...] = jnp.where(shift > 0, shifted, a).astype(o_ref.dtype)

    @pl.when(c == N_PROJ_GROUPS)
    def _():
        n_gate = 4 * MLSTM_HEADS
        small = jnp.where(lane < n_gate, a, jnp.where(lane < n_gate + 2 * GLA_RANK, b, 0.0))
        small_ref[...] = small.astype(small_ref.dtype)


IN_PROJ_MISALIGN = (4 * MLSTM_HEADS) % LANES


def regroup_input_weights(w_in_all, layer):
    K = w_in_all.shape[1]
    mw = MLSTM_HEADS * HEAD_DIM
    aw = ATTN_HEADS * HEAD_DIM
    gkw = GLA_HEADS * GLA_HEAD_K
    gvw = GLA_HEADS * HEAD_DIM
    o_mq, o_mk, o_mv, o_mo = 0, mw, 2 * mw, 3 * mw
    o_aq = 4 * mw + 4 * MLSTM_HEADS
    o_ak, o_av = o_aq + aw, o_aq + 2 * aw
    o_gq = o_aq + 3 * aw
    o_gk = o_gq + gkw
    o_gv = o_gk + gkw
    o_gr = o_gv + gvw
    order = [(o_gv, gvw), (o_gr, gvw), (o_gq, gkw), (o_gk, gkw), (o_aq, aw), (o_mq, mw), (o_mk, mw),
             (o_mv, mw), (o_mo, mw), (o_ak, aw), (o_av, aw)]
    src, shift = [], []
    for off, width in order:
        for g in range(width // LANES):
            col = off + g * LANES
            assert col % LANES in (0, IN_PROJ_MISALIGN)
            src.append(col // LANES)
            shift.append(col % LANES)
    n_groups = len(src)
    assert n_groups == N_PROJ_GROUPS and o_gr + gvw == n_groups * LANES + IN_PROJ_MISALIGN
    src_a = src + [4 * mw // LANES]
    src_b = [s + 1 for s in src] + [(o_gr + gvw) // LANES]
    shift = shift + [0]
    grid_spec = pltpu.PrefetchScalarGridSpec(
        num_scalar_prefetch=3,
        grid=(n_groups + 1,),
        in_specs=[pl.BlockSpec((None, K, LANES), lambda c, sa, sb, sh: (layer, 0, sa[c])),
                  pl.BlockSpec((None, K, LANES), lambda c, sa, sb, sh: (layer, 0, sb[c]))],
        out_specs=[pl.BlockSpec((K, LANES), lambda c, sa, sb, sh: (0, jnp.minimum(c, n_groups - 1))),
                   pl.BlockSpec((K, LANES), lambda c, sa, sb, sh: (0, 0))])
    return pl.pallas_call(
        _regroup_kernel,
        grid_spec=grid_spec,
        out_shape=[jax.ShapeDtypeStruct((K, n_groups * LANES), BF16), jax.ShapeDtypeStruct((K, LANES), BF16)],
        name="regroup_input_weights",
        compiler_params=_cparams(("arbitrary",)),
    )(jnp.asarray(src_a, jnp.int32), jnp.asarray(src_b, jnp.int32), jnp.asarray(shift, jnp.int32),
      w_in_all, w_in_all)


def _prepare_layer_params(layer, w_in_all, big, mlstm_gate_b, mlstm_norm_g, gla_w_a2, gla_a_b, gla_norm_g,
                          ln1_g, ln1_b, conv_w, conv_b, ln2_g, ln2_b):
    n_gate = 4 * MLSTM_HEADS
    w_big, w_small = regroup_input_weights(w_in_all, layer)
    bias = jnp.pad(mlstm_gate_b.reshape(-1), (0, LANES - n_gate)).astype(F32)
    wa = jnp.zeros((2, GLA_HEADS // 2, LANES, LANES), F32)
    a2 = gla_w_a2.reshape(2, GLA_RANK, GLA_HEADS // 2, LANES).transpose(0, 2, 1, 3)
    wa = wa.at[0, :, GATE_A_F:GATE_A_F + GLA_RANK, :].set(a2[0])
    wa = wa.at[1, :, GATE_A_B:GATE_A_B + GLA_RANK, :].set(a2[1])
    return dict(
        w_big=w_big, w_small=w_small, w_small_t=w_small.T,
        bias_c=bias.reshape(1, LANES), bias_r=bias.reshape(LANES, 1),
        mlstm_norm_g=mlstm_norm_g.reshape(MLSTM_HEADS, 1, HEAD_DIM),
        wa=wa.astype(BF16), a_bias=gla_a_b.reshape(2, GLA_HEADS // 2, 1, LANES),
        gla_norm_g=gla_norm_g.reshape(GLA_HEADS, 1, HEAD_DIM),
        ln1_g=ln1_g.reshape(1, -1), ln1_b=ln1_b.reshape(1, -1), conv_w=conv_w, conv_b=conv_b.reshape(1, -1),
        ln2_g=ln2_g.reshape(1, -1), ln2_b=ln2_b.reshape(1, -1), layer=layer, **big)


def _rope_tables(S):
    half = HEAD_DIM // 2
    inv = ROPE_THETA ** (-jnp.arange(half, dtype=F32) / half)
    ang = jnp.arange(S, dtype=F32)[:, None] * inv[None, :]
    cos, sin = jnp.cos(ang), jnp.sin(ang)
    return jnp.concatenate([cos, cos], axis=-1), jnp.concatenate([-sin, sin], axis=-1)


def _encoder_layer(x, xb, p_i, prm, tables, B, S):
    proj = proj_matmul(xb, prm['w_big'])
    gates = gates_project(xb, prm['w_small'], prm['w_small_t'], prm['bias_c'], prm['bias_r'], MLSTM_CHUNK)
    hf = mlstm_direction(proj, gates, B, S, reverse=False)
    y_m = mlstm_direction(proj, gates, B, S, reverse=True, hf=hf, norm_g=prm['mlstm_norm_g'])
    y_a = dilated_attention(proj, tables[0], tables[1], B, S)
    of = gla_direction(proj, gates[0], prm['wa'], prm['a_bias'], B, S, reverse=False)
    y_g = gla_direction(proj, gates[0], prm['wa'], prm['a_bias'], B, S, reverse=True, of=of,
                        norm_g=prm['gla_norm_g'])
    layer = prm['layer']
    x, xb = mix_out_layer_norm(y_m, y_a, y_g, prm['w_out'], layer, x, prm['ln1_g'], prm['ln1_b'])
    hmid = up_conv_gate(xb, prm['w_up'], layer, prm['conv_w'], prm['conv_b'], B, S)
    z = matmul_residual(hmid, prm['w_down'], layer, x, tm=512, tn=512)
    return ln_ple_update(z, prm['ln2_g'], prm['ln2_b'], p_i, prm['w_ple_gate'], prm['w_ple'], layer)


def _trunk(x, p, layer_params):
    B, S, D = x.shape
    tables = _rope_tables(S)
    x = x.reshape(B * S, D)
    xb = x.astype(BF16)
    for i, prm in enumerate(layer_params):
        x, xb = _encoder_layer(x, xb, p[i].reshape(B * S, -1), prm, tables, B, S)
    return x.reshape(B, S, D)


def kernel(x_prompt, x_sample, p_prompt, p_sample, w_in, mlstm_gate_b, mlstm_norm_g, gla_w_a2, gla_a_b,
           gla_norm_g, w_out, ln1_g, ln1_b, w_up, conv_w, conv_b, w_down, ln2_g, ln2_b, w_ple, w_ple_gate):
    big = dict(w_out=w_out.astype(BF16), w_up=w_up.astype(BF16), w_down=w_down.astype(BF16),
               w_ple=w_ple.astype(BF16), w_ple_gate=w_ple_gate.astype(BF16))
    small = (mlstm_gate_b, mlstm_norm_g, gla_w_a2, gla_a_b, gla_norm_g, ln1_g, ln1_b, conv_w, conv_b, ln2_g, ln2_b)
    layer_params = [_prepare_layer_params(i, w_in, big, *(w[i] for w in small)) for i in range(w_in.shape[0])]
    return (_trunk(x_prompt, p_prompt, layer_params), _trunk(x_sample, p_sample, layer_params))
```

```python
import functools

import jax
import jax.numpy as jnp
from jax import lax
from jax.experimental import pallas as pl
from jax.experimental.pallas import tpu as pltpu

F32 = jnp.float32
BF16 = jnp.bfloat16

D_MODEL = 4096
HEAD_DIM = 128
MLSTM_HEADS = 8
ATTN_HEADS = 12
GLA_HEADS = 12
GLA_HEAD_K = 64
GLA_RANK = 16
GLA_TAU = 16.0
GLA_CHUNK = 64
ATTN_REACH = 64
ATTN_DILATIONS = (1, 4, 16)
ROPE_THETA = 10000.0
D_FF = 11008
PLE_DIM = 256
LN_EPS = 1e-5
HEAD_NORM_EPS = 1e-6
DEPTH = 2
DEEPNORM_ALPHA = (2 * DEPTH) ** 0.25

LANES = 128
VMEM_LIMIT = 56 * 1024 * 1024

G_GV, G_GR, G_GQ, G_GK = 0, 12, 24, 30
G_AQ = 36
G_MQ, G_MK, G_MV, G_MO = 48, 56, 64, 72
G_AK, G_AV = 80, 92
N_PROJ_GROUPS = 104
GATE_I_F, GATE_F_F, GATE_I_B, GATE_F_B, GATE_A_F, GATE_A_B = 0, 8, 16, 24, 32, 48

MLSTM_CHUNK = 256
GLA_STEP = 256
ATTN_TILE = 2048
ATTN_HALO = ATTN_REACH * ATTN_DILATIONS[-1]
ATTN_QB = 128
ATTN_KB = ATTN_QB + 2 * ATTN_REACH
ATTN_UNROLL = 8


def _cparams(sem):
    return pltpu.CompilerParams(dimension_semantics=sem, vmem_limit_bytes=VMEM_LIMIT)


def _log_sigmoid(x):
    return jnp.minimum(x, 0.0) - jnp.log1p(jnp.exp(-jnp.abs(x)))


def _sigmoid(x):
    return 1.0 / (1.0 + jnp.exp(-x))


def _dot(a, b):
    return jnp.dot(a, b, preferred_element_type=F32)


def _dot_nt(a, b):
    return lax.dot_general(a, b, (((1,), (1,)), ((), ())), preferred_element_type=F32)


def _dot_tn(a, b):
    return lax.dot_general(a, b, (((0,), (0,)), ((), ())), preferred_element_type=F32)


def _split3(x):
    hi = x.astype(BF16)
    r1 = x - hi.astype(F32)
    mid = r1.astype(BF16)
    lo = (r1 - mid.astype(F32)).astype(BF16)
    return hi, mid, lo


def _tri_dot(tri, x):
    hi, mid, lo = _split3(x)
    return _dot(tri, hi) + _dot(tri, mid) + _dot(tri, lo)


def _dot_tri(x, tri):
    hi, mid, lo = _split3(x)
    return _dot(hi, tri) + _dot(mid, tri) + _dot(lo, tri)


def _proj_kernel(x_ref, wt_ref, o_ref):
    acc = _dot_nt(x_ref[...], wt_ref[...])
    for g in range(o_ref.shape[0]):
        o_ref[g] = acc[:, g * LANES:(g + 1) * LANES].astype(o_ref.dtype)


def proj_matmul(xb, wt, tm=1024, tn=1024):
    T, K = xb.shape
    N = wt.shape[0]
    return pl.pallas_call(
        _proj_kernel,
        grid=(T // tm, N // tn),
        in_specs=[pl.BlockSpec((tm, K), lambda i, j: (i, 0)),
                  pl.BlockSpec((tn, K), lambda i, j: (j, 0))],
        out_specs=pl.BlockSpec((tn // LANES, tm, LANES), lambda i, j: (j, i, 0)),
        out_shape=jax.ShapeDtypeStruct((N // LANES, T, LANES), F32),
        name="proj_matmul",
        compiler_params=_cparams(("parallel", "parallel")),
    )(xb, wt)


def _mm_residual_kernel(x_ref, w_ref, r_ref, o_ref):
    o_ref[...] = DEEPNORM_ALPHA * r_ref[...] + _dot(x_ref[...], w_ref[...])


def matmul_residual(xb, w, layer, res, tm, tn):
    T, K = xb.shape
    N = w.shape[2]
    return pl.pallas_call(
        _mm_residual_kernel,
        grid=(T // tm, N // tn),
        in_specs=[pl.BlockSpec((tm, K), lambda i, j: (i, 0)),
                  pl.BlockSpec((None, K, tn), lambda i, j: (layer, 0, j)),
                  pl.BlockSpec((tm, tn), lambda i, j: (i, j))],
        out_specs=pl.BlockSpec((tm, tn), lambda i, j: (i, j)),
        out_shape=jax.ShapeDtypeStruct((T, N), F32),
        name="matmul_residual",
        compiler_params=_cparams(("parallel", "parallel")),
    )(xb, w, res)


def _mix_out_ln_kernel(ym_ref, ya_ref, yg_ref, w_ref, x_ref, g_ref, b_ref, o_ref, ob_ref):
    j = pl.program_id(1)
    tn = w_ref.shape[1]
    d = o_ref.shape[1]
    n_slabs = d // tn
    parts = ([ym_ref[g] for g in range(ym_ref.shape[0])] + [ya_ref[g] for g in range(ya_ref.shape[0])]
             + [yg_ref[g] for g in range(yg_ref.shape[0])])
    y = jnp.concatenate(parts, axis=-1)
    o_ref[:, pl.ds(pl.multiple_of(j * tn, tn), tn)] = DEEPNORM_ALPHA * x_ref[...] + _dot(y, w_ref[...])

    @pl.when(j == n_slabs - 1)
    def _():
        slabs = [slice(s * tn, (s + 1) * tn) for s in range(n_slabs)]
        mu = sum(jnp.sum(o_ref[:, c], axis=-1, keepdims=True) for c in slabs) * (1.0 / d)
        var = sum(jnp.sum(jnp.square(o_ref[:, c] - mu), axis=-1, keepdims=True) for c in slabs) * (1.0 / d)
        inv = lax.rsqrt(var + LN_EPS)
        for c in slabs:
            out = (o_ref[:, c] - mu) * inv * g_ref[:, c] + b_ref[:, c]
            o_ref[:, c] = out
            ob_ref[:, c] = out.astype(BF16)


def mix_out_layer_norm(ym, ya, yg, w, layer, x, g, b, tm=512, tn=512):
    T = ym.shape[1]
    K, N = w.shape[1], w.shape[2]

    def gspec(a):
        return pl.BlockSpec((a.shape[0], tm, LANES), lambda i, j: (0, i, 0))

    vec = pl.BlockSpec((1, N), lambda i, j: (0, 0))
    row = pl.BlockSpec((tm, N), lambda i, j: (i, 0))
    return pl.pallas_call(
        _mix_out_ln_kernel,
        grid=(T // tm, N // tn),
        in_specs=[gspec(ym), gspec(ya), gspec(yg), pl.BlockSpec((None, K, tn), lambda i, j: (layer, 0, j)),
                  pl.BlockSpec((tm, tn), lambda i, j: (i, j)), vec, vec],
        out_specs=[row, row],
        out_shape=[jax.ShapeDtypeStruct((T, N), F32), jax.ShapeDtypeStruct((T, N), BF16)],
        name="mix_out_layer_norm",
        compiler_params=_cparams(("parallel", "arbitrary")),
    )(ym, ya, yg, w, x, g, b)


def _gates_kernel(x_ref, w_ref, wt_ref, bc_ref, br_ref, pc_ref, cc_ref, pr_ref, cr_ref, *, chunk):
    x = x_ref[...]
    tm = x.shape[0]
    pc = _dot(x, w_ref[...]) + bc_ref[...]
    pr = _dot_nt(wt_ref[...], x) + br_ref[...]
    pc_ref[...] = pc
    pr_ref[...] = pr
    ls_c = _log_sigmoid(pc)
    ls_r = _log_sigmoid(pr)
    row = lax.broadcasted_iota(jnp.int32, (chunk, chunk), 0)
    col = lax.broadcasted_iota(jnp.int32, (chunk, chunk), 1)
    lower = (row >= col).astype(BF16)
    upper = (row <= col).astype(BF16)
    lane = lax.broadcasted_iota(jnp.int32, (chunk, LANES), 1)
    sub = lax.broadcasted_iota(jnp.int32, (LANES, chunk), 0)
    bwd_c = (lane >= GATE_F_B) & (lane < GATE_F_B + MLSTM_HEADS)
    bwd_r = (sub >= GATE_F_B) & (sub < GATE_F_B + MLSTM_HEADS)
    for c in range(tm // chunk):
        sl = slice(c * chunk, (c + 1) * chunk)
        lc = ls_c[sl]
        cc_ref[sl, :] = jnp.where(bwd_c, _tri_dot(upper, lc), _tri_dot(lower, lc))
        lr = ls_r[:, sl]
        cr_ref[:, sl] = jnp.where(bwd_r, _dot_tri(lr, lower), _dot_tri(lr, upper))


def gates_project(xb, w_small, w_small_t, bias_c, bias_r, chunk, tm=1024):
    T, K = xb.shape
    return pl.pallas_call(
        functools.partial(_gates_kernel, chunk=chunk),
        grid=(T // tm,),
        in_specs=[pl.BlockSpec((tm, K), lambda i: (i, 0)),
                  pl.BlockSpec((K, LANES), lambda i: (0, 0)),
                  pl.BlockSpec((LANES, K), lambda i: (0, 0)),
                  pl.BlockSpec((1, LANES), lambda i: (0, 0)),
                  pl.BlockSpec((LANES, 1), lambda i: (0, 0))],
        out_specs=[pl.BlockSpec((tm, LANES), lambda i: (i, 0)),
                   pl.BlockSpec((tm, LANES), lambda i: (i, 0)),
                   pl.BlockSpec((LANES, tm), lambda i: (0, i)),
                   pl.BlockSpec((LANES, tm), lambda i: (0, i))],
        out_shape=[jax.ShapeDtypeStruct((T, LANES), F32), jax.ShapeDtypeStruct((T, LANES), F32),
                   jax.ShapeDtypeStruct((LANES, T), F32), jax.ShapeDtypeStruct((LANES, T), F32)],
        name="gates_project",
        compiler_params=_cparams(("parallel",)),
    )(xb, w_small, w_small_t, bias_c, bias_r)


def _mlstm_kernel(*refs, reverse, final):
    if final:
        (q_ref, k_ref, v_ref, pc_ref, cc_ref, pr_ref, cr_ref, hf_ref, o_ref, g_ref,
         out_ref, ct_s, n_s, m_s) = refs
    else:
        q_ref, k_ref, v_ref, pc_ref, cc_ref, pr_ref, cr_ref, out_ref, ct_s, n_s, m_s = refs
    c = pl.program_id(1)
    H, L = q_ref.shape[0], q_ref.shape[1]

    @pl.when(c == 0)
    def _():
        ct_s[...] = jnp.zeros(ct_s.shape, F32)
        n_s[...] = jnp.zeros(n_s.shape, F32)
        m_s[...] = jnp.zeros(m_s.shape, F32)

    pc = pc_ref[...]
    cc = cc_ref[...]
    lane = lax.broadcasted_iota(jnp.int32, (L, LANES), 1)
    t_idx = lax.broadcasted_iota(jnp.int32, (L, L), 0)
    s_idx = lax.broadcasted_iota(jnp.int32, (L, L), 1)
    causal = (s_idx >= t_idx) if reverse else (s_idx <= t_idx)

    heads = range(H)
    icol = [(GATE_I_B if reverse else GATE_I_F) + h for h in heads]
    fcol = [(GATE_F_B if reverse else GATE_F_F) + h for h in heads]
    qf = [q_ref[h] for h in heads]
    kf = [k_ref[h] * (HEAD_DIM ** -0.5) for h in heads]
    qb = [x.astype(BF16) for x in qf]
    kb = [x.astype(BF16) for x in kf]
    vb = [v_ref[h].astype(BF16) for h in heads]
    ct = [ct_s[h] for h in heads]
    n_prev = [n_s[h] for h in heads]
    m_prev = [m_s[h][:, 0:1] for h in heads]

    i_col = [jnp.sum(jnp.where(lane == icol[h], pc, 0.0), axis=-1, keepdims=True) for h in heads]
    b_col = [jnp.sum(jnp.where(lane == fcol[h], cc, 0.0), axis=-1, keepdims=True) for h in heads]
    r_col = [i_col[h] - b_col[h] for h in heads]
    b_row = [cr_ref[fcol[h]:fcol[h] + 1, :] for h in heads]
    r_row = [pr_ref[icol[h]:icol[h] + 1, :] - b_row[h] for h in heads]

    r_mat = [jnp.where(causal, r_row[h], -jnp.inf) for h in heads]
    m_row = [jnp.maximum(jnp.max(r_mat[h], axis=-1, keepdims=True), m_prev[h]) for h in heads]
    scores = [_dot_nt(qb[h], kb[h]) for h in heads]
    inter = [_dot(qb[h], ct[h].astype(BF16)) for h in heads]
    p = [jnp.exp(r_mat[h] - m_row[h]) * scores[h] for h in heads]
    w_inter = [jnp.exp(m_prev[h] - m_row[h]) for h in heads]
    pv = [_dot(p[h].astype(BF16), vb[h]) for h in heads]
    nq = [w_inter[h] * jnp.sum(qf[h] * n_prev[h], axis=-1, keepdims=True)
          + jnp.sum(p[h], axis=-1, keepdims=True) for h in heads]
    den = [jnp.maximum(jnp.abs(nq[h]), jnp.exp(-(b_col[h] + m_row[h]))) for h in heads]
    hout = [(w_inter[h] * inter[h] + pv[h]) / den[h] for h in heads]

    g = [b_row[h][:, 0:1] if reverse else b_row[h][:, L - 1:L] for h in heads]
    r_max = [jnp.max(r_row[h], axis=-1, keepdims=True) for h in heads]
    m_loc = [g[h] + r_max[h] for h in heads]
    m_new = [jnp.maximum(g[h] + m_prev[h], m_loc[h]) for h in heads]
    s_old = [jnp.exp(g[h] + m_prev[h] - m_new[h]) for h in heads]
    s_new = [jnp.exp(m_loc[h] - m_new[h]) for h in heads]
    kw = [kf[h] * jnp.exp(r_col[h] - r_max[h]) for h in heads]
    ct_loc = [_dot_tn(kw[h].astype(BF16), vb[h]) for h in heads]
    for h in heads:
        ct_s[h] = s_old[h] * ct[h] + s_new[h] * ct_loc[h]
        n_s[h] = s_old[h] * n_prev[h] + s_new[h] * jnp.sum(kw[h], axis=0, keepdims=True)
        m_s[h] = jnp.broadcast_to(m_new[h], m_s.shape[1:])

    for h in heads:
        if final:
            hs = hf_ref[h] + hout[h]
            mu = jnp.mean(hs, axis=-1, keepdims=True)
            var = jnp.mean(jnp.square(hs - mu), axis=-1, keepdims=True)
            hn = (hs - mu) * lax.rsqrt(var + HEAD_NORM_EPS) * g_ref[h]
            out_ref[h] = (_sigmoid(o_ref[h]) * hn).astype(out_ref.dtype)
        else:
            out_ref[h] = hout[h]


def mlstm_direction(proj, gates, B, S, reverse, hf=None, norm_g=None):
    pc, cc, pr, cr = gates
    L = MLSTM_CHUNK
    nc = S // L
    T = B * S
    H = MLSTM_HEADS
    final = hf is not None

    def tok(b, c):
        return b * nc + (nc - 1 - c if reverse else c)

    def gspec(goff):
        return pl.BlockSpec((H, L, LANES), lambda b, c: (goff // H, tok(b, c), 0))

    in_specs = [gspec(G_MQ), gspec(G_MK), gspec(G_MV),
                pl.BlockSpec((L, LANES), lambda b, c: (tok(b, c), 0)),
                pl.BlockSpec((L, LANES), lambda b, c: (tok(b, c), 0)),
                pl.BlockSpec((LANES, L), lambda b, c: (0, tok(b, c))),
                pl.BlockSpec((LANES, L), lambda b, c: (0, tok(b, c)))]
    args = [proj, proj, proj, pc, cc, pr, cr]
    if final:
        in_specs += [gspec(0), gspec(G_MO), pl.BlockSpec((H, 1, LANES), lambda b, c: (0, 0, 0))]
        args += [hf, proj, norm_g]
    return pl.pallas_call(
        functools.partial(_mlstm_kernel, reverse=reverse, final=final),
        grid=(B, nc),
        in_specs=in_specs,
        out_specs=gspec(0),
        out_shape=jax.ShapeDtypeStruct((H, T, LANES), BF16 if final else F32),
        scratch_shapes=[pltpu.VMEM((H, HEAD_DIM, HEAD_DIM), F32), pltpu.VMEM((H, 1, HEAD_DIM), F32),
                        pltpu.VMEM((H, 1, LANES), F32)],
        name="mlstm_bwd" if reverse else "mlstm_fwd",
        compiler_params=_cparams(("arbitrary", "arbitrary")),
    )(*args)


def _gla_kernel(*refs, reverse, final):
    if final:
        (q_ref, k_ref, v_ref, pc_ref, wa_ref, ab_ref, of_ref, r_ref, g_ref, out_ref, st_s) = refs
    else:
        q_ref, k_ref, v_ref, pc_ref, wa_ref, ab_ref, out_ref, st_s = refs
    c = pl.program_id(1)
    NG, TS = q_ref.shape[0], q_ref.shape[1]
    L = GLA_CHUNK
    n_sub = TS // L

    @pl.when(c == 0)
    def _():
        st_s[...] = jnp.zeros(st_s.shape, F32)

    row = lax.broadcasted_iota(jnp.int32, (TS, TS), 0)
    col = lax.broadcasted_iota(jnp.int32, (TS, TS), 1)
    same_chunk = (row // L) == (col // L)
    causal = same_chunk & ((col >= row) if reverse else (col <= row))
    cum_mat = causal.astype(BF16)
    lane = lax.broadcasted_iota(jnp.int32, (TS, LANES), 1)
    head_mask = [lane < GLA_HEAD_K, lane >= GLA_HEAD_K]
    pcb = pc_ref[...].astype(BF16)
    order = list(range(n_sub - 1, -1, -1)) if reverse else list(range(n_sub))
    pairs = range(NG)
    heads = range(2 * NG)


    la = [_log_sigmoid(_dot(pcb, wa_ref[gi]) + ab_ref[gi]) * (1.0 / GLA_TAU) for gi in pairs]
    split = [_split3(x) for x in la]
    b = [_dot(cum_mat, hi) + _dot(cum_mat, mid) + _dot(cum_mat, lo) for hi, mid, lo in split]

    def per_chunk_rows(x, idx):
        return jnp.concatenate([jnp.broadcast_to(x[s * L + idx:s * L + idx + 1, :], (L, LANES))
                                for s in range(n_sub)], axis=0)

    g_full = [per_chunk_rows(x, 0 if reverse else L - 1) for x in b]
    b_mid = [per_chunk_rows(x, L // 2 - 1 if reverse else L // 2) for x in b]
    qf = [q_ref[gi] * (GLA_HEAD_K ** -0.5) for gi in pairs]
    kf = [k_ref[gi] for gi in pairs]
    qd = [qf[gi] * jnp.exp(b[gi] - b_mid[gi]) for gi in pairs]
    kd = [(kf[gi] * jnp.exp(b_mid[gi] - b[gi])).astype(BF16) for gi in pairs]
    kg = [kf[gi] * jnp.exp(g_full[gi] - b[gi]) for gi in pairs]
    qe = [qf[gi] * jnp.exp(b[gi]) for gi in pairs]
    decay = [[jnp.exp(b[gi][s * L:s * L + 1, :] if reverse else b[gi][s * L + L - 1:s * L + L, :])
              for s in range(n_sub)] for gi in pairs]

    vb = [v_ref[hd].astype(BF16) for hd in heads]
    qd_h = [jnp.where(head_mask[hd % 2], qd[hd // 2], 0.0).astype(BF16) for hd in heads]
    kg_h = [jnp.where(head_mask[hd % 2], kg[hd // 2], 0.0).astype(BF16) for hd in heads]
    qe_h = [jnp.where(head_mask[hd % 2], qe[hd // 2], 0.0).astype(BF16) for hd in heads]
    a = [_dot_nt(qd_h[hd], kd[hd // 2]) for hd in heads]
    a = [jnp.where(causal, x, 0.0).astype(BF16) for x in a]
    o_intra = [_dot(a[hd], vb[hd]) for hd in heads]

    st = [st_s[hd] for hd in heads]
    o_inter = [[None] * n_sub for _ in heads]
    for s in order:
        sl = slice(s * L, (s + 1) * L)
        for hd in heads:
            o_inter[hd][s] = _dot_nt(qe_h[hd][sl], st[hd].astype(BF16))
        st_loc = [_dot_tn(vb[hd][sl], kg_h[hd][sl]) for hd in heads]
        st = [st[hd] * decay[hd // 2][s] + st_loc[hd] for hd in heads]
    for hd in heads:
        st_s[hd] = st[hd]

    for hd in heads:
        o = o_intra[hd] + jnp.concatenate(o_inter[hd], axis=0)
        if final:
            ot = of_ref[hd] + o
            on = ot * lax.rsqrt(jnp.mean(jnp.square(ot), axis=-1, keepdims=True) + HEAD_NORM_EPS) * g_ref[hd]
            rr = r_ref[hd]
            out_ref[hd] = (on * (rr * _sigmoid(rr))).astype(out_ref.dtype)
        else:
            out_ref[hd] = o


def gla_direction(proj, pc, wa, a_bias, B, S, reverse, of=None, norm_g=None):
    TS = GLA_STEP
    nc = S // TS
    T = B * S
    NG = GLA_HEADS // 2
    final = of is not None
    d = 1 if reverse else 0

    def tok(b, c):
        return b * nc + (nc - 1 - c if reverse else c)

    def gspec(goff, n):
        return pl.BlockSpec((n, TS, LANES), lambda b, c: (goff // n, tok(b, c), 0))

    in_specs = [gspec(G_GQ, NG), gspec(G_GK, NG), gspec(G_GV, GLA_HEADS),
                pl.BlockSpec((TS, LANES), lambda b, c: (tok(b, c), 0)),
                pl.BlockSpec((None, NG, LANES, LANES), lambda b, c: (d, 0, 0, 0)),
                pl.BlockSpec((None, NG, 1, LANES), lambda b, c: (d, 0, 0, 0))]
    args = [proj, proj, proj, pc, wa, a_bias]
    if final:
        in_specs += [gspec(0, GLA_HEADS), gspec(G_GR, GLA_HEADS),
                     pl.BlockSpec((GLA_HEADS, 1, LANES), lambda b, c: (0, 0, 0))]
        args += [of, proj, norm_g]
    return pl.pallas_call(
        functools.partial(_gla_kernel, reverse=reverse, final=final),
        grid=(B, nc),
        in_specs=in_specs,
        out_specs=gspec(0, GLA_HEADS),
        out_shape=jax.ShapeDtypeStruct((GLA_HEADS, T, LANES), BF16 if final else F32),
        scratch_shapes=[pltpu.VMEM((GLA_HEADS, HEAD_DIM, LANES), F32)],
        name="gla_bwd" if reverse else "gla_fwd",
        compiler_params=_cparams(("arbitrary", "arbitrary")),
    )(*args)


def _rope(t, cos2, sin2):
    return t * cos2 + pltpu.roll(t, shift=HEAD_DIM // 2, axis=1) * sin2


def _attn_kernel(q_ref, kp_ref, km_ref, kn_ref, vp_ref, vm_ref, vn_ref,
                 cp_ref, cm_ref, cn_ref, sp_ref, sm_ref, sn_ref, out_ref,
                 q_s, k_s, v_s, q4_s, k4_s, v4_s, st0_s, st4_s, o_s, *, seq_len):
    i = pl.program_id(1)
    TQ, HALO, QB, KB = ATTN_TILE, ATTN_HALO, ATTN_QB, ATTN_KB
    R = ATTN_DILATIONS[1]
    t0 = i * TQ

    q_s[...] = _rope(q_ref[...], cm_ref[...], sm_ref[...]) * (HEAD_DIM ** -0.5)
    k_s[0:HALO, :] = _rope(kp_ref[...], cp_ref[...], sp_ref[...])
    k_s[HALO:HALO + TQ, :] = _rope(km_ref[...], cm_ref[...], sm_ref[...])
    k_s[HALO + TQ:, :] = _rope(kn_ref[...], cn_ref[...], sn_ref[...])
    v_s[0:HALO, :] = vp_ref[...]
    v_s[HALO:HALO + TQ, :] = vm_ref[...]
    v_s[HALO + TQ:, :] = vn_ref[...]
    for j in range(R):
        q4_s[j] = q_s[pl.ds(j, TQ // R, stride=R), :]
        k4_s[j] = k_s[pl.ds(j, (TQ + 2 * HALO) // R, stride=R), :]
        v4_s[j] = v_s[pl.ds(j, (TQ + 2 * HALO) // R, stride=R), :]

    qi = lax.broadcasted_iota(jnp.int32, (QB, KB), 0)
    ki = lax.broadcasted_iota(jnp.int32, (QB, KB), 1)
    band = (ki >= qi) & (ki <= qi + 2 * ATTN_REACH)
    kcol = lax.broadcasted_iota(jnp.int32, (1, KB), 1)

    def block_group(blocks):
        n = range(len(blocks))
        qb = [blocks[u][0]().astype(BF16) for u in n]
        kb = [blocks[u][1]().astype(BF16) for u in n]
        vb = [blocks[u][2]().astype(BF16) for u in n]
        s = [_dot_nt(qb[u], kb[u]) for u in n]
        kpos = [blocks[u][3] for u in n]
        s = [jnp.where(band & (kpos[u] >= 0) & (kpos[u] < seq_len), s[u], -jnp.inf) for u in n]
        m = [jnp.max(s[u], axis=-1, keepdims=True) for u in n]
        e = [jnp.exp(s[u] - m[u]) for u in n]
        acc = [_dot(e[u].astype(BF16), vb[u]) for u in n]
        for u in n:
            store = blocks[u][4]
            store(0, acc[u])
            store(1, jnp.broadcast_to(m[u], (QB, LANES)))
            store(2, jnp.broadcast_to(jnp.sum(e[u], axis=-1, keepdims=True), (QB, LANES)))

    def dense_block(it):
        q0 = it * QB
        k0 = HALO + q0 - ATTN_REACH

        def store(which, val):
            st0_s[which, pl.ds(q0, QB), :] = val
        return (lambda: q_s[pl.ds(q0, QB), :], lambda: k_s[pl.ds(k0, KB), :], lambda: v_s[pl.ds(k0, KB), :],
                t0 - HALO + k0 + kcol, store)

    def mod4_block(it):
        j = it % R
        a0 = (it // R) * QB
        k0 = HALO // R + a0 - ATTN_REACH

        def store(which, val):
            st4_s[0, which, j, pl.ds(a0, QB), :] = val
        return (lambda: q4_s[j, pl.ds(a0, QB), :], lambda: k4_s[j, pl.ds(k0, KB), :],
                lambda: v4_s[j, pl.ds(k0, KB), :], t0 - HALO + R * (k0 + kcol) + j, store)

    def mod16_block(r):
        j = r % R
        m0 = r // R

        def store(which, val):
            st4_s[1, which, j, pl.ds(m0, QB, stride=R), :] = val
        return (lambda: q4_s[j, pl.ds(m0, QB, stride=R), :], lambda: k4_s[j, pl.ds(m0, KB, stride=R), :],
                lambda: v4_s[j, pl.ds(m0, KB, stride=R), :], t0 - HALO + r + ATTN_DILATIONS[2] * kcol, store)

    for make_block in (dense_block, mod4_block, mod16_block):
        def body(g, carry, make_block=make_block):
            block_group([make_block(g * ATTN_UNROLL + u) for u in range(ATTN_UNROLL)])
            return carry

        lax.fori_loop(0, (TQ // QB) // ATTN_UNROLL, body, 0)

    for j in range(R):
        rows = pl.ds(j, TQ // R, stride=R)
        parts = [(st0_s[0, rows, :], st0_s[1, rows, :], st0_s[2, rows, :]),
                 (st4_s[0, 0, j], st4_s[0, 1, j], st4_s[0, 2, j]),
                 (st4_s[1, 0, j], st4_s[1, 1, j], st4_s[1, 2, j])]
        m_all = jnp.maximum(jnp.maximum(parts[0][1], parts[1][1]), parts[2][1])
        num = jnp.zeros((TQ // R, LANES), F32)
        den = jnp.zeros((TQ // R, LANES), F32)
        for acc, m, l in parts:
            w = jnp.exp(m - m_all)
            num = num + w * acc
            den = den + w * l
        o_s[rows, :] = num / den
    out_ref[...] = o_s[...].astype(out_ref.dtype)


def dilated_attention(proj, cos2, sin2, B, S):
    TQ, HALO = ATTN_TILE, ATTN_HALO
    nt = S // TQ
    T = B * S
    hpt = TQ // HALO
    n_halo = T // HALO
    KT = TQ + 2 * HALO
    R = ATTN_DILATIONS[1]

    def main(goff):
        return pl.BlockSpec((None, TQ, LANES), lambda b, i, h: (goff + h, b * nt + i, 0))

    def prev(goff):
        return pl.BlockSpec((None, HALO, LANES),
                            lambda b, i, h: (goff + h, jnp.maximum((b * nt + i) * hpt - 1, 0), 0))

    def nxt(goff):
        return pl.BlockSpec((None, HALO, LANES),
                            lambda b, i, h: (goff + h, jnp.minimum((b * nt + i + 1) * hpt, n_halo - 1), 0))

    t_prev = pl.BlockSpec((HALO, LANES), lambda b, i, h: (jnp.maximum(i * hpt - 1, 0), 0))
    t_main = pl.BlockSpec((TQ, LANES), lambda b, i, h: (i, 0))
    t_next = pl.BlockSpec((HALO, LANES), lambda b, i, h: (jnp.minimum((i + 1) * hpt, S // HALO - 1), 0))

    return pl.pallas_call(
        functools.partial(_attn_kernel, seq_len=S),
        grid=(B, nt, ATTN_HEADS),
        in_specs=[main(G_AQ), prev(G_AK), main(G_AK), nxt(G_AK), prev(G_AV), main(G_AV), nxt(G_AV),
                  t_prev, t_main, t_next, t_prev, t_main, t_next],
        out_specs=main(0),
        out_shape=jax.ShapeDtypeStruct((ATTN_HEADS, T, LANES), BF16),
        scratch_shapes=[pltpu.VMEM((TQ, LANES), F32), pltpu.VMEM((KT, LANES), F32), pltpu.VMEM((KT, LANES), F32),
                        pltpu.VMEM((R, TQ // R, LANES), F32), pltpu.VMEM((R, KT // R, LANES), F32),
                        pltpu.VMEM((R, KT // R, LANES), F32), pltpu.VMEM((3, TQ, LANES), F32),
                        pltpu.VMEM((2, 3, R, TQ // R, LANES), F32), pltpu.VMEM((TQ, LANES), F32)],
        name="dilated_attention",
        compiler_params=_cparams(("parallel", "parallel", "arbitrary")),
    )(proj, proj, proj, proj, proj, proj, proj, cos2, cos2, cos2, sin2, sin2, sin2)


def _ln_kernel(x_ref, y_ref, g_ref, b_ref, o_ref, ob_ref):
    z = DEEPNORM_ALPHA * x_ref[...] + y_ref[...]
    mu = jnp.mean(z, axis=-1, keepdims=True)
    zc = z - mu
    var = jnp.mean(jnp.square(zc), axis=-1, keepdims=True)
    out = zc * lax.rsqrt(var + LN_EPS) * g_ref[...] + b_ref[...]
    o_ref[...] = out
    ob_ref[...] = out.astype(BF16)


def residual_layer_norm(x, y, g, b, tr=256):
    T, D = x.shape
    row = pl.BlockSpec((tr, D), lambda i: (i, 0))
    vec = pl.BlockSpec((1, D), lambda i: (0, 0))
    return pl.pallas_call(
        _ln_kernel,
        grid=(T // tr,),
        in_specs=[row, row, vec, vec],
        out_specs=[row, row],
        out_shape=[jax.ShapeDtypeStruct((T, D), F32), jax.ShapeDtypeStruct((T, D), BF16)],
        name="residual_layer_norm",
        compiler_params=_cparams(("parallel",)),
    )(x, y, g, b)


FFN_HALO = 16

def _up_conv_gate_kernel(xp_ref, xm_ref, xn_ref, wg_ref, wv_ref, cwg_ref, cwv_ref, cbg_ref, cbv_ref, o_ref,
                         xs_ref, *, n_tiles):
    i = pl.program_id(1)
    j = pl.program_id(2)
    tm = xm_ref.shape[0]
    hr = xp_ref.shape[0]

    @pl.when(j == 0)
    def _():
        xs_ref[0:hr, :] = xp_ref[...] * (i > 0).astype(BF16)
        xs_ref[hr:hr + tm, :] = xm_ref[...]
        xs_ref[hr + tm:, :] = xn_ref[...] * (i < n_tiles - 1).astype(BF16)

    xs = xs_ref[...]

    def conv(w_ref, cw_ref, cb_ref):
        u = _dot(xs, w_ref[...])
        return (cb_ref[...] + u[hr - 1:hr - 1 + tm] * cw_ref[0:1, :] + u[hr:hr + tm] * cw_ref[1:2, :]
                + u[hr + 1:hr + 1 + tm] * cw_ref[2:3, :])

    gate = conv(wg_ref, cwg_ref, cbg_ref)
    val = conv(wv_ref, cwv_ref, cbv_ref)
    o_ref[...] = (gate * _sigmoid(gate) * val).astype(o_ref.dtype)


def up_conv_gate(xb, w_up, layer, conv_w, conv_b, B, S, tm=1024, tf=256):
    T, K = xb.shape
    nt = S // tm
    nf = D_FF // tf
    hb = tm // FFN_HALO
    n_hb = T // FFN_HALO

    def wspec(off, r):
        return pl.BlockSpec((r, tf), lambda b, i, j: (0, off + j))

    def upspec(off):
        return pl.BlockSpec((None, K, tf), lambda b, i, j: (layer, 0, off + j))

    return pl.pallas_call(
        functools.partial(_up_conv_gate_kernel, n_tiles=nt),
        grid=(B, nt, nf),
        in_specs=[pl.BlockSpec((FFN_HALO, K), lambda b, i, j: (jnp.maximum((b * nt + i) * hb - 1, 0), 0)),
                  pl.BlockSpec((tm, K), lambda b, i, j: (b * nt + i, 0)),
                  pl.BlockSpec((FFN_HALO, K), lambda b, i, j: (jnp.minimum((b * nt + i + 1) * hb, n_hb - 1), 0)),
                  upspec(0), upspec(nf), wspec(0, 3), wspec(nf, 3), wspec(0, 1), wspec(nf, 1)],
        out_specs=pl.BlockSpec((tm, tf), lambda b, i, j: (b * nt + i, j)),
        out_shape=jax.ShapeDtypeStruct((T, D_FF), BF16),
        scratch_shapes=[pltpu.VMEM((tm + 2 * FFN_HALO, K), BF16)],
        name="up_conv_gate",
        compiler_params=_cparams(("parallel", "parallel", "arbitrary")),
    )(xb, xb, xb, w_up, w_up, conv_w, conv_w, conv_b, conv_b)


def _ln_ple_kernel(z_ref, g_ref, b_ref, wg_ref, p_ref, wp_ref, o_ref, ob_ref, x_s, xb_s):
    j = pl.program_id(1)
    tn = wg_ref.shape[1]

    @pl.when(j == 0)
    def _():
        z = z_ref[...]
        mu = jnp.mean(z, axis=-1, keepdims=True)
        zc = z - mu
        var = jnp.mean(jnp.square(zc), axis=-1, keepdims=True)
        x = zc * lax.rsqrt(var + LN_EPS) * g_ref[...] + b_ref[...]
        xb_s[...] = x.astype(BF16)
        for jj in range(x_s.shape[0]):
            x_s[jj] = x[:, jj * tn:(jj + 1) * tn]

    gate = _sigmoid(_dot(xb_s[...], wg_ref[...]))
    emb = _dot(p_ref[...].astype(BF16), wp_ref[...])
    out = x_s[j] + gate * emb
    o_ref[...] = out
    ob_ref[...] = out.astype(BF16)


def ln_ple_update(z, g, b, p, w_gate, w_ple, layer, tm=512, tn=512):
    T, D = z.shape
    vec = pl.BlockSpec((1, D), lambda i, j: (0, 0))
    return pl.pallas_call(
        _ln_ple_kernel,
        grid=(T // tm, D // tn),
        in_specs=[pl.BlockSpec((tm, D), lambda i, j: (i, 0)), vec, vec,
                  pl.BlockSpec((None, D, tn), lambda i, j: (layer, 0, j)),
                  pl.BlockSpec((tm, PLE_DIM), lambda i, j: (i, 0)),
                  pl.BlockSpec((None, PLE_DIM, tn), lambda i, j: (layer, 0, j))],
        out_specs=[pl.BlockSpec((tm, tn), lambda i, j: (i, j)), pl.BlockSpec((tm, tn), lambda i, j: (i, j))],
        out_shape=[jax.ShapeDtypeStruct((T, D), F32), jax.ShapeDtypeStruct((T, D), BF16)],
        scratch_shapes=[pltpu.VMEM((D // tn, tm, tn), F32), pltpu.VMEM((tm, D), BF16)],
        name="ln_ple_update",
        compiler_params=_cparams(("parallel", "arbitrary")),
    )(z, g, b, w_gate, p, w_ple)


IN_PROJ_ROW_UNIT = 4 * MLSTM_HEADS


def _in_proj_segments():
    mw = MLSTM_HEADS * HEAD_DIM
    aw = ATTN_HEADS * HEAD_DIM
    gkw = GLA_HEADS * GLA_HEAD_K
    gvw = GLA_HEADS * HEAD_DIM
    o_mq, o_mk, o_mv, o_mo = 0, mw, 2 * mw, 3 * mw
    o_gate = 4 * mw
    o_aq = o_gate + 4 * MLSTM_HEADS
    o_ak, o_av = o_aq + aw, o_aq + 2 * aw
    o_gq = o_aq + 3 * aw
    o_gk = o_gq + gkw
    o_gv = o_gk + gkw
    o_gr = o_gv + gvw
    o_alpha = o_gr + gvw
    big = [(o_gv, gvw), (o_gr, gvw), (o_gq, gkw), (o_gk, gkw), (o_aq, aw), (o_mq, mw), (o_mk, mw),
           (o_mv, mw), (o_mo, mw), (o_ak, aw), (o_av, aw)]
    return big, (o_gate, 4 * MLSTM_HEADS), (o_alpha, 2 * GLA_RANK)


def _regroup_rows_kernel(tbl_ref, *refs):
    *in_refs, o_ref = refs
    unit = in_refs[0].shape[0]
    for u, r in enumerate(in_refs):
        o_ref[u * unit:(u + 1) * unit, :] = r[...].astype(o_ref.dtype)


def regroup_input_weights(w_in_t, layer):
    K = w_in_t.shape[2]
    unit = IN_PROJ_ROW_UNIT
    big, _, _ = _in_proj_segments()
    tbl = []
    for off, width in big:
        assert off % unit == 0 and width % LANES == 0
        tbl += [(off + g * LANES) // unit for g in range(width // LANES)]
    assert len(tbl) == N_PROJ_GROUPS
    per_group = LANES // unit
    grid_spec = pltpu.PrefetchScalarGridSpec(
        num_scalar_prefetch=1,
        grid=(len(tbl),),
        in_specs=[pl.BlockSpec((None, unit, K), lambda c, tbl, u=u: (layer, tbl[c] + u, 0))
                  for u in range(per_group)],
        out_specs=pl.BlockSpec((LANES, K), lambda c, tbl: (c, 0)))
    return pl.pallas_call(
        _regroup_rows_kernel,
        grid_spec=grid_spec,
        out_shape=jax.ShapeDtypeStruct((len(tbl) * LANES, K), BF16),
        name="regroup_input_weights",
        compiler_params=_cparams(("parallel",)),
    )(jnp.asarray(tbl, jnp.int32), *([w_in_t] * per_group))


def _small_input_weights(w_in_t, layer):
    _, (o_gate, n_gate), (o_alpha, n_alpha) = _in_proj_segments()
    rows = jnp.concatenate([w_in_t[layer, o_gate:o_gate + n_gate], w_in_t[layer, o_alpha:o_alpha + n_alpha]], axis=0)
    return jnp.pad(rows, ((0, LANES - n_gate - n_alpha), (0, 0))).astype(BF16)


def _prepare_layer_params(layer, w_in_t, big, mlstm_gate_b, mlstm_norm_g, gla_w_a2, gla_a_b, gla_norm_g,
                          ln1_g, ln1_b, conv_w, conv_b, ln2_g, ln2_b):
    n_gate = 4 * MLSTM_HEADS
    w_big_t = regroup_input_weights(w_in_t, layer)
    w_small_t = _small_input_weights(w_in_t, layer)
    bias = jnp.pad(mlstm_gate_b.reshape(-1), (0, LANES - n_gate)).astype(F32)
    wa = jnp.zeros((2, GLA_HEADS // 2, LANES, LANES), F32)
    a2 = gla_w_a2.reshape(2, GLA_RANK, GLA_HEADS // 2, LANES).transpose(0, 2, 1, 3)
    wa = wa.at[0, :, GATE_A_F:GATE_A_F + GLA_RANK, :].set(a2[0])
    wa = wa.at[1, :, GATE_A_B:GATE_A_B + GLA_RANK, :].set(a2[1])
    return dict(
        w_big_t=w_big_t, w_small=w_small_t.T, w_small_t=w_small_t,
        bias_c=bias.reshape(1, LANES), bias_r=bias.reshape(LANES, 1),
        mlstm_norm_g=mlstm_norm_g.reshape(MLSTM_HEADS, 1, HEAD_DIM),
        wa=wa.astype(BF16), a_bias=gla_a_b.reshape(2, GLA_HEADS // 2, 1, LANES),
        gla_norm_g=gla_norm_g.reshape(GLA_HEADS, 1, HEAD_DIM),
        ln1_g=ln1_g.reshape(1, -1), ln1_b=ln1_b.reshape(1, -1), conv_w=conv_w, conv_b=conv_b.reshape(1, -1),
        ln2_g=ln2_g.reshape(1, -1), ln2_b=ln2_b.reshape(1, -1), layer=layer, **big)


def _rope_tables(S):
    half = HEAD_DIM // 2
    inv = ROPE_THETA ** (-jnp.arange(half, dtype=F32) / half)
    ang = jnp.arange(S, dtype=F32)[:, None] * inv[None, :]
    cos, sin = jnp.cos(ang), jnp.sin(ang)
    return jnp.concatenate([cos, cos], axis=-1), jnp.concatenate([-sin, sin], axis=-1)


def _encoder_layer(x, xb, p_i, prm, tables, B, S):
    proj = proj_matmul(xb, prm['w_big_t'])
    gates = gates_project(xb, prm['w_small'], prm['w_small_t'], prm['bias_c'], prm['bias_r'], MLSTM_CHUNK)
    hf = mlstm_direction(proj, gates, B, S, reverse=False)
    y_m = mlstm_direction(proj, gates, B, S, reverse=True, hf=hf, norm_g=prm['mlstm_norm_g'])
    y_a = dilated_attention(proj, tables[0], tables[1], B, S)
    of = gla_direction(proj, gates[0], prm['wa'], prm['a_bias'], B, S, reverse=False)
    y_g = gla_direction(proj, gates[0], prm['wa'], prm['a_bias'], B, S, reverse=True, of=of,
                        norm_g=prm['gla_norm_g'])
    layer = prm['layer']
    x, xb = mix_out_layer_norm(y_m, y_a, y_g, prm['w_out'], layer, x, prm['ln1_g'], prm['ln1_b'])
    hmid = up_conv_gate(xb, prm['w_up'], layer, prm['conv_w'], prm['conv_b'], B, S)
    z = matmul_residual(hmid, prm['w_down'], layer, x, tm=512, tn=512)
    return ln_ple_update(z, prm['ln2_g'], prm['ln2_b'], p_i, prm['w_ple_gate'], prm['w_ple'], layer)


def _trunk(x, p, layer_params):
    B, S, D = x.shape
    tables = _rope_tables(S)
    x = x.reshape(B * S, D)
    xb = x.astype(BF16)
    for i, prm in enumerate(layer_params):
        x, xb = _encoder_layer(x, xb, p[i].reshape(B * S, -1), prm, tables, B, S)
    return x.reshape(B, S, D)


def kernel(x_prompt, x_sample, p_prompt, p_sample, w_in, mlstm_gate_b, mlstm_norm_g, gla_w_a2, gla_a_b,
           gla_norm_g, w_out, ln1_g, ln1_b, w_up, conv_w, conv_b, w_down, ln2_g, ln2_b, w_ple, w_ple_gate):
    big = dict(w_out=w_out.astype(BF16), w_up=w_up.astype(BF16), w_down=w_down.astype(BF16),
               w_ple=w_ple.astype(BF16), w_ple_gate=w_ple_gate.astype(BF16))
    small = (mlstm_gate_b, mlstm_norm_g, gla_w_a2, gla_a_b, gla_norm_g, ln1_g, ln1_b, conv_w, conv_b, ln2_g, ln2_b)
    w_in_t = jnp.swapaxes(w_in, 1, 2)
    layer_params = [_prepare_layer_params(i, w_in_t, big, *(w[i] for w in small)) for i in range(w_in.shape[0])]
    return (_trunk(x_prompt, p_prompt, layer_params), _trunk(x_sample, p_sample, layer_params))
```

```python
import functools

import jax
import jax.numpy as jnp
from jax import lax
from jax.experimental import pallas as pl
from jax.experimental.pallas import tpu as pltpu

F32 = jnp.float32
BF16 = jnp.bfloat16

D_MODEL = 4096
HEAD_DIM = 128
MLSTM_HEADS = 8
ATTN_HEADS = 12
GLA_HEADS = 12
GLA_HEAD_K = 64
GLA_RANK = 16
GLA_TAU = 16.0
GLA_CHUNK = 64
ATTN_REACH = 64
ATTN_DILATIONS = (1, 4, 16)
ROPE_THETA = 10000.0
D_FF = 11008
PLE_DIM = 256
LN_EPS = 1e-5
HEAD_NORM_EPS = 1e-6
DEPTH = 2
DEEPNORM_ALPHA = (2 * DEPTH) ** 0.25

LANES = 128
VMEM_LIMIT = 56 * 1024 * 1024

G_GV, G_GR, G_GQ, G_GK = 0, 12, 24, 30
G_AQ = 36
G_MQ, G_MK, G_MV, G_MO = 48, 56, 64, 72
G_AK, G_AV = 80, 92
N_PROJ_GROUPS = 104
GATE_I_F, GATE_F_F, GATE_I_B, GATE_F_B, GATE_A_F, GATE_A_B = 0, 8, 16, 24, 32, 48

MLSTM_CHUNK = 256
GLA_STEP = 256
ATTN_TILE = 2048
ATTN_HALO = ATTN_REACH * ATTN_DILATIONS[-1]
ATTN_QB = 128
ATTN_KB = ATTN_QB + 2 * ATTN_REACH
ATTN_UNROLL = 8


def _cparams(sem):
    return pltpu.CompilerParams(dimension_semantics=sem, vmem_limit_bytes=VMEM_LIMIT)


def _log_sigmoid(x):
    return jnp.minimum(x, 0.0) - jnp.log1p(jnp.exp(-jnp.abs(x)))


def _sigmoid(x):
    return 1.0 / (1.0 + jnp.exp(-x))


def _dot(a, b):
    return jnp.dot(a, b, preferred_element_type=F32)


def _dot_nt(a, b):
    return lax.dot_general(a, b, (((1,), (1,)), ((), ())), preferred_element_type=F32)


def _dot_tn(a, b):
    return lax.dot_general(a, b, (((0,), (0,)), ((), ())), preferred_element_type=F32)


def _split3(x):
    hi = x.astype(BF16)
    r1 = x - hi.astype(F32)
    mid = r1.astype(BF16)
    lo = (r1 - mid.astype(F32)).astype(BF16)
    return hi, mid, lo


def _tri_dot(tri, x):
    hi, mid, lo = _split3(x)
    return _dot(tri, hi) + _dot(tri, mid) + _dot(tri, lo)


def _dot_tri(x, tri):
    hi, mid, lo = _split3(x)
    return _dot(hi, tri) + _dot(mid, tri) + _dot(lo, tri)


def _proj_kernel(x_ref, wt_ref, o_ref):
    acc = _dot_nt(x_ref[...], wt_ref[...])
    for g in range(o_ref.shape[0]):
        o_ref[g] = acc[:, g * LANES:(g + 1) * LANES].astype(o_ref.dtype)


def proj_matmul(xb, wt, tm=1024, tn=1024):
    T, K = xb.shape
    N = wt.shape[0]
    return pl.pallas_call(
        _proj_kernel,
        grid=(T // tm, N // tn),
        in_specs=[pl.BlockSpec((tm, K), lambda i, j: (i, 0)),
                  pl.BlockSpec((tn, K), lambda i, j: (j, 0))],
        out_specs=pl.BlockSpec((tn // LANES, tm, LANES), lambda i, j: (j, i, 0)),
        out_shape=jax.ShapeDtypeStruct((N // LANES, T, LANES), F32),
        name="proj_matmul",
        compiler_params=_cparams(("parallel", "parallel")),
    )(xb, wt)


def _mm_residual_kernel(x_ref, w_ref, r_ref, o_ref):
    o_ref[...] = DEEPNORM_ALPHA * r_ref[...] + _dot(x_ref[...], w_ref[...])


def matmul_residual(xb, w, layer, res, tm, tn):
    T, K = xb.shape
    N = w.shape[2]
    return pl.pallas_call(
        _mm_residual_kernel,
        grid=(T // tm, N // tn),
        in_specs=[pl.BlockSpec((tm, K), lambda i, j: (i, 0)),
                  pl.BlockSpec((None, K, tn), lambda i, j: (layer, 0, j)),
                  pl.BlockSpec((tm, tn), lambda i, j: (i, j))],
        out_specs=pl.BlockSpec((tm, tn), lambda i, j: (i, j)),
        out_shape=jax.ShapeDtypeStruct((T, N), F32),
        name="matmul_residual",
        compiler_params=_cparams(("parallel", "parallel")),
    )(xb, w, res)


def _mix_out_ln_kernel(ym_ref, ya_ref, yg_ref, w_ref, x_ref, g_ref, b_ref, o_ref, ob_ref):
    j = pl.program_id(1)
    tn = w_ref.shape[1]
    d = o_ref.shape[1]
    n_slabs = d // tn
    parts = ([ym_ref[g] for g in range(ym_ref.shape[0])] + [ya_ref[g] for g in range(ya_ref.shape[0])]
             + [yg_ref[g] for g in range(yg_ref.shape[0])])
    y = jnp.concatenate(parts, axis=-1)
    o_ref[:, pl.ds(pl.multiple_of(j * tn, tn), tn)] = DEEPNORM_ALPHA * x_ref[...] + _dot(y, w_ref[...])

    @pl.when(j == n_slabs - 1)
    def _():
        slabs = [slice(s * tn, (s + 1) * tn) for s in range(n_slabs)]
        mu = sum(jnp.sum(o_ref[:, c], axis=-1, keepdims=True) for c in slabs) * (1.0 / d)
        var = sum(jnp.sum(jnp.square(o_ref[:, c] - mu), axis=-1, keepdims=True) for c in slabs) * (1.0 / d)
        inv = lax.rsqrt(var + LN_EPS)
        for c in slabs:
            out = (o_ref[:, c] - mu) * inv * g_ref[:, c] + b_ref[:, c]
            o_ref[:, c] = out
            ob_ref[:, c] = out.astype(BF16)


def mix_out_layer_norm(ym, ya, yg, w, layer, x, g, b, tm=512, tn=512):
    T = ym.shape[1]
    K, N = w.shape[1], w.shape[2]

    def gspec(a):
        return pl.BlockSpec((a.shape[0], tm, LANES), lambda i, j: (0, i, 0))

    vec = pl.BlockSpec((1, N), lambda i, j: (0, 0))
    row = pl.BlockSpec((tm, N), lambda i, j: (i, 0))
    return pl.pallas_call(
        _mix_out_ln_kernel,
        grid=(T // tm, N // tn),
        in_specs=[gspec(ym), gspec(ya), gspec(yg), pl.BlockSpec((None, K, tn), lambda i, j: (layer, 0, j)),
                  pl.BlockSpec((tm, tn), lambda i, j: (i, j)), vec, vec],
        out_specs=[row, row],
        out_shape=[jax.ShapeDtypeStruct((T, N), F32), jax.ShapeDtypeStruct((T, N), BF16)],
        name="mix_out_layer_norm",
        compiler_params=_cparams(("parallel", "arbitrary")),
    )(ym, ya, yg, w, x, g, b)


def _gates_kernel(x_ref, w_ref, wt_ref, bc_ref, br_ref, pc_ref, cc_ref, pr_ref, cr_ref, *, chunk):
    x = x_ref[...]
    tm = x.shape[0]
    pc = _dot(x, w_ref[...]) + bc_ref[...]
    pr = _dot_nt(wt_ref[...], x) + br_ref[...]
    pc_ref[...] = pc
    pr_ref[...] = pr
    ls_c = _log_sigmoid(pc)
    ls_r = _log_sigmoid(pr)
    row = lax.broadcasted_iota(jnp.int32, (chunk, chunk), 0)
    col = lax.broadcasted_iota(jnp.int32, (chunk, chunk), 1)
    lower = (row >= col).astype(BF16)
    upper = (row <= col).astype(BF16)
    lane = lax.broadcasted_iota(jnp.int32, (chunk, LANES), 1)
    sub = lax.broadcasted_iota(jnp.int32, (LANES, chunk), 0)
    bwd_c = (lane >= GATE_F_B) & (lane < GATE_F_B + MLSTM_HEADS)
    bwd_r = (sub >= GATE_F_B) & (sub < GATE_F_B + MLSTM_HEADS)
    for c in range(tm // chunk):
        sl = slice(c * chunk, (c + 1) * chunk)
        lc = ls_c[sl]
        cc_ref[sl, :] = jnp.where(bwd_c, _tri_dot(upper, lc), _tri_dot(lower, lc))
        lr = ls_r[:, sl]
        cr_ref[:, sl] = jnp.where(bwd_r, _dot_tri(lr, lower), _dot_tri(lr, upper))


def gates_project(xb, w_small, w_small_t, bias_c, bias_r, chunk, tm=1024):
    T, K = xb.shape
    return pl.pallas_call(
        functools.partial(_gates_kernel, chunk=chunk),
        grid=(T // tm,),
        in_specs=[pl.BlockSpec((tm, K), lambda i: (i, 0)),
                  pl.BlockSpec((K, LANES), lambda i: (0, 0)),
                  pl.BlockSpec((LANES, K), lambda i: (0, 0)),
                  pl.BlockSpec((1, LANES), lambda i: (0, 0)),
                  pl.BlockSpec((LANES, 1), lambda i: (0, 0))],
        out_specs=[pl.BlockSpec((tm, LANES), lambda i: (i, 0)),
                   pl.BlockSpec((tm, LANES), lambda i: (i, 0)),
                   pl.BlockSpec((LANES, tm), lambda i: (0, i)),
                   pl.BlockSpec((LANES, tm), lambda i: (0, i))],
        out_shape=[jax.ShapeDtypeStruct((T, LANES), F32), jax.ShapeDtypeStruct((T, LANES), F32),
                   jax.ShapeDtypeStruct((LANES, T), F32), jax.ShapeDtypeStruct((LANES, T), F32)],
        name="gates_project",
        compiler_params=_cparams(("parallel",)),
    )(xb, w_small, w_small_t, bias_c, bias_r)


def _mlstm_kernel(*refs, reverse, final):
    if final:
        (q_ref, k_ref, v_ref, pc_ref, cc_ref, pr_ref, cr_ref, hf_ref, o_ref, g_ref,
         out_ref, ct_s, n_s, m_s) = refs
    else:
        q_ref, k_ref, v_ref, pc_ref, cc_ref, pr_ref, cr_ref, out_ref, ct_s, n_s, m_s = refs
    c = pl.program_id(1)
    H, L = q_ref.shape[0], q_ref.shape[1]

    @pl.when(c == 0)
    def _():
        ct_s[...] = jnp.zeros(ct_s.shape, F32)
        n_s[...] = jnp.zeros(n_s.shape, F32)
        m_s[...] = jnp.zeros(m_s.shape, F32)

    pc = pc_ref[...]
    cc = cc_ref[...]
    lane = lax.broadcasted_iota(jnp.int32, (L, LANES), 1)
    t_idx = lax.broadcasted_iota(jnp.int32, (L, L), 0)
    s_idx = lax.broadcasted_iota(jnp.int32, (L, L), 1)
    causal = (s_idx >= t_idx) if reverse else (s_idx <= t_idx)

    heads = range(H)
    icol = [(GATE_I_B if reverse else GATE_I_F) + h for h in heads]
    fcol = [(GATE_F_B if reverse else GATE_F_F) + h for h in heads]
    qf = [q_ref[h] for h in heads]
    kf = [k_ref[h] * (HEAD_DIM ** -0.5) for h in heads]
    qb = [x.astype(BF16) for x in qf]
    kb = [x.astype(BF16) for x in kf]
    vb = [v_ref[h].astype(BF16) for h in heads]
    ct = [ct_s[h] for h in heads]
    n_prev = [n_s[h] for h in heads]
    m_prev = [m_s[h][:, 0:1] for h in heads]

    i_col = [jnp.sum(jnp.where(lane == icol[h], pc, 0.0), axis=-1, keepdims=True) for h in heads]
    b_col = [jnp.sum(jnp.where(lane == fcol[h], cc, 0.0), axis=-1, keepdims=True) for h in heads]
    r_col = [i_col[h] - b_col[h] for h in heads]
    b_row = [cr_ref[fcol[h]:fcol[h] + 1, :] for h in heads]
    r_row = [pr_ref[icol[h]:icol[h] + 1, :] - b_row[h] for h in heads]

    r_mat = [jnp.where(causal, r_row[h], -jnp.inf) for h in heads]
    m_row = [jnp.maximum(jnp.max(r_mat[h], axis=-1, keepdims=True), m_prev[h]) for h in heads]
    scores = [_dot_nt(qb[h], kb[h]) for h in heads]
    inter = [_dot(qb[h], ct[h].astype(BF16)) for h in heads]
    p = [jnp.exp(r_mat[h] - m_row[h]) * scores[h] for h in heads]
    w_inter = [jnp.exp(m_prev[h] - m_row[h]) for h in heads]
    pv = [_dot(p[h].astype(BF16), vb[h]) for h in heads]
    nq = [w_inter[h] * jnp.sum(qf[h] * n_prev[h], axis=-1, keepdims=True)
          + jnp.sum(p[h], axis=-1, keepdims=True) for h in heads]
    den = [jnp.maximum(jnp.abs(nq[h]), jnp.exp(-(b_col[h] + m_row[h]))) for h in heads]
    hout = [(w_inter[h] * inter[h] + pv[h]) / den[h] for h in heads]

    g = [b_row[h][:, 0:1] if reverse else b_row[h][:, L - 1:L] for h in heads]
    r_max = [jnp.max(r_row[h], axis=-1, keepdims=True) for h in heads]
    m_loc = [g[h] + r_max[h] for h in heads]
    m_new = [jnp.maximum(g[h] + m_prev[h], m_loc[h]) for h in heads]
    s_old = [jnp.exp(g[h] + m_prev[h] - m_new[h]) for h in heads]
    s_new = [jnp.exp(m_loc[h] - m_new[h]) for h in heads]
    kw = [kf[h] * jnp.exp(r_col[h] - r_max[h]) for h in heads]
    ct_loc = [_dot_tn(kw[h].astype(BF16), vb[h]) for h in heads]
    for h in heads:
        ct_s[h] = s_old[h] * ct[h] + s_new[h] * ct_loc[h]
        n_s[h] = s_old[h] * n_prev[h] + s_new[h] * jnp.sum(kw[h], axis=0, keepdims=True)
        m_s[h] = jnp.broadcast_to(m_new[h], m_s.shape[1:])

    for h in heads:
        if final:
            hs = hf_ref[h] + hout[h]
            mu = jnp.mean(hs, axis=-1, keepdims=True)
            var = jnp.mean(jnp.square(hs - mu), axis=-1, keepdims=True)
            hn = (hs - mu) * lax.rsqrt(var + HEAD_NORM_EPS) * g_ref[h]
            out_ref[h] = (_sigmoid(o_ref[h]) * hn).astype(out_ref.dtype)
        else:
            out_ref[h] = hout[h]


def mlstm_direction(proj, gates, B, S, reverse, hf=None, norm_g=None):
    pc, cc, pr, cr = gates
    L = MLSTM_CHUNK
    nc = S // L
    T = B * S
    H = MLSTM_HEADS
    final = hf is not None

    def tok(b, c):
        return b * nc + (nc - 1 - c if reverse else c)

    def gspec(goff):
        return pl.BlockSpec((H, L, LANES), lambda b, c: (goff // H, tok(b, c), 0))

    in_specs = [gspec(G_MQ), gspec(G_MK), gspec(G_MV),
                pl.BlockSpec((L, LANES), lambda b, c: (tok(b, c), 0)),
                pl.BlockSpec((L, LANES), lambda b, c: (tok(b, c), 0)),
                pl.BlockSpec((LANES, L), lambda b, c: (0, tok(b, c))),
                pl.BlockSpec((LANES, L), lambda b, c: (0, tok(b, c)))]
    args = [proj, proj, proj, pc, cc, pr, cr]
    if final:
        in_specs += [gspec(0), gspec(G_MO), pl.BlockSpec((H, 1, LANES), lambda b, c: (0, 0, 0))]
        args += [hf, proj, norm_g]
    return pl.pallas_call(
        functools.partial(_mlstm_kernel, reverse=reverse, final=final),
        grid=(B, nc),
        in_specs=in_specs,
        out_specs=gspec(0),
        out_shape=jax.ShapeDtypeStruct((H, T, LANES), BF16 if final else F32),
        scratch_shapes=[pltpu.VMEM((H, HEAD_DIM, HEAD_DIM), F32), pltpu.VMEM((H, 1, HEAD_DIM), F32),
                        pltpu.VMEM((H, 1, LANES), F32)],
        name="mlstm_bwd" if reverse else "mlstm_fwd",
        compiler_params=_cparams(("arbitrary", "arbitrary")),
    )(*args)


def _gla_kernel(*refs, reverse, final):
    if final:
        (q_ref, k_ref, v_ref, pc_ref, wa_ref, ab_ref, of_ref, r_ref, g_ref, out_ref, st_s) = refs
    else:
        q_ref, k_ref, v_ref, pc_ref, wa_ref, ab_ref, out_ref, st_s = refs
    c = pl.program_id(1)
    NG, TS = q_ref.shape[0], q_ref.shape[1]
    L = GLA_CHUNK
    n_sub = TS // L

    @pl.when(c == 0)
    def _():
        st_s[...] = jnp.zeros(st_s.shape, F32)

    row = lax.broadcasted_iota(jnp.int32, (TS, TS), 0)
    col = lax.broadcasted_iota(jnp.int32, (TS, TS), 1)
    same_chunk = (row // L) == (col // L)
    causal = same_chunk & ((col >= row) if reverse else (col <= row))
    cum_mat = causal.astype(BF16)
    lane = lax.broadcasted_iota(jnp.int32, (TS, LANES), 1)
    head_mask = [lane < GLA_HEAD_K, lane >= GLA_HEAD_K]
    pcb = pc_ref[...].astype(BF16)
    order = list(range(n_sub - 1, -1, -1)) if reverse else list(range(n_sub))
    pairs = range(NG)
    heads = range(2 * NG)


    la = [_log_sigmoid(_dot(pcb, wa_ref[gi]) + ab_ref[gi]) * (1.0 / GLA_TAU) for gi in pairs]
    split = [_split3(x) for x in la]
    b = [_dot(cum_mat, hi) + _dot(cum_mat, mid) + _dot(cum_mat, lo) for hi, mid, lo in split]

    def per_chunk_rows(x, idx):
        return jnp.concatenate([jnp.broadcast_to(x[s * L + idx:s * L + idx + 1, :], (L, LANES))
                                for s in range(n_sub)], axis=0)

    g_full = [per_chunk_rows(x, 0 if reverse else L - 1) for x in b]
    b_mid = [per_chunk_rows(x, L // 2 - 1 if reverse else L // 2) for x in b]
    qf = [q_ref[gi] * (GLA_HEAD_K ** -0.5) for gi in pairs]
    kf = [k_ref[gi] for gi in pairs]
    qd = [qf[gi] * jnp.exp(b[gi] - b_mid[gi]) for gi in pairs]
    kd = [(kf[gi] * jnp.exp(b_mid[gi] - b[gi])).astype(BF16) for gi in pairs]
    kg = [kf[gi] * jnp.exp(g_full[gi] - b[gi]) for gi in pairs]
    qe = [qf[gi] * jnp.exp(b[gi]) for gi in pairs]
    decay = [[jnp.exp(b[gi][s * L:s * L + 1, :] if reverse else b[gi][s * L + L - 1:s * L + L, :])
              for s in range(n_sub)] for gi in pairs]

    vb = [v_ref[hd].astype(BF16) for hd in heads]
    qd_h = [jnp.where(head_mask[hd % 2], qd[hd // 2], 0.0).astype(BF16) for hd in heads]
    kg_h = [jnp.where(head_mask[hd % 2], kg[hd // 2], 0.0).astype(BF16) for hd in heads]
    qe_h = [jnp.where(head_mask[hd % 2], qe[hd // 2], 0.0).astype(BF16) for hd in heads]
    a = [_dot_nt(qd_h[hd], kd[hd // 2]) for hd in heads]
    a = [jnp.where(causal, x, 0.0).astype(BF16) for x in a]
    o_intra = [_dot(a[hd], vb[hd]) for hd in heads]

    st = [st_s[hd] for hd in heads]
    o_inter = [[None] * n_sub for _ in heads]
    for s in order:
        sl = slice(s * L, (s + 1) * L)
        for hd in heads:
            o_inter[hd][s] = _dot_nt(qe_h[hd][sl], st[hd].astype(BF16))
        st_loc = [_dot_tn(vb[hd][sl], kg_h[hd][sl]) for hd in heads]
        st = [st[hd] * decay[hd // 2][s] + st_loc[hd] for hd in heads]
    for hd in heads:
        st_s[hd] = st[hd]

    for hd in heads:
        o = o_intra[hd] + jnp.concatenate(o_inter[hd], axis=0)
        if final:
            ot = of_ref[hd] + o
            on = ot * lax.rsqrt(jnp.mean(jnp.square(ot), axis=-1, keepdims=True) + HEAD_NORM_EPS) * g_ref[hd]
            rr = r_ref[hd]
            out_ref[hd] = (on * (rr * _sigmoid(rr))).astype(out_ref.dtype)
        else:
            out_ref[hd] = o


def gla_direction(proj, pc, wa, a_bias, B, S, reverse, of=None, norm_g=None):
    TS = GLA_STEP
    nc = S // TS
    T = B * S
    NG = GLA_HEADS // 2
    final = of is not None
    d = 1 if reverse else 0

    def tok(b, c):
        return b * nc + (nc - 1 - c if reverse else c)

    def gspec(goff, n):
        return pl.BlockSpec((n, TS, LANES), lambda b, c: (goff // n, tok(b, c), 0))

    in_specs = [gspec(G_GQ, NG), gspec(G_GK, NG), gspec(G_GV, GLA_HEADS),
                pl.BlockSpec((TS, LANES), lambda b, c: (tok(b, c), 0)),
                pl.BlockSpec((None, NG, LANES, LANES), lambda b, c: (d, 0, 0, 0)),
                pl.BlockSpec((None, NG, 1, LANES), lambda b, c: (d, 0, 0, 0))]
    args = [proj, proj, proj, pc, wa, a_bias]
    if final:
        in_specs += [gspec(0, GLA_HEADS), gspec(G_GR, GLA_HEADS),
                     pl.BlockSpec((GLA_HEADS, 1, LANES), lambda b, c: (0, 0, 0))]
        args += [of, proj, norm_g]
    return pl.pallas_call(
        functools.partial(_gla_kernel, reverse=reverse, final=final),
        grid=(B, nc),
        in_specs=in_specs,
        out_specs=gspec(0, GLA_HEADS),
        out_shape=jax.ShapeDtypeStruct((GLA_HEADS, T, LANES), BF16 if final else F32),
        scratch_shapes=[pltpu.VMEM((GLA_HEADS, HEAD_DIM, LANES), F32)],
        name="gla_bwd" if reverse else "gla_fwd",
        compiler_params=_cparams(("arbitrary", "arbitrary")),
    )(*args)


def _rope(t, cos2, sin2):
    return t * cos2 + pltpu.roll(t, shift=HEAD_DIM // 2, axis=1) * sin2


def _attn_kernel(q_ref, kp_ref, km_ref, kn_ref, vp_ref, vm_ref, vn_ref,
                 cp_ref, cm_ref, cn_ref, sp_ref, sm_ref, sn_ref, out_ref,
                 q_s, k_s, v_s, q4_s, k4_s, v4_s, st0_s, st4_s, o_s, *, seq_len):
    i = pl.program_id(1)
    TQ, HALO, QB, KB = ATTN_TILE, ATTN_HALO, ATTN_QB, ATTN_KB
    R = ATTN_DILATIONS[1]
    t0 = i * TQ

    q_s[...] = _rope(q_ref[...], cm_ref[...], sm_ref[...]) * (HEAD_DIM ** -0.5)
    k_s[0:HALO, :] = _rope(kp_ref[...], cp_ref[...], sp_ref[...])
    k_s[HALO:HALO + TQ, :] = _rope(km_ref[...], cm_ref[...], sm_ref[...])
    k_s[HALO + TQ:, :] = _rope(kn_ref[...], cn_ref[...], sn_ref[...])
    v_s[0:HALO, :] = vp_ref[...]
    v_s[HALO:HALO + TQ, :] = vm_ref[...]
    v_s[HALO + TQ:, :] = vn_ref[...]
    for j in range(R):
        q4_s[j] = q_s[pl.ds(j, TQ // R, stride=R), :]
        k4_s[j] = k_s[pl.ds(j, (TQ + 2 * HALO) // R, stride=R), :]
        v4_s[j] = v_s[pl.ds(j, (TQ + 2 * HALO) // R, stride=R), :]

    qi = lax.broadcasted_iota(jnp.int32, (QB, KB), 0)
    ki = lax.broadcasted_iota(jnp.int32, (QB, KB), 1)
    band = (ki >= qi) & (ki <= qi + 2 * ATTN_REACH)
    kcol = lax.broadcasted_iota(jnp.int32, (1, KB), 1)

    def block_group(blocks):
        n = range(len(blocks))
        qb = [blocks[u][0]().astype(BF16) for u in n]
        kb = [blocks[u][1]().astype(BF16) for u in n]
        vb = [blocks[u][2]().astype(BF16) for u in n]
        s = [_dot_nt(qb[u], kb[u]) for u in n]
        kpos = [blocks[u][3] for u in n]
        s = [jnp.where(band & (kpos[u] >= 0) & (kpos[u] < seq_len), s[u], -jnp.inf) for u in n]
        m = [jnp.max(s[u], axis=-1, keepdims=True) for u in n]
        e = [jnp.exp(s[u] - m[u]) for u in n]
        acc = [_dot(e[u].astype(BF16), vb[u]) for u in n]
        for u in n:
            store = blocks[u][4]
            store(0, acc[u])
            store(1, jnp.broadcast_to(m[u], (QB, LANES)))
            store(2, jnp.broadcast_to(jnp.sum(e[u], axis=-1, keepdims=True), (QB, LANES)))

    def dense_block(it):
        q0 = it * QB
        k0 = HALO + q0 - ATTN_REACH

        def store(which, val):
            st0_s[which, pl.ds(q0, QB), :] = val
        return (lambda: q_s[pl.ds(q0, QB), :], lambda: k_s[pl.ds(k0, KB), :], lambda: v_s[pl.ds(k0, KB), :],
                t0 - HALO + k0 + kcol, store)

    def mod4_block(it):
        j = it % R
        a0 = (it // R) * QB
        k0 = HALO // R + a0 - ATTN_REACH

        def store(which, val):
            st4_s[0, which, j, pl.ds(a0, QB), :] = val
        return (lambda: q4_s[j, pl.ds(a0, QB), :], lambda: k4_s[j, pl.ds(k0, KB), :],
                lambda: v4_s[j, pl.ds(k0, KB), :], t0 - HALO + R * (k0 + kcol) + j, store)

    def mod16_block(r):
        j = r % R
        m0 = r // R

        def store(which, val):
            st4_s[1, which, j, pl.ds(m0, QB, stride=R), :] = val
        return (lambda: q4_s[j, pl.ds(m0, QB, stride=R), :], lambda: k4_s[j, pl.ds(m0, KB, stride=R), :],
                lambda: v4_s[j, pl.ds(m0, KB, stride=R), :], t0 - HALO + r + ATTN_DILATIONS[2] * kcol, store)

    for make_block in (dense_block, mod4_block, mod16_block):
        def body(g, carry, make_block=make_block):
            block_group([make_block(g * ATTN_UNROLL + u) for u in range(ATTN_UNROLL)])
            return carry

        lax.fori_loop(0, (TQ // QB) // ATTN_UNROLL, body, 0)

    for j in range(R):
        rows = pl.ds(j, TQ // R, stride=R)
        parts = [(st0_s[0, rows, :], st0_s[1, rows, :], st0_s[2, rows, :]),
                 (st4_s[0, 0, j], st4_s[0, 1, j], st4_s[0, 2, j]),
                 (st4_s[1, 0, j], st4_s[1, 1, j], st4_s[1, 2, j])]
        m_all = jnp.maximum(jnp.maximum(parts[0][1], parts[1][1]), parts[2][1])
        num = jnp.zeros((TQ // R, LANES), F32)
        den = jnp.zeros((TQ // R, LANES), F32)
        for acc, m, l in parts:
            w = jnp.exp(m - m_all)
            num = num + w * acc
            den = den + w * l
        o_s[rows, :] = num / den
    out_ref[...] = o_s[...].astype(out_ref.dtype)


def dilated_attention(proj, cos2, sin2, B, S):
    TQ, HALO = ATTN_TILE, ATTN_HALO
    nt = S // TQ
    T = B * S
    hpt = TQ // HALO
    n_halo = T // HALO
    KT = TQ + 2 * HALO
    R = ATTN_DILATIONS[1]

    def main(goff):
        return pl.BlockSpec((None, TQ, LANES), lambda b, i, h: (goff + h, b * nt + i, 0))

    def prev(goff):
        return pl.BlockSpec((None, HALO, LANES),
                            lambda b, i, h: (goff + h, jnp.maximum((b * nt + i) * hpt - 1, 0), 0))

    def nxt(goff):
        return pl.BlockSpec((None, HALO, LANES),
                            lambda b, i, h: (goff + h, jnp.minimum((b * nt + i + 1) * hpt, n_halo - 1), 0))

    t_prev = pl.BlockSpec((HALO, LANES), lambda b, i, h: (jnp.maximum(i * hpt - 1, 0), 0))
    t_main = pl.BlockSpec((TQ, LANES), lambda b, i, h: (i, 0))
    t_next = pl.BlockSpec((HALO, LANES), lambda b, i, h: (jnp.minimum((i + 1) * hpt, S // HALO - 1), 0))

    return pl.pallas_call(
        functools.partial(_attn_kernel, seq_len=S),
        grid=(B, nt, ATTN_HEADS),
        in_specs=[main(G_AQ), prev(G_AK), main(G_AK), nxt(G_AK), prev(G_AV), main(G_AV), nxt(G_AV),
                  t_prev, t_main, t_next, t_prev, t_main, t_next],
        out_specs=main(0),
        out_shape=jax.ShapeDtypeStruct((ATTN_HEADS, T, LANES), BF16),
        scratch_shapes=[pltpu.VMEM((TQ, LANES), F32), pltpu.VMEM((KT, LANES), F32), pltpu.VMEM((KT, LANES), F32),
                        pltpu.VMEM((R, TQ // R, LANES), F32), pltpu.VMEM((R, KT // R, LANES), F32),
                        pltpu.VMEM((R, KT // R, LANES), F32), pltpu.VMEM((3, TQ, LANES), F32),
                        pltpu.VMEM((2, 3, R, TQ // R, LANES), F32), pltpu.VMEM((TQ, LANES), F32)],
        name="dilated_attention",
        compiler_params=_cparams(("parallel", "parallel", "arbitrary")),
    )(proj, proj, proj, proj, proj, proj, proj, cos2, cos2, cos2, sin2, sin2, sin2)


FFN_HALO = 16


def _up_conv_gate_kernel(xp_ref, xm_ref, xn_ref, wg_ref, wv_ref, cwg_ref, cwv_ref, cbg_ref, cbv_ref, o_ref,
                         xs_ref, *, n_tiles):
    i = pl.program_id(1)
    j = pl.program_id(2)
    tm = xm_ref.shape[0]
    hr = xp_ref.shape[0]

    @pl.when(j == 0)
    def _():
        xs_ref[0:hr, :] = xp_ref[...] * (i > 0).astype(BF16)
        xs_ref[hr:hr + tm, :] = xm_ref[...]
        xs_ref[hr + tm:, :] = xn_ref[...] * (i < n_tiles - 1).astype(BF16)

    xs = xs_ref[...]

    def conv(w_ref, cw_ref, cb_ref):
        u = _dot(xs, w_ref[...])
        return (cb_ref[...] + u[hr - 1:hr - 1 + tm] * cw_ref[0:1, :] + u[hr:hr + tm] * cw_ref[1:2, :]
                + u[hr + 1:hr + 1 + tm] * cw_ref[2:3, :])

    gate = conv(wg_ref, cwg_ref, cbg_ref)
    val = conv(wv_ref, cwv_ref, cbv_ref)
    o_ref[...] = (gate * _sigmoid(gate) * val).astype(o_ref.dtype)


def up_conv_gate(xb, w_up, layer, conv_w, conv_b, B, S, tm=1024, tf=256):
    T, K = xb.shape
    nt = S // tm
    nf = D_FF // tf
    hb = tm // FFN_HALO
    n_hb = T // FFN_HALO

    def wspec(off, r):
        return pl.BlockSpec((r, tf), lambda b, i, j: (0, off + j))

    def upspec(off):
        return pl.BlockSpec((None, K, tf), lambda b, i, j: (layer, 0, off + j))

    return pl.pallas_call(
        functools.partial(_up_conv_gate_kernel, n_tiles=nt),
        grid=(B, nt, nf),
        in_specs=[pl.BlockSpec((FFN_HALO, K), lambda b, i, j: (jnp.maximum((b * nt + i) * hb - 1, 0), 0)),
                  pl.BlockSpec((tm, K), lambda b, i, j: (b * nt + i, 0)),
                  pl.BlockSpec((FFN_HALO, K), lambda b, i, j: (jnp.minimum((b * nt + i + 1) * hb, n_hb - 1), 0)),
                  upspec(0), upspec(nf), wspec(0, 3), wspec(nf, 3), wspec(0, 1), wspec(nf, 1)],
        out_specs=pl.BlockSpec((tm, tf), lambda b, i, j: (b * nt + i, j)),
        out_shape=jax.ShapeDtypeStruct((T, D_FF), BF16),
        scratch_shapes=[pltpu.VMEM((tm + 2 * FFN_HALO, K), BF16)],
        name="up_conv_gate",
        compiler_params=_cparams(("parallel", "parallel", "arbitrary")),
    )(xb, xb, xb, w_up, w_up, conv_w, conv_w, conv_b, conv_b)


def _ln_ple_kernel(z_ref, g_ref, b_ref, wg_ref, p_ref, wp_ref, o_ref, ob_ref, x_s, xb_s):
    j = pl.program_id(1)
    tn = wg_ref.shape[1]

    @pl.when(j == 0)
    def _():
        z = z_ref[...]
        mu = jnp.mean(z, axis=-1, keepdims=True)
        zc = z - mu
        var = jnp.mean(jnp.square(zc), axis=-1, keepdims=True)
        x = zc * lax.rsqrt(var + LN_EPS) * g_ref[...] + b_ref[...]
        xb_s[...] = x.astype(BF16)
        for jj in range(x_s.shape[0]):
            x_s[jj] = x[:, jj * tn:(jj + 1) * tn]

    gate = _sigmoid(_dot(xb_s[...], wg_ref[...]))
    emb = _dot(p_ref[...].astype(BF16), wp_ref[...])
    out = x_s[j] + gate * emb
    o_ref[...] = out
    ob_ref[...] = out.astype(BF16)


def ln_ple_update(z, g, b, p, w_gate, w_ple, layer, tm=512, tn=512):
    T, D = z.shape
    vec = pl.BlockSpec((1, D), lambda i, j: (0, 0))
    return pl.pallas_call(
        _ln_ple_kernel,
        grid=(T // tm, D // tn),
        in_specs=[pl.BlockSpec((tm, D), lambda i, j: (i, 0)), vec, vec,
                  pl.BlockSpec((None, D, tn), lambda i, j: (layer, 0, j)),
                  pl.BlockSpec((tm, PLE_DIM), lambda i, j: (i, 0)),
                  pl.BlockSpec((None, PLE_DIM, tn), lambda i, j: (layer, 0, j))],
        out_specs=[pl.BlockSpec((tm, tn), lambda i, j: (i, j)), pl.BlockSpec((tm, tn), lambda i, j: (i, j))],
        out_shape=[jax.ShapeDtypeStruct((T, D), F32), jax.ShapeDtypeStruct((T, D), BF16)],
        scratch_shapes=[pltpu.VMEM((D // tn, tm, tn), F32), pltpu.VMEM((tm, D), BF16)],
        name="ln_ple_update",
        compiler_params=_cparams(("parallel", "arbitrary")),
    )(z, g, b, w_gate, p, w_ple)


IN_PROJ_ROW_UNIT = 4 * MLSTM_HEADS


def _in_proj_segments():
    mw = MLSTM_HEADS * HEAD_DIM
    aw = ATTN_HEADS * HEAD_DIM
    gkw = GLA_HEADS * GLA_HEAD_K
    gvw = GLA_HEADS * HEAD_DIM
    o_mq, o_mk, o_mv, o_mo = 0, mw, 2 * mw, 3 * mw
    o_gate = 4 * mw
    o_aq = o_gate + 4 * MLSTM_HEADS
    o_ak, o_av = o_aq + aw, o_aq + 2 * aw
    o_gq = o_aq + 3 * aw
    o_gk = o_gq + gkw
    o_gv = o_gk + gkw
    o_gr = o_gv + gvw
    o_alpha = o_gr + gvw
    big = [(o_gv, gvw), (o_gr, gvw), (o_gq, gkw), (o_gk, gkw), (o_aq, aw), (o_mq, mw), (o_mk, mw),
           (o_mv, mw), (o_mo, mw), (o_ak, aw), (o_av, aw)]
    return big, (o_gate, 4 * MLSTM_HEADS), (o_alpha, 2 * GLA_RANK)


def _regroup_rows_kernel(tbl_ref, *refs):
    *in_refs, o_ref, small_ref = refs
    c = pl.program_id(0)
    unit = in_refs[0].shape[0]

    @pl.when(c < N_PROJ_GROUPS)
    def _():
        for u, r in enumerate(in_refs):
            o_ref[u * unit:(u + 1) * unit, :] = r[...].astype(o_ref.dtype)

    @pl.when(c == N_PROJ_GROUPS)
    def _():
        small_ref[0:unit, :] = in_refs[0][...].astype(small_ref.dtype)
        small_ref[unit:2 * unit, :] = in_refs[1][...].astype(small_ref.dtype)
        small_ref[2 * unit:, :] = jnp.zeros((small_ref.shape[0] - 2 * unit, small_ref.shape[1]), small_ref.dtype)


def regroup_input_weights(w_in_t, layer):
    K = w_in_t.shape[2]
    unit = IN_PROJ_ROW_UNIT
    per_group = LANES // unit
    big, (o_gate, n_gate), (o_alpha, n_alpha) = _in_proj_segments()
    assert n_gate == unit and n_alpha == unit and o_gate % unit == 0 and o_alpha % unit == 0
    tbl = []
    for off, width in big:
        assert off % unit == 0 and width % LANES == 0
        for g in range(width // LANES):
            tbl += [(off + g * LANES) // unit + u for u in range(per_group)]
    n_groups = len(tbl) // per_group
    assert n_groups == N_PROJ_GROUPS
    tbl += [o_gate // unit, o_alpha // unit] + [0] * (per_group - 2)
    grid_spec = pltpu.PrefetchScalarGridSpec(
        num_scalar_prefetch=1,
        grid=(n_groups + 1,),
        in_specs=[pl.BlockSpec((None, unit, K), lambda c, tbl, u=u: (layer, tbl[c * per_group + u], 0))
                  for u in range(per_group)],
        out_specs=[pl.BlockSpec((LANES, K), lambda c, tbl: (jnp.minimum(c, n_groups - 1), 0)),
                   pl.BlockSpec((LANES, K), lambda c, tbl: (0, 0))])
    return pl.pallas_call(
        _regroup_rows_kernel,
        grid_spec=grid_spec,
        out_shape=[jax.ShapeDtypeStruct((n_groups * LANES, K), BF16), jax.ShapeDtypeStruct((LANES, K), BF16)],
        name="regroup_input_weights",
        compiler_params=_cparams(("arbitrary",)),
    )(jnp.asarray(tbl, jnp.int32), *([w_in_t] * per_group))


def _prepare_layer_params(layer, w_in_t, big, mlstm_gate_b, mlstm_norm_g, gla_w_a2, gla_a_b, gla_norm_g,
                          ln1_g, ln1_b, conv_w, conv_b, ln2_g, ln2_b):
    n_gate = 4 * MLSTM_HEADS
    w_big_t, w_small_t = regroup_input_weights(w_in_t, layer)
    bias = jnp.pad(mlstm_gate_b.reshape(-1), (0, LANES - n_gate)).astype(F32)
    wa = jnp.zeros((2, GLA_HEADS // 2, LANES, LANES), F32)
    a2 = gla_w_a2.reshape(2, GLA_RANK, GLA_HEADS // 2, LANES).transpose(0, 2, 1, 3)
    wa = wa.at[0, :, GATE_A_F:GATE_A_F + GLA_RANK, :].set(a2[0])
    wa = wa.at[1, :, GATE_A_B:GATE_A_B + GLA_RANK, :].set(a2[1])
    return dict(
        w_big_t=w_big_t, w_small=w_small_t.T, w_small_t=w_small_t,
        bias_c=bias.reshape(1, LANES), bias_r=bias.reshape(LANES, 1),
        mlstm_norm_g=mlstm_norm_g.reshape(MLSTM_HEADS, 1, HEAD_DIM),
        wa=wa.astype(BF16), a_bias=gla_a_b.reshape(2, GLA_HEADS // 2, 1, LANES),
        gla_norm_g=gla_norm_g.reshape(GLA_HEADS, 1, HEAD_DIM),
        ln1_g=ln1_g.reshape(1, -1), ln1_b=ln1_b.reshape(1, -1), conv_w=conv_w, conv_b=conv_b.reshape(1, -1),
        ln2_g=ln2_g.reshape(1, -1), ln2_b=ln2_b.reshape(1, -1), layer=layer, **big)


def _rope_tables(S):
    half = HEAD_DIM // 2
    inv = ROPE_THETA ** (-jnp.arange(half, dtype=F32) / half)
    ang = jnp.arange(S, dtype=F32)[:, None] * inv[None, :]
    cos, sin = jnp.cos(ang), jnp.sin(ang)
    return jnp.concatenate([cos, cos], axis=-1), jnp.concatenate([-sin, sin], axis=-1)


def _encoder_layer(x, xb, p_i, prm, tables, B, S):
    proj = proj_matmul(xb, prm['w_big_t'])
    gates = gates_project(xb, prm['w_small'], prm['w_small_t'], prm['bias_c'], prm['bias_r'], MLSTM_CHUNK)
    hf = mlstm_direction(proj, gates, B, S, reverse=False)
    y_m = mlstm_direction(proj, gates, B, S, reverse=True, hf=hf, norm_g=prm['mlstm_norm_g'])
    y_a = dilated_attention(proj, tables[0], tables[1], B, S)
    of = gla_direction(proj, gates[0], prm['wa'], prm['a_bias'], B, S, reverse=False)
    y_g = gla_direction(proj, gates[0], prm['wa'], prm['a_bias'], B, S, reverse=True, of=of,
                        norm_g=prm['gla_norm_g'])
    layer = prm['layer']
    x, xb = mix_out_layer_norm(y_m, y_a, y_g, prm['w_out'], layer, x, prm['ln1_g'], prm['ln1_b'])
    hmid = up_conv_gate(xb, prm['w_up'], layer, prm['conv_w'], prm['conv_b'], B, S)
    z = matmul_residual(hmid, prm['w_down'], layer, x, tm=512, tn=512)
    return ln_ple_update(z, prm['ln2_g'], prm['ln2_b'], p_i, prm['w_ple_gate'], prm['w_ple'], layer)


def _trunk(x, p, layer_params):
    B, S, D = x.shape
    tables = _rope_tables(S)
    x = x.reshape(B * S, D)
    xb = x.astype(BF16)
    for i, prm in enumerate(layer_params):
        x, xb = _encoder_layer(x, xb, p[i].reshape(B * S, -1), prm, tables, B, S)
    return x.reshape(B, S, D)


def kernel(x_prompt, x_sample, p_prompt, p_sample, w_in, mlstm_gate_b, mlstm_norm_g, gla_w_a2, gla_a_b,
           gla_norm_g, w_out, ln1_g, ln1_b, w_up, conv_w, conv_b, w_down, ln2_g, ln2_b, w_ple, w_ple_gate):
    big = dict(w_out=w_out.astype(BF16), w_up=w_up.astype(BF16), w_down=w_down.astype(BF16),
               w_ple=w_ple.astype(BF16), w_ple_gate=w_ple_gate.astype(BF16))
    small = (mlstm_gate_b, mlstm_norm_g, gla_w_a2, gla_a_b, gla_norm_g, ln1_g, ln1_b, conv_w, conv_b, ln2_g, ln2_b)
    w_in_t = jnp.swapaxes(w_in, 1, 2)
    layer_params = [_prepare_layer_params(i, w_in_t, big, *(w[i] for w in small)) for i in range(w_in.shape[0])]
    return (_trunk(x_prompt, p_prompt, layer_params), _trunk(x_sample, p_sample, layer_params))
```

```python
import functools

import jax
import jax.numpy as jnp
from jax import lax
from jax.experimental import pallas as pl
from jax.experimental.pallas import tpu as pltpu

F32 = jnp.float32
BF16 = jnp.bfloat16

D_MODEL = 4096
HEAD_DIM = 128
MLSTM_HEADS = 8
ATTN_HEADS = 12
GLA_HEADS = 12
GLA_HEAD_K = 64
GLA_RANK = 16
GLA_TAU = 16.0
GLA_CHUNK = 64
ATTN_REACH = 64
ATTN_DILATIONS = (1, 4, 16)
ROPE_THETA = 10000.0
D_FF = 11008
PLE_DIM = 256
LN_EPS = 1e-5
HEAD_NORM_EPS = 1e-6
DEPTH = 2
DEEPNORM_ALPHA = (2 * DEPTH) ** 0.25

LANES = 128
VMEM_LIMIT = 56 * 1024 * 1024

G_GV, G_GR, G_GQ, G_GK = 0, 12, 24, 30
G_AQ = 36
G_MQ, G_MK, G_MV, G_MO = 48, 56, 64, 72
G_AK, G_AV = 80, 92
N_PROJ_GROUPS = 104
GATE_I_F, GATE_F_F, GATE_I_B, GATE_F_B, GATE_A_F, GATE_A_B = 0, 8, 16, 24, 32, 48

MLSTM_CHUNK = 256
GLA_STEP = 256
ATTN_TILE = 2048
ATTN_HALO = ATTN_REACH * ATTN_DILATIONS[-1]
ATTN_QB = 128
ATTN_KB = ATTN_QB + 2 * ATTN_REACH
ATTN_UNROLL = 8


def _cparams(sem):
    return pltpu.CompilerParams(dimension_semantics=sem, vmem_limit_bytes=VMEM_LIMIT)


def _log_sigmoid(x):
    return jnp.minimum(x, 0.0) - jnp.log1p(jnp.exp(-jnp.abs(x)))


def _sigmoid(x):
    return lax.logistic(x)


def _dot(a, b):
    return jnp.dot(a, b, preferred_element_type=F32)


def _dot_nt(a, b):
    return lax.dot_general(a, b, (((1,), (1,)), ((), ())), preferred_element_type=F32)


def _dot_tn(a, b):
    return lax.dot_general(a, b, (((0,), (0,)), ((), ())), preferred_element_type=F32)


def _split3(x):
    hi = x.astype(BF16)
    r1 = x - hi.astype(F32)
    mid = r1.astype(BF16)
    lo = (r1 - mid.astype(F32)).astype(BF16)
    return hi, mid, lo


def _tri_dot(tri, x):
    hi, mid, lo = _split3(x)
    return _dot(tri, hi) + _dot(tri, mid) + _dot(tri, lo)


def _dot_tri(x, tri):
    hi, mid, lo = _split3(x)
    return _dot(hi, tri) + _dot(mid, tri) + _dot(lo, tri)


def _proj_kernel(x_ref, wt_ref, o_ref):
    acc = _dot_nt(x_ref[...], wt_ref[...])
    for g in range(o_ref.shape[0]):
        o_ref[g] = acc[:, g * LANES:(g + 1) * LANES].astype(o_ref.dtype)


def proj_matmul(xb, wt, tm=1024, tn=1024):
    T, K = xb.shape
    N = wt.shape[0]
    return pl.pallas_call(
        _proj_kernel,
        grid=(T // tm, N // tn),
        in_specs=[pl.BlockSpec((tm, K), lambda i, j: (i, 0)),
                  pl.BlockSpec((tn, K), lambda i, j: (j, 0))],
        out_specs=pl.BlockSpec((tn // LANES, tm, LANES), lambda i, j: (j, i, 0)),
        out_shape=jax.ShapeDtypeStruct((N // LANES, T, LANES), F32),
        name="proj_matmul",
        compiler_params=_cparams(("parallel", "parallel")),
    )(xb, wt)


def _mm_residual_kernel(x_ref, w_ref, r_ref, o_ref):
    o_ref[...] = DEEPNORM_ALPHA * r_ref[...] + _dot(x_ref[...], w_ref[...])


def matmul_residual(xb, w, layer, res, tm, tn):
    T, K = xb.shape
    N = w.shape[2]
    return pl.pallas_call(
        _mm_residual_kernel,
        grid=(T // tm, N // tn),
        in_specs=[pl.BlockSpec((tm, K), lambda i, j: (i, 0)),
                  pl.BlockSpec((None, K, tn), lambda i, j: (layer, 0, j)),
                  pl.BlockSpec((tm, tn), lambda i, j: (i, j))],
        out_specs=pl.BlockSpec((tm, tn), lambda i, j: (i, j)),
        out_shape=jax.ShapeDtypeStruct((T, N), F32),
        name="matmul_residual",
        compiler_params=_cparams(("parallel", "parallel")),
    )(xb, w, res)


def _mix_out_ln_kernel(ym_ref, ya_ref, yg_ref, w_ref, x_ref, g_ref, b_ref, o_ref, ob_ref):
    j = pl.program_id(1)
    tn = w_ref.shape[1]
    d = o_ref.shape[1]
    n_slabs = d // tn
    parts = ([ym_ref[g] for g in range(ym_ref.shape[0])] + [ya_ref[g] for g in range(ya_ref.shape[0])]
             + [yg_ref[g] for g in range(yg_ref.shape[0])])
    y = jnp.concatenate(parts, axis=-1)
    o_ref[:, pl.ds(pl.multiple_of(j * tn, tn), tn)] = DEEPNORM_ALPHA * x_ref[...] + _dot(y, w_ref[...])

    @pl.when(j == n_slabs - 1)
    def _():
        slabs = [slice(s * tn, (s + 1) * tn) for s in range(n_slabs)]
        mu = sum(jnp.sum(o_ref[:, c], axis=-1, keepdims=True) for c in slabs) * (1.0 / d)
        var = sum(jnp.sum(jnp.square(o_ref[:, c] - mu), axis=-1, keepdims=True) for c in slabs) * (1.0 / d)
        inv = lax.rsqrt(var + LN_EPS)
        for c in slabs:
            out = (o_ref[:, c] - mu) * inv * g_ref[:, c] + b_ref[:, c]
            o_ref[:, c] = out
            ob_ref[:, c] = out.astype(BF16)


def mix_out_layer_norm(ym, ya, yg, w, layer, x, g, b, tm=512, tn=512):
    T = ym.shape[1]
    K, N = w.shape[1], w.shape[2]

    def gspec(a):
        return pl.BlockSpec((a.shape[0], tm, LANES), lambda i, j: (0, i, 0))

    vec = pl.BlockSpec((1, N), lambda i, j: (0, 0))
    row = pl.BlockSpec((tm, N), lambda i, j: (i, 0))
    return pl.pallas_call(
        _mix_out_ln_kernel,
        grid=(T // tm, N // tn),
        in_specs=[gspec(ym), gspec(ya), gspec(yg), pl.BlockSpec((None, K, tn), lambda i, j: (layer, 0, j)),
                  pl.BlockSpec((tm, tn), lambda i, j: (i, j)), vec, vec],
        out_specs=[row, row],
        out_shape=[jax.ShapeDtypeStruct((T, N), F32), jax.ShapeDtypeStruct((T, N), BF16)],
        name="mix_out_layer_norm",
        compiler_params=_cparams(("parallel", "arbitrary")),
    )(ym, ya, yg, w, x, g, b)


def _gates_kernel(x_ref, w_ref, wt_ref, bc_ref, br_ref, pc_ref, cc_ref, pr_ref, cr_ref, *, chunk):
    x = x_ref[...]
    tm = x.shape[0]
    pc = _dot(x, w_ref[...]) + bc_ref[...]
    pr = _dot_nt(wt_ref[...], x) + br_ref[...]
    pc_ref[...] = pc
    pr_ref[...] = pr
    ls_c = _log_sigmoid(pc)
    ls_r = _log_sigmoid(pr)
    row = lax.broadcasted_iota(jnp.int32, (chunk, chunk), 0)
    col = lax.broadcasted_iota(jnp.int32, (chunk, chunk), 1)
    lower = (row >= col).astype(BF16)
    upper = (row <= col).astype(BF16)
    lane = lax.broadcasted_iota(jnp.int32, (chunk, LANES), 1)
    sub = lax.broadcasted_iota(jnp.int32, (LANES, chunk), 0)
    bwd_c = (lane >= GATE_F_B) & (lane < GATE_F_B + MLSTM_HEADS)
    bwd_r = (sub >= GATE_F_B) & (sub < GATE_F_B + MLSTM_HEADS)
    for c in range(tm // chunk):
        sl = slice(c * chunk, (c + 1) * chunk)
        lc = ls_c[sl]
        cc_ref[sl, :] = jnp.where(bwd_c, _tri_dot(upper, lc), _tri_dot(lower, lc))
        lr = ls_r[:, sl]
        cr_ref[:, sl] = jnp.where(bwd_r, _dot_tri(lr, lower), _dot_tri(lr, upper))


def gates_project(xb, w_small, w_small_t, bias_c, bias_r, chunk, tm=1024):
    T, K = xb.shape
    return pl.pallas_call(
        functools.partial(_gates_kernel, chunk=chunk),
        grid=(T // tm,),
        in_specs=[pl.BlockSpec((tm, K), lambda i: (i, 0)),
                  pl.BlockSpec((K, LANES), lambda i: (0, 0)),
                  pl.BlockSpec((LANES, K), lambda i: (0, 0)),
                  pl.BlockSpec((1, LANES), lambda i: (0, 0)),
                  pl.BlockSpec((LANES, 1), lambda i: (0, 0))],
        out_specs=[pl.BlockSpec((tm, LANES), lambda i: (i, 0)),
                   pl.BlockSpec((tm, LANES), lambda i: (i, 0)),
                   pl.BlockSpec((LANES, tm), lambda i: (0, i)),
                   pl.BlockSpec((LANES, tm), lambda i: (0, i))],
        out_shape=[jax.ShapeDtypeStruct((T, LANES), F32), jax.ShapeDtypeStruct((T, LANES), F32),
                   jax.ShapeDtypeStruct((LANES, T), F32), jax.ShapeDtypeStruct((LANES, T), F32)],
        name="gates_project",
        compiler_params=_cparams(("parallel",)),
    )(xb, w_small, w_small_t, bias_c, bias_r)


def _mlstm_kernel(*refs, reverse, final):
    if final:
        (q_ref, k_ref, v_ref, pc_ref, cc_ref, pr_ref, cr_ref, hf_ref, o_ref, g_ref,
         out_ref, ct_s, n_s, m_s) = refs
    else:
        q_ref, k_ref, v_ref, pc_ref, cc_ref, pr_ref, cr_ref, out_ref, ct_s, n_s, m_s = refs
    c = pl.program_id(1)
    H, L = q_ref.shape[0], q_ref.shape[1]

    @pl.when(c == 0)
    def _():
        ct_s[...] = jnp.zeros(ct_s.shape, F32)
        n_s[...] = jnp.zeros(n_s.shape, F32)
        m_s[...] = jnp.zeros(m_s.shape, F32)

    pc = pc_ref[...]
    cc = cc_ref[...]
    lane = lax.broadcasted_iota(jnp.int32, (L, LANES), 1)
    t_idx = lax.broadcasted_iota(jnp.int32, (L, L), 0)
    s_idx = lax.broadcasted_iota(jnp.int32, (L, L), 1)
    causal = (s_idx >= t_idx) if reverse else (s_idx <= t_idx)

    heads = range(H)
    icol = [(GATE_I_B if reverse else GATE_I_F) + h for h in heads]
    fcol = [(GATE_F_B if reverse else GATE_F_F) + h for h in heads]
    qf = [q_ref[h] for h in heads]
    kf = [k_ref[h] * (HEAD_DIM ** -0.5) for h in heads]
    qb = [x.astype(BF16) for x in qf]
    kb = [x.astype(BF16) for x in kf]
    vb = [v_ref[h].astype(BF16) for h in heads]
    ct = [ct_s[h] for h in heads]
    n_prev = [n_s[h] for h in heads]
    m_prev = [m_s[h][:, 0:1] for h in heads]

    i_col = [jnp.sum(jnp.where(lane == icol[h], pc, 0.0), axis=-1, keepdims=True) for h in heads]
    b_col = [jnp.sum(jnp.where(lane == fcol[h], cc, 0.0), axis=-1, keepdims=True) for h in heads]
    r_col = [i_col[h] - b_col[h] for h in heads]
    b_row = [cr_ref[fcol[h]:fcol[h] + 1, :] for h in heads]
    r_row = [pr_ref[icol[h]:icol[h] + 1, :] - b_row[h] for h in heads]

    r_mat = [jnp.where(causal, r_row[h], -jnp.inf) for h in heads]
    m_row = [jnp.maximum(jnp.max(r_mat[h], axis=-1, keepdims=True), m_prev[h]) for h in heads]
    scores = [_dot_nt(qb[h], kb[h]) for h in heads]
    inter = [_dot(qb[h], ct[h].astype(BF16)) for h in heads]
    p = [jnp.exp(r_mat[h] - m_row[h]) * scores[h] for h in heads]
    w_inter = [jnp.exp(m_prev[h] - m_row[h]) for h in heads]
    pv = [_dot(p[h].astype(BF16), vb[h]) for h in heads]
    nq = [w_inter[h] * jnp.sum(qf[h] * n_prev[h], axis=-1, keepdims=True)
          + jnp.sum(p[h], axis=-1, keepdims=True) for h in heads]
    den = [jnp.maximum(jnp.abs(nq[h]), jnp.exp(-(b_col[h] + m_row[h]))) for h in heads]
    hout = [(w_inter[h] * inter[h] + pv[h]) / den[h] for h in heads]

    g = [b_row[h][:, 0:1] if reverse else b_row[h][:, L - 1:L] for h in heads]
    r_max = [jnp.max(r_row[h], axis=-1, keepdims=True) for h in heads]
    m_loc = [g[h] + r_max[h] for h in heads]
    m_new = [jnp.maximum(g[h] + m_prev[h], m_loc[h]) for h in heads]
    s_old = [jnp.exp(g[h] + m_prev[h] - m_new[h]) for h in heads]
    s_new = [jnp.exp(m_loc[h] - m_new[h]) for h in heads]
    kw = [kf[h] * jnp.exp(r_col[h] - r_max[h]) for h in heads]
    ct_loc = [_dot_tn(kw[h].astype(BF16), vb[h]) for h in heads]
    for h in heads:
        ct_s[h] = s_old[h] * ct[h] + s_new[h] * ct_loc[h]
        n_s[h] = s_old[h] * n_prev[h] + s_new[h] * jnp.sum(kw[h], axis=0, keepdims=True)
        m_s[h] = jnp.broadcast_to(m_new[h], m_s.shape[1:])

    for h in heads:
        if final:
            hs = hf_ref[h] + hout[h]
            mu = jnp.mean(hs, axis=-1, keepdims=True)
            var = jnp.mean(jnp.square(hs - mu), axis=-1, keepdims=True)
            hn = (hs - mu) * lax.rsqrt(var + HEAD_NORM_EPS) * g_ref[h]
            out_ref[h] = (_sigmoid(o_ref[h]) * hn).astype(out_ref.dtype)
        else:
            out_ref[h] = hout[h]


def mlstm_direction(proj, gates, B, S, reverse, hf=None, norm_g=None):
    pc, cc, pr, cr = gates
    L = MLSTM_CHUNK
    nc = S // L
    T = B * S
    H = MLSTM_HEADS
    final = hf is not None

    def tok(b, c):
        return b * nc + (nc - 1 - c if reverse else c)

    def gspec(goff):
        return pl.BlockSpec((H, L, LANES), lambda b, c: (goff // H, tok(b, c), 0))

    in_specs = [gspec(G_MQ), gspec(G_MK), gspec(G_MV),
                pl.BlockSpec((L, LANES), lambda b, c: (tok(b, c), 0)),
                pl.BlockSpec((L, LANES), lambda b, c: (tok(b, c), 0)),
                pl.BlockSpec((LANES, L), lambda b, c: (0, tok(b, c))),
                pl.BlockSpec((LANES, L), lambda b, c: (0, tok(b, c)))]
    args = [proj, proj, proj, pc, cc, pr, cr]
    if final:
        in_specs += [gspec(0), gspec(G_MO), pl.BlockSpec((H, 1, LANES), lambda b, c: (0, 0, 0))]
        args += [hf, proj, norm_g]
    return pl.pallas_call(
        functools.partial(_mlstm_kernel, reverse=reverse, final=final),
        grid=(B, nc),
        in_specs=in_specs,
        out_specs=gspec(0),
        out_shape=jax.ShapeDtypeStruct((H, T, LANES), BF16 if final else F32),
        scratch_shapes=[pltpu.VMEM((H, HEAD_DIM, HEAD_DIM), F32), pltpu.VMEM((H, 1, HEAD_DIM), F32),
                        pltpu.VMEM((H, 1, LANES), F32)],
        name="mlstm_bwd" if reverse else "mlstm_fwd",
        compiler_params=_cparams(("arbitrary", "arbitrary")),
    )(*args)


def _gla_kernel(*refs, reverse, final):
    if final:
        (q_ref, k_ref, v_ref, pc_ref, wa_ref, ab_ref, of_ref, r_ref, g_ref, out_ref, st_s) = refs
    else:
        q_ref, k_ref, v_ref, pc_ref, wa_ref, ab_ref, out_ref, st_s = refs
    c = pl.program_id(1)
    NG, TS = q_ref.shape[0], q_ref.shape[1]
    L = GLA_CHUNK
    n_sub = TS // L

    @pl.when(c == 0)
    def _():
        st_s[...] = jnp.zeros(st_s.shape, F32)

    row = lax.broadcasted_iota(jnp.int32, (TS, TS), 0)
    col = lax.broadcasted_iota(jnp.int32, (TS, TS), 1)
    same_chunk = (row // L) == (col // L)
    causal = same_chunk & ((col >= row) if reverse else (col <= row))
    cum_mat = causal.astype(BF16)
    lane = lax.broadcasted_iota(jnp.int32, (TS, LANES), 1)
    head_mask = [lane < GLA_HEAD_K, lane >= GLA_HEAD_K]
    pcb = pc_ref[...].astype(BF16)
    order = list(range(n_sub - 1, -1, -1)) if reverse else list(range(n_sub))
    pairs = range(NG)
    heads = range(2 * NG)


    la = [_log_sigmoid(_dot(pcb, wa_ref[gi]) + ab_ref[gi]) * (1.0 / GLA_TAU) for gi in pairs]
    split = [_split3(x) for x in la]
    b = [_dot(cum_mat, hi) + _dot(cum_mat, mid) + _dot(cum_mat, lo) for hi, mid, lo in split]

    def per_chunk_rows(x, idx):
        return jnp.concatenate([jnp.broadcast_to(x[s * L + idx:s * L + idx + 1, :], (L, LANES))
                                for s in range(n_sub)], axis=0)

    g_full = [per_chunk_rows(x, 0 if reverse else L - 1) for x in b]
    b_mid = [per_chunk_rows(x, L // 2 - 1 if reverse else L // 2) for x in b]
    qf = [q_ref[gi] * (GLA_HEAD_K ** -0.5) for gi in pairs]
    kf = [k_ref[gi] for gi in pairs]
    qd = [qf[gi] * jnp.exp(b[gi] - b_mid[gi]) for gi in pairs]
    kd = [(kf[gi] * jnp.exp(b_mid[gi] - b[gi])).astype(BF16) for gi in pairs]
    kg = [kf[gi] * jnp.exp(g_full[gi] - b[gi]) for gi in pairs]
    qe = [qf[gi] * jnp.exp(b[gi]) for gi in pairs]
    decay = [[jnp.exp(b[gi][s * L:s * L + 1, :] if reverse else b[gi][s * L + L - 1:s * L + L, :])
              for s in range(n_sub)] for gi in pairs]

    vb = [v_ref[hd].astype(BF16) for hd in heads]
    qd_h = [jnp.where(head_mask[hd % 2], qd[hd // 2], 0.0).astype(BF16) for hd in heads]
    kg_h = [jnp.where(head_mask[hd % 2], kg[hd // 2], 0.0).astype(BF16) for hd in heads]
    qe_h = [jnp.where(head_mask[hd % 2], qe[hd // 2], 0.0).astype(BF16) for hd in heads]
    a = [_dot_nt(qd_h[hd], kd[hd // 2]) for hd in heads]
    a = [jnp.where(causal, x, 0.0).astype(BF16) for x in a]
    o_intra = [_dot(a[hd], vb[hd]) for hd in heads]

    st = [st_s[hd] for hd in heads]
    o_inter = [[None] * n_sub for _ in heads]
    for s in order:
        sl = slice(s * L, (s + 1) * L)
        for hd in heads:
            o_inter[hd][s] = _dot_nt(qe_h[hd][sl], st[hd].astype(BF16))
        st_loc = [_dot_tn(vb[hd][sl], kg_h[hd][sl]) for hd in heads]
        st = [st[hd] * decay[hd // 2][s] + st_loc[hd] for hd in heads]
    for hd in heads:
        st_s[hd] = st[hd]

    for hd in heads:
        o = o_intra[hd] + jnp.concatenate(o_inter[hd], axis=0)
        if final:
            ot = of_ref[hd] + o
            on = ot * lax.rsqrt(jnp.mean(jnp.square(ot), axis=-1, keepdims=True) + HEAD_NORM_EPS) * g_ref[hd]
            rr = r_ref[hd]
            out_ref[hd] = (on * (rr * _sigmoid(rr))).astype(out_ref.dtype)
        else:
            out_ref[hd] = o


def gla_direction(proj, pc, wa, a_bias, B, S, reverse, of=None, norm_g=None):
    TS = GLA_STEP
    nc = S // TS
    T = B * S
    NG = GLA_HEADS // 2
    final = of is not None
    d = 1 if reverse else 0

    def tok(b, c):
        return b * nc + (nc - 1 - c if reverse else c)

    def gspec(goff, n):
        return pl.BlockSpec((n, TS, LANES), lambda b, c: (goff // n, tok(b, c), 0))

    in_specs = [gspec(G_GQ, NG), gspec(G_GK, NG), gspec(G_GV, GLA_HEADS),
                pl.BlockSpec((TS, LANES), lambda b, c: (tok(b, c), 0)),
                pl.BlockSpec((None, NG, LANES, LANES), lambda b, c: (d, 0, 0, 0)),
                pl.BlockSpec((None, NG, 1, LANES), lambda b, c: (d, 0, 0, 0))]
    args = [proj, proj, proj, pc, wa, a_bias]
    if final:
        in_specs += [gspec(0, GLA_HEADS), gspec(G_GR, GLA_HEADS),
                     pl.BlockSpec((GLA_HEADS, 1, LANES), lambda b, c: (0, 0, 0))]
        args += [of, proj, norm_g]
    return pl.pallas_call(
        functools.partial(_gla_kernel, reverse=reverse, final=final),
        grid=(B, nc),
        in_specs=in_specs,
        out_specs=gspec(0, GLA_HEADS),
        out_shape=jax.ShapeDtypeStruct((GLA_HEADS, T, LANES), BF16 if final else F32),
        scratch_shapes=[pltpu.VMEM((GLA_HEADS, HEAD_DIM, LANES), F32)],
        name="gla_bwd" if reverse else "gla_fwd",
        compiler_params=_cparams(("arbitrary", "arbitrary")),
    )(*args)


def _rope(t, cos2, sin2):
    return t * cos2 + pltpu.roll(t, shift=HEAD_DIM // 2, axis=1) * sin2


def _attn_kernel(q_ref, kp_ref, km_ref, kn_ref, vp_ref, vm_ref, vn_ref,
                 cp_ref, cm_ref, cn_ref, sp_ref, sm_ref, sn_ref, out_ref,
                 q_s, k_s, v_s, q4_s, k4_s, v4_s, st0_s, st4_s, o_s, *, seq_len):
    i = pl.program_id(1)
    TQ, HALO, QB, KB = ATTN_TILE, ATTN_HALO, ATTN_QB, ATTN_KB
    R = ATTN_DILATIONS[1]
    t0 = i * TQ

    q_s[...] = _rope(q_ref[...], cm_ref[...], sm_ref[...]) * (HEAD_DIM ** -0.5)
    k_s[0:HALO, :] = _rope(kp_ref[...], cp_ref[...], sp_ref[...])
    k_s[HALO:HALO + TQ, :] = _rope(km_ref[...], cm_ref[...], sm_ref[...])
    k_s[HALO + TQ:, :] = _rope(kn_ref[...], cn_ref[...], sn_ref[...])
    v_s[0:HALO, :] = vp_ref[...]
    v_s[HALO:HALO + TQ, :] = vm_ref[...]
    v_s[HALO + TQ:, :] = vn_ref[...]
    for j in range(R):
        q4_s[j] = q_s[pl.ds(j, TQ // R, stride=R), :]
        k4_s[j] = k_s[pl.ds(j, (TQ + 2 * HALO) // R, stride=R), :]
        v4_s[j] = v_s[pl.ds(j, (TQ + 2 * HALO) // R, stride=R), :]

    qi = lax.broadcasted_iota(jnp.int32, (QB, KB), 0)
    ki = lax.broadcasted_iota(jnp.int32, (QB, KB), 1)
    band = (ki >= qi) & (ki <= qi + 2 * ATTN_REACH)
    kcol = lax.broadcasted_iota(jnp.int32, (1, KB), 1)

    def block_group(blocks):
        n = range(len(blocks))
        qb = [blocks[u][0]().astype(BF16) for u in n]
        kb = [blocks[u][1]().astype(BF16) for u in n]
        vb = [blocks[u][2]().astype(BF16) for u in n]
        s = [_dot_nt(qb[u], kb[u]) for u in n]
        kpos = [blocks[u][3] for u in n]
        s = [jnp.where(band & (kpos[u] >= 0) & (kpos[u] < seq_len), s[u], -jnp.inf) for u in n]
        m = [jnp.max(s[u], axis=-1, keepdims=True) for u in n]
        e = [jnp.exp(s[u] - m[u]) for u in n]
        acc = [_dot(e[u].astype(BF16), vb[u]) for u in n]
        for u in n:
            store = blocks[u][4]
            store(0, acc[u])
            store(1, jnp.broadcast_to(m[u], (QB, LANES)))
            store(2, jnp.broadcast_to(jnp.sum(e[u], axis=-1, keepdims=True), (QB, LANES)))

    def dense_block(it):
        q0 = it * QB
        k0 = HALO + q0 - ATTN_REACH

        def store(which, val):
            st0_s[which, pl.ds(q0, QB), :] = val
        return (lambda: q_s[pl.ds(q0, QB), :], lambda: k_s[pl.ds(k0, KB), :], lambda: v_s[pl.ds(k0, KB), :],
                t0 - HALO + k0 + kcol, store)

    def mod4_block(it):
        j = it % R
        a0 = (it // R) * QB
        k0 = HALO // R + a0 - ATTN_REACH

        def store(which, val):
            st4_s[0, which, j, pl.ds(a0, QB), :] = val
        return (lambda: q4_s[j, pl.ds(a0, QB), :], lambda: k4_s[j, pl.ds(k0, KB), :],
                lambda: v4_s[j, pl.ds(k0, KB), :], t0 - HALO + R * (k0 + kcol) + j, store)

    def mod16_block(r):
        j = r % R
        m0 = r // R

        def store(which, val):
            st4_s[1, which, j, pl.ds(m0, QB, stride=R), :] = val
        return (lambda: q4_s[j, pl.ds(m0, QB, stride=R), :], lambda: k4_s[j, pl.ds(m0, KB, stride=R), :],
                lambda: v4_s[j, pl.ds(m0, KB, stride=R), :], t0 - HALO + r + ATTN_DILATIONS[2] * kcol, store)

    for make_block in (dense_block, mod4_block, mod16_block):
        def body(g, carry, make_block=make_block):
            block_group([make_block(g * ATTN_UNROLL + u) for u in range(ATTN_UNROLL)])
            return carry

        lax.fori_loop(0, (TQ // QB) // ATTN_UNROLL, body, 0)

    for j in range(R):
        rows = pl.ds(j, TQ // R, stride=R)
        parts = [(st0_s[0, rows, :], st0_s[1, rows, :], st0_s[2, rows, :]),
                 (st4_s[0, 0, j], st4_s[0, 1, j], st4_s[0, 2, j]),
                 (st4_s[1, 0, j], st4_s[1, 1, j], st4_s[1, 2, j])]
        m_all = jnp.maximum(jnp.maximum(parts[0][1], parts[1][1]), parts[2][1])
        num = jnp.zeros((TQ // R, LANES), F32)
        den = jnp.zeros((TQ // R, LANES), F32)
        for acc, m, l in parts:
            w = jnp.exp(m - m_all)
            num = num + w * acc
            den = den + w * l
        o_s[rows, :] = num / den
    out_ref[...] = o_s[...].astype(out_ref.dtype)


def dilated_attention(proj, cos2, sin2, B, S):
    TQ, HALO = ATTN_TILE, ATTN_HALO
    nt = S // TQ
    T = B * S
    hpt = TQ // HALO
    n_halo = T // HALO
    KT = TQ + 2 * HALO
    R = ATTN_DILATIONS[1]

    def main(goff):
        return pl.BlockSpec((None, TQ, LANES), lambda b, i, h: (goff + h, b * nt + i, 0))

    def prev(goff):
        return pl.BlockSpec((None, HALO, LANES),
                            lambda b, i, h: (goff + h, jnp.maximum((b * nt + i) * hpt - 1, 0), 0))

    def nxt(goff):
        return pl.BlockSpec((None, HALO, LANES),
                            lambda b, i, h: (goff + h, jnp.minimum((b * nt + i + 1) * hpt, n_halo - 1), 0))

    t_prev = pl.BlockSpec((HALO, LANES), lambda b, i, h: (jnp.maximum(i * hpt - 1, 0), 0))
    t_main = pl.BlockSpec((TQ, LANES), lambda b, i, h: (i, 0))
    t_next = pl.BlockSpec((HALO, LANES), lambda b, i, h: (jnp.minimum((i + 1) * hpt, S // HALO - 1), 0))

    return pl.pallas_call(
        functools.partial(_attn_kernel, seq_len=S),
        grid=(B, nt, ATTN_HEADS),
        in_specs=[main(G_AQ), prev(G_AK), main(G_AK), nxt(G_AK), prev(G_AV), main(G_AV), nxt(G_AV),
                  t_prev, t_main, t_next, t_prev, t_main, t_next],
        out_specs=main(0),
        out_shape=jax.ShapeDtypeStruct((ATTN_HEADS, T, LANES), BF16),
        scratch_shapes=[pltpu.VMEM((TQ, LANES), F32), pltpu.VMEM((KT, LANES), F32), pltpu.VMEM((KT, LANES), F32),
                        pltpu.VMEM((R, TQ // R, LANES), F32), pltpu.VMEM((R, KT // R, LANES), F32),
                        pltpu.VMEM((R, KT // R, LANES), F32), pltpu.VMEM((3, TQ, LANES), F32),
                        pltpu.VMEM((2, 3, R, TQ // R, LANES), F32), pltpu.VMEM((TQ, LANES), F32)],
        name="dilated_attention",
        compiler_params=_cparams(("parallel", "parallel", "arbitrary")),
    )(proj, proj, proj, proj, proj, proj, proj, cos2, cos2, cos2, sin2, sin2, sin2)


FFN_HALO = 16


def _up_conv_gate_kernel(xp_ref, xm_ref, xn_ref, wg_ref, wv_ref, cwg_ref, cwv_ref, cbg_ref, cbv_ref, o_ref,
                         xs_ref, *, n_tiles):
    i = pl.program_id(1)
    j = pl.program_id(2)
    tm = xm_ref.shape[0]
    hr = xp_ref.shape[0]

    @pl.when(j == 0)
    def _():
        xs_ref[0:hr, :] = xp_ref[...] * (i > 0).astype(BF16)
        xs_ref[hr:hr + tm, :] = xm_ref[...]
        xs_ref[hr + tm:, :] = xn_ref[...] * (i < n_tiles - 1).astype(BF16)

    xs = xs_ref[...]

    def conv(w_ref, cw_ref, cb_ref):
        u = _dot(xs, w_ref[...])
        rows = u.shape[0]
        u_prev = pltpu.roll(u, shift=1, axis=0)[hr:hr + tm]
        u_next = pltpu.roll(u, shift=rows - 1, axis=0)[hr:hr + tm]
        return cb_ref[...] + u_prev * cw_ref[0:1, :] + u[hr:hr + tm] * cw_ref[1:2, :] + u_next * cw_ref[2:3, :]

    gate = conv(wg_ref, cwg_ref, cbg_ref)
    val = conv(wv_ref, cwv_ref, cbv_ref)
    o_ref[...] = (gate * _sigmoid(gate) * val).astype(o_ref.dtype)


def up_conv_gate(xb, w_up, layer, conv_w, conv_b, B, S, tm=1024, tf=256):
    T, K = xb.shape
    nt = S // tm
    nf = D_FF // tf
    hb = tm // FFN_HALO
    n_hb = T // FFN_HALO

    def wspec(off, r):
        return pl.BlockSpec((r, tf), lambda b, i, j: (0, off + j))

    def upspec(off):
        return pl.BlockSpec((None, K, tf), lambda b, i, j: (layer, 0, off + j))

    return pl.pallas_call(
        functools.partial(_up_conv_gate_kernel, n_tiles=nt),
        grid=(B, nt, nf),
        in_specs=[pl.BlockSpec((FFN_HALO, K), lambda b, i, j: (jnp.maximum((b * nt + i) * hb - 1, 0), 0)),
                  pl.BlockSpec((tm, K), lambda b, i, j: (b * nt + i, 0)),
                  pl.BlockSpec((FFN_HALO, K), lambda b, i, j: (jnp.minimum((b * nt + i + 1) * hb, n_hb - 1), 0)),
                  upspec(0), upspec(nf), wspec(0, 3), wspec(nf, 3), wspec(0, 1), wspec(nf, 1)],
        out_specs=pl.BlockSpec((tm, tf), lambda b, i, j: (b * nt + i, j)),
        out_shape=jax.ShapeDtypeStruct((T, D_FF), BF16),
        scratch_shapes=[pltpu.VMEM((tm + 2 * FFN_HALO, K), BF16)],
        name="up_conv_gate",
        compiler_params=_cparams(("parallel", "parallel", "arbitrary")),
    )(xb, xb, xb, w_up, w_up, conv_w, conv_w, conv_b, conv_b)


def _ln_ple_kernel(z_ref, g_ref, b_ref, wg_ref, p_ref, wp_ref, o_ref, ob_ref, x_s, xb_s):
    j = pl.program_id(1)
    tn = wg_ref.shape[1]

    @pl.when(j == 0)
    def _():
        z = z_ref[...]
        mu = jnp.mean(z, axis=-1, keepdims=True)
        zc = z - mu
        var = jnp.mean(jnp.square(zc), axis=-1, keepdims=True)
        x = zc * lax.rsqrt(var + LN_EPS) * g_ref[...] + b_ref[...]
        xb_s[...] = x.astype(BF16)
        for jj in range(x_s.shape[0]):
            x_s[jj] = x[:, jj * tn:(jj + 1) * tn]

    gate = _sigmoid(_dot(xb_s[...], wg_ref[...]))
    emb = _dot(p_ref[...].astype(BF16), wp_ref[...])
    out = x_s[j] + gate * emb
    o_ref[...] = out
    ob_ref[...] = out.astype(BF16)


def ln_ple_update(z, g, b, p, w_gate, w_ple, layer, tm=512, tn=512):
    T, D = z.shape
    vec = pl.BlockSpec((1, D), lambda i, j: (0, 0))
    return pl.pallas_call(
        _ln_ple_kernel,
        grid=(T // tm, D // tn),
        in_specs=[pl.BlockSpec((tm, D), lambda i, j: (i, 0)), vec, vec,
                  pl.BlockSpec((None, D, tn), lambda i, j: (layer, 0, j)),
                  pl.BlockSpec((tm, PLE_DIM), lambda i, j: (i, 0)),
                  pl.BlockSpec((None, PLE_DIM, tn), lambda i, j: (layer, 0, j))],
        out_specs=[pl.BlockSpec((tm, tn), lambda i, j: (i, j)), pl.BlockSpec((tm, tn), lambda i, j: (i, j))],
        out_shape=[jax.ShapeDtypeStruct((T, D), F32), jax.ShapeDtypeStruct((T, D), BF16)],
        scratch_shapes=[pltpu.VMEM((D // tn, tm, tn), F32), pltpu.VMEM((tm, D), BF16)],
        name="ln_ple_update",
        compiler_params=_cparams(("parallel", "arbitrary")),
    )(z, g, b, w_gate, p, w_ple)


IN_PROJ_ROW_UNIT = 4 * MLSTM_HEADS


def _in_proj_segments():
    mw = MLSTM_HEADS * HEAD_DIM
    aw = ATTN_HEADS * HEAD_DIM
    gkw = GLA_HEADS * GLA_HEAD_K
    gvw = GLA_HEADS * HEAD_DIM
    o_mq, o_mk, o_mv, o_mo = 0, mw, 2 * mw, 3 * mw
    o_gate = 4 * mw
    o_aq = o_gate + 4 * MLSTM_HEADS
    o_ak, o_av = o_aq + aw, o_aq + 2 * aw
    o_gq = o_aq + 3 * aw
    o_gk = o_gq + gkw
    o_gv = o_gk + gkw
    o_gr = o_gv + gvw
    o_alpha = o_gr + gvw
    big = [(o_gv, gvw), (o_gr, gvw), (o_gq, gkw), (o_gk, gkw), (o_aq, aw), (o_mq, mw), (o_mk, mw),
           (o_mv, mw), (o_mo, mw), (o_ak, aw), (o_av, aw)]
    return big, (o_gate, 4 * MLSTM_HEADS), (o_alpha, 2 * GLA_RANK)


def _regroup_rows_kernel(tbl_ref, *refs):
    *in_refs, o_ref, small_ref = refs
    c = pl.program_id(0)
    unit = in_refs[0].shape[0]

    @pl.when(c < N_PROJ_GROUPS)
    def _():
        for u, r in enumerate(in_refs):
            o_ref[u * unit:(u + 1) * unit, :] = r[...].astype(o_ref.dtype)

    @pl.when(c == N_PROJ_GROUPS)
    def _():
        small_ref[0:unit, :] = in_refs[0][...].astype(small_ref.dtype)
        small_ref[unit:2 * unit, :] = in_refs[1][...].astype(small_ref.dtype)
        small_ref[2 * unit:, :] = jnp.zeros((small_ref.shape[0] - 2 * unit, small_ref.shape[1]), small_ref.dtype)


def regroup_input_weights(w_in_t, layer):
    K = w_in_t.shape[2]
    unit = IN_PROJ_ROW_UNIT
    per_group = LANES // unit
    big, (o_gate, n_gate), (o_alpha, n_alpha) = _in_proj_segments()
    assert n_gate == unit and n_alpha == unit and o_gate % unit == 0 and o_alpha % unit == 0
    tbl = []
    for off, width in big:
        assert off % unit == 0 and width % LANES == 0
        for g in range(width // LANES):
            tbl += [(off + g * LANES) // unit + u for u in range(per_group)]
    n_groups = len(tbl) // per_group
    assert n_groups == N_PROJ_GROUPS
    tbl += [o_gate // unit, o_alpha // unit] + [0] * (per_group - 2)
    grid_spec = pltpu.PrefetchScalarGridSpec(
        num_scalar_prefetch=1,
        grid=(n_groups + 1,),
        in_specs=[pl.BlockSpec((None, unit, K), lambda c, tbl, u=u: (layer, tbl[c * per_group + u], 0))
                  for u in range(per_group)],
        out_specs=[pl.BlockSpec((LANES, K), lambda c, tbl: (jnp.minimum(c, n_groups - 1), 0)),
                   pl.BlockSpec((LANES, K), lambda c, tbl: (0, 0))])
    return pl.pallas_call(
        _regroup_rows_kernel,
        grid_spec=grid_spec,
        out_shape=[jax.ShapeDtypeStruct((n_groups * LANES, K), BF16), jax.ShapeDtypeStruct((LANES, K), BF16)],
        name="regroup_input_weights",
        compiler_params=_cparams(("arbitrary",)),
    )(jnp.asarray(tbl, jnp.int32), *([w_in_t] * per_group))


def _prepare_layer_params(layer, w_in_t, big, mlstm_gate_b, mlstm_norm_g, gla_w_a2, gla_a_b, gla_norm_g,
                          ln1_g, ln1_b, conv_w, conv_b, ln2_g, ln2_b):
    n_gate = 4 * MLSTM_HEADS
    w_big_t, w_small_t = regroup_input_weights(w_in_t, layer)
    bias = jnp.pad(mlstm_gate_b.reshape(-1), (0, LANES - n_gate)).astype(F32)
    wa = jnp.zeros((2, GLA_HEADS // 2, LANES, LANES), F32)
    a2 = gla_w_a2.reshape(2, GLA_RANK, GLA_HEADS // 2, LANES).transpose(0, 2, 1, 3)
    wa = wa.at[0, :, GATE_A_F:GATE_A_F + GLA_RANK, :].set(a2[0])
    wa = wa.at[1, :, GATE_A_B:GATE_A_B + GLA_RANK, :].set(a2[1])
    return dict(
        w_big_t=w_big_t, w_small=w_small_t.T, w_small_t=w_small_t,
        bias_c=bias.reshape(1, LANES), bias_r=bias.reshape(LANES, 1),
        mlstm_norm_g=mlstm_norm_g.reshape(MLSTM_HEADS, 1, HEAD_DIM),
        wa=wa.astype(BF16), a_bias=gla_a_b.reshape(2, GLA_HEADS // 2, 1, LANES),
        gla_norm_g=gla_norm_g.reshape(GLA_HEADS, 1, HEAD_DIM),
        ln1_g=ln1_g.reshape(1, -1), ln1_b=ln1_b.reshape(1, -1), conv_w=conv_w, conv_b=conv_b.reshape(1, -1),
        ln2_g=ln2_g.reshape(1, -1), ln2_b=ln2_b.reshape(1, -1), layer=layer, **big)


def _rope_tables(S):
    half = HEAD_DIM // 2
    inv = ROPE_THETA ** (-jnp.arange(half, dtype=F32) / half)
    ang = jnp.arange(S, dtype=F32)[:, None] * inv[None, :]
    cos, sin = jnp.cos(ang), jnp.sin(ang)
    return jnp.concatenate([cos, cos], axis=-1), jnp.concatenate([-sin, sin], axis=-1)


def _encoder_layer(x, xb, p_i, prm, tables, B, S):
    proj = proj_matmul(xb, prm['w_big_t'])
    gates = gates_project(xb, prm['w_small'], prm['w_small_t'], prm['bias_c'], prm['bias_r'], MLSTM_CHUNK)
    hf = mlstm_direction(proj, gates, B, S, reverse=False)
    y_m = mlstm_direction(proj, gates, B, S, reverse=True, hf=hf, norm_g=prm['mlstm_norm_g'])
    y_a = dilated_attention(proj, tables[0], tables[1], B, S)
    of = gla_direction(proj, gates[0], prm['wa'], prm['a_bias'], B, S, reverse=False)
    y_g = gla_direction(proj, gates[0], prm['wa'], prm['a_bias'], B, S, reverse=True, of=of,
                        norm_g=prm['gla_norm_g'])
    layer = prm['layer']
    x, xb = mix_out_layer_norm(y_m, y_a, y_g, prm['w_out'], layer, x, prm['ln1_g'], prm['ln1_b'])
    hmid = up_conv_gate(xb, prm['w_up'], layer, prm['conv_w'], prm['conv_b'], B, S)
    z = matmul_residual(hmid, prm['w_down'], layer, x, tm=512, tn=512)
    return ln_ple_update(z, prm['ln2_g'], prm['ln2_b'], p_i, prm['w_ple_gate'], prm['w_ple'], layer)


def _trunk(x, p, layer_params):
    B, S, D = x.shape
    tables = _rope_tables(S)
    x = x.reshape(B * S, D)
    xb = x.astype(BF16)
    for i, prm in enumerate(layer_params):
        x, xb = _encoder_layer(x, xb, p[i].reshape(B * S, -1), prm, tables, B, S)
    return x.reshape(B, S, D)


def kernel(x_prompt, x_sample, p_prompt, p_sample, w_in, mlstm_gate_b, mlstm_norm_g, gla_w_a2, gla_a_b,
           gla_norm_g, w_out, ln1_g, ln1_b, w_up, conv_w, conv_b, w_down, ln2_g, ln2_b, w_ple, w_ple_gate):
    big = dict(w_out=w_out.astype(BF16), w_up=w_up.astype(BF16), w_down=w_down.astype(BF16),
               w_ple=w_ple.astype(BF16), w_ple_gate=w_ple_gate.astype(BF16))
    small = (mlstm_gate_b, mlstm_norm_g, gla_w_a2, gla_a_b, gla_norm_g, ln1_g, ln1_b, conv_w, conv_b, ln2_g, ln2_b)
    w_in_t = jnp.swapaxes(w_in, 1, 2)
    layer_params = [_prepare_layer_params(i, w_in_t, big, *(w[i] for w in small)) for i in range(w_in.shape[0])]
    return (_trunk(x_prompt, p_prompt, layer_params), _trunk(x_sample, p_sample, layer_params))
```

```python
import functools

import jax
import jax.numpy as jnp
from jax import lax
from jax.experimental import pallas as pl
from jax.experimental.pallas import tpu as pltpu

F32 = jnp.float32
BF16 = jnp.bfloat16

D_MODEL = 4096
HEAD_DIM = 128
MLSTM_HEADS = 8
ATTN_HEADS = 12
GLA_HEADS = 12
GLA_HEAD_K = 64
GLA_RANK = 16
GLA_TAU = 16.0
GLA_CHUNK = 64
ATTN_REACH = 64
ATTN_DILATIONS = (1, 4, 16)
ROPE_THETA = 10000.0
D_FF = 11008
PLE_DIM = 256
LN_EPS = 1e-5
HEAD_NORM_EPS = 1e-6
DEPTH = 2
DEEPNORM_ALPHA = (2 * DEPTH) ** 0.25

LANES = 128
VMEM_LIMIT = 56 * 1024 * 1024

G_GV, G_GR, G_GQ, G_GK = 0, 12, 24, 30
G_AQ = 36
G_MQ, G_MK, G_MV, G_MO = 48, 56, 64, 72
G_AK, G_AV = 80, 92
N_PROJ_GROUPS = 104
GATE_I_F, GATE_F_F, GATE_I_B, GATE_F_B, GATE_A_F, GATE_A_B = 0, 8, 16, 24, 32, 48

MLSTM_CHUNK = 256
GLA_STEP = 256
ATTN_TILE = 2048
ATTN_HALO = ATTN_REACH * ATTN_DILATIONS[-1]
ATTN_QB = 128
ATTN_KB = ATTN_QB + 2 * ATTN_REACH
ATTN_UNROLL = 8


def _cparams(sem):
    return pltpu.CompilerParams(dimension_semantics=sem, vmem_limit_bytes=VMEM_LIMIT)


def _log_sigmoid(x):
    return jnp.minimum(x, 0.0) - jnp.log1p(jnp.exp(-jnp.abs(x)))


def _sigmoid(x):
    return lax.logistic(x)


def _dot(a, b):
    return jnp.dot(a, b, preferred_element_type=F32)


def _dot_nt(a, b):
    return lax.dot_general(a, b, (((1,), (1,)), ((), ())), preferred_element_type=F32)


def _dot_tn(a, b):
    return lax.dot_general(a, b, (((0,), (0,)), ((), ())), preferred_element_type=F32)


def _split3(x):
    hi = x.astype(BF16)
    r1 = x - hi.astype(F32)
    mid = r1.astype(BF16)
    lo = (r1 - mid.astype(F32)).astype(BF16)
    return hi, mid, lo


def _tri_dot(tri, x):
    hi, mid, lo = _split3(x)
    return _dot(tri, hi) + _dot(tri, mid) + _dot(tri, lo)


def _dot_tri(x, tri):
    hi, mid, lo = _split3(x)
    return _dot(hi, tri) + _dot(mid, tri) + _dot(lo, tri)


def _proj_kernel(x_ref, wt_ref, o_ref):
    acc = _dot_nt(x_ref[...], wt_ref[...])
    for g in range(o_ref.shape[0]):
        o_ref[g] = acc[:, g * LANES:(g + 1) * LANES].astype(o_ref.dtype)


def proj_matmul(xb, wt, tm=1024, tn=1024):
    T, K = xb.shape
    N = wt.shape[0]
    return pl.pallas_call(
        _proj_kernel,
        grid=(T // tm, N // tn),
        in_specs=[pl.BlockSpec((tm, K), lambda i, j: (i, 0)),
                  pl.BlockSpec((tn, K), lambda i, j: (j, 0))],
        out_specs=pl.BlockSpec((tn // LANES, tm, LANES), lambda i, j: (j, i, 0)),
        out_shape=jax.ShapeDtypeStruct((N // LANES, T, LANES), F32),
        name="proj_matmul",
        compiler_params=_cparams(("parallel", "parallel")),
    )(xb, wt)


def _mm_residual_kernel(x_ref, w_ref, r_ref, o_ref):
    o_ref[...] = DEEPNORM_ALPHA * r_ref[...] + _dot(x_ref[...], w_ref[...])


def matmul_residual(xb, w, layer, res, tm, tn):
    T, K = xb.shape
    N = w.shape[2]
    return pl.pallas_call(
        _mm_residual_kernel,
        grid=(T // tm, N // tn),
        in_specs=[pl.BlockSpec((tm, K), lambda i, j: (i, 0)),
                  pl.BlockSpec((None, K, tn), lambda i, j: (layer, 0, j)),
                  pl.BlockSpec((tm, tn), lambda i, j: (i, j))],
        out_specs=pl.BlockSpec((tm, tn), lambda i, j: (i, j)),
        out_shape=jax.ShapeDtypeStruct((T, N), F32),
        name="matmul_residual",
        compiler_params=_cparams(("parallel", "parallel")),
    )(xb, w, res)


def _mix_out_ln_kernel(ym_ref, ya_ref, yg_ref, w_ref, x_ref, g_ref, b_ref, o_ref, ob_ref):
    j = pl.program_id(1)
    tn = w_ref.shape[1]
    d = o_ref.shape[1]
    n_slabs = d // tn
    parts = ([ym_ref[g] for g in range(ym_ref.shape[0])] + [ya_ref[g] for g in range(ya_ref.shape[0])]
             + [yg_ref[g] for g in range(yg_ref.shape[0])])
    y = jnp.concatenate(parts, axis=-1)
    o_ref[:, pl.ds(pl.multiple_of(j * tn, tn), tn)] = DEEPNORM_ALPHA * x_ref[...] + _dot(y, w_ref[...])

    @pl.when(j == n_slabs - 1)
    def _():
        slabs = [slice(s * tn, (s + 1) * tn) for s in range(n_slabs)]
        mu = sum(jnp.sum(o_ref[:, c], axis=-1, keepdims=True) for c in slabs) * (1.0 / d)
        var = sum(jnp.sum(jnp.square(o_ref[:, c] - mu), axis=-1, keepdims=True) for c in slabs) * (1.0 / d)
        inv = lax.rsqrt(var + LN_EPS)
        for c in slabs:
            out = (o_ref[:, c] - mu) * inv * g_ref[:, c] + b_ref[:, c]
            o_ref[:, c] = out
            ob_ref[:, c] = out.astype(BF16)


def mix_out_layer_norm(ym, ya, yg, w, layer, x, g, b, tm=512, tn=512):
    T = ym.shape[1]
    K, N = w.shape[1], w.shape[2]

    def gspec(a):
        return pl.BlockSpec((a.shape[0], tm, LANES), lambda i, j: (0, i, 0))

    vec = pl.BlockSpec((1, N), lambda i, j: (0, 0))
    row = pl.BlockSpec((tm, N), lambda i, j: (i, 0))
    return pl.pallas_call(
        _mix_out_ln_kernel,
        grid=(T // tm, N // tn),
        in_specs=[gspec(ym), gspec(ya), gspec(yg), pl.BlockSpec((None, K, tn), lambda i, j: (layer, 0, j)),
                  pl.BlockSpec((tm, tn), lambda i, j: (i, j)), vec, vec],
        out_specs=[row, row],
        out_shape=[jax.ShapeDtypeStruct((T, N), F32), jax.ShapeDtypeStruct((T, N), BF16)],
        name="mix_out_layer_norm",
        compiler_params=_cparams(("parallel", "arbitrary")),
    )(ym, ya, yg, w, x, g, b)


def _gates_kernel(x_ref, w_ref, wt_ref, bc_ref, br_ref, pc_ref, cc_ref, pr_ref, cr_ref, *, chunk):
    x = x_ref[...]
    tm = x.shape[0]
    pc = _dot(x, w_ref[...]) + bc_ref[...]
    pr = _dot_nt(wt_ref[...], x) + br_ref[...]
    pc_ref[...] = pc
    pr_ref[...] = pr
    ls_c = _log_sigmoid(pc)
    ls_r = _log_sigmoid(pr)
    row = lax.broadcasted_iota(jnp.int32, (chunk, chunk), 0)
    col = lax.broadcasted_iota(jnp.int32, (chunk, chunk), 1)
    lower = (row >= col).astype(BF16)
    upper = (row <= col).astype(BF16)
    lane = lax.broadcasted_iota(jnp.int32, (chunk, LANES), 1)
    sub = lax.broadcasted_iota(jnp.int32, (LANES, chunk), 0)
    bwd_c = (lane >= GATE_F_B) & (lane < GATE_F_B + MLSTM_HEADS)
    bwd_r = (sub >= GATE_F_B) & (sub < GATE_F_B + MLSTM_HEADS)
    for c in range(tm // chunk):
        sl = slice(c * chunk, (c + 1) * chunk)
        lc = ls_c[sl]
        cc_ref[sl, :] = jnp.where(bwd_c, _tri_dot(upper, lc), _tri_dot(lower, lc))
        lr = ls_r[:, sl]
        cr_ref[:, sl] = jnp.where(bwd_r, _dot_tri(lr, lower), _dot_tri(lr, upper))


def gates_project(xb, w_small, w_small_t, bias_c, bias_r, chunk, tm=1024):
    T, K = xb.shape
    return pl.pallas_call(
        functools.partial(_gates_kernel, chunk=chunk),
        grid=(T // tm,),
        in_specs=[pl.BlockSpec((tm, K), lambda i: (i, 0)),
                  pl.BlockSpec((K, LANES), lambda i: (0, 0)),
                  pl.BlockSpec((LANES, K), lambda i: (0, 0)),
                  pl.BlockSpec((1, LANES), lambda i: (0, 0)),
                  pl.BlockSpec((LANES, 1), lambda i: (0, 0))],
        out_specs=[pl.BlockSpec((tm, LANES), lambda i: (i, 0)),
                   pl.BlockSpec((tm, LANES), lambda i: (i, 0)),
                   pl.BlockSpec((LANES, tm), lambda i: (0, i)),
                   pl.BlockSpec((LANES, tm), lambda i: (0, i))],
        out_shape=[jax.ShapeDtypeStruct((T, LANES), F32), jax.ShapeDtypeStruct((T, LANES), F32),
                   jax.ShapeDtypeStruct((LANES, T), F32), jax.ShapeDtypeStruct((LANES, T), F32)],
        name="gates_project",
        compiler_params=_cparams(("parallel",)),
    )(xb, w_small, w_small_t, bias_c, bias_r)


def _mlstm_kernel(*refs, reverse, final):
    if final:
        (q_ref, k_ref, v_ref, pc_ref, cc_ref, pr_ref, cr_ref, hf_ref, o_ref, g_ref,
         out_ref, ct_s, n_s, m_s) = refs
    else:
        q_ref, k_ref, v_ref, pc_ref, cc_ref, pr_ref, cr_ref, out_ref, ct_s, n_s, m_s = refs
    c = pl.program_id(1)
    H, L = q_ref.shape[0], q_ref.shape[1]

    @pl.when(c == 0)
    def _():
        ct_s[...] = jnp.zeros(ct_s.shape, F32)
        n_s[...] = jnp.zeros(n_s.shape, F32)
        m_s[...] = jnp.zeros(m_s.shape, F32)

    pc = pc_ref[...]
    cc = cc_ref[...]
    lane = lax.broadcasted_iota(jnp.int32, (L, LANES), 1)
    t_idx = lax.broadcasted_iota(jnp.int32, (L, L), 0)
    s_idx = lax.broadcasted_iota(jnp.int32, (L, L), 1)
    causal = (s_idx >= t_idx) if reverse else (s_idx <= t_idx)

    heads = range(H)
    icol = [(GATE_I_B if reverse else GATE_I_F) + h for h in heads]
    fcol = [(GATE_F_B if reverse else GATE_F_F) + h for h in heads]
    qf = [q_ref[h] for h in heads]
    kf = [k_ref[h] * (HEAD_DIM ** -0.5) for h in heads]
    qb = [x.astype(BF16) for x in qf]
    kb = [x.astype(BF16) for x in kf]
    vb = [v_ref[h].astype(BF16) for h in heads]
    ct = [ct_s[h] for h in heads]
    n_prev = [n_s[h] for h in heads]
    m_prev = [m_s[h][:, 0:1] for h in heads]

    i_col = [jnp.sum(jnp.where(lane == icol[h], pc, 0.0), axis=-1, keepdims=True) for h in heads]
    b_col = [jnp.sum(jnp.where(lane == fcol[h], cc, 0.0), axis=-1, keepdims=True) for h in heads]
    r_col = [i_col[h] - b_col[h] for h in heads]
    b_row = [cr_ref[fcol[h]:fcol[h] + 1, :] for h in heads]
    r_row = [pr_ref[icol[h]:icol[h] + 1, :] - b_row[h] for h in heads]

    r_mat = [jnp.where(causal, r_row[h], -jnp.inf) for h in heads]
    m_row = [jnp.maximum(jnp.max(r_mat[h], axis=-1, keepdims=True), m_prev[h]) for h in heads]
    scores = [_dot_nt(qb[h], kb[h]) for h in heads]
    inter = [_dot(qb[h], ct[h].astype(BF16)) for h in heads]
    p = [jnp.exp(r_mat[h] - m_row[h]) * scores[h] for h in heads]
    w_inter = [jnp.exp(m_prev[h] - m_row[h]) for h in heads]
    pv = [_dot(p[h].astype(BF16), vb[h]) for h in heads]
    n_rows = [jnp.broadcast_to(n_prev[h], (8, HEAD_DIM)).astype(BF16) for h in heads]
    nq_inter = [_dot_nt(qb[h], n_rows[h])[:, 0:1] for h in heads]
    nq = [w_inter[h] * nq_inter[h] + jnp.sum(p[h], axis=-1, keepdims=True) for h in heads]
    den = [jnp.maximum(jnp.abs(nq[h]), jnp.exp(-(b_col[h] + m_row[h]))) for h in heads]
    hout = [(w_inter[h] * inter[h] + pv[h]) / den[h] for h in heads]

    g = [b_row[h][:, 0:1] if reverse else b_row[h][:, L - 1:L] for h in heads]
    r_max = [jnp.max(r_row[h], axis=-1, keepdims=True) for h in heads]
    m_loc = [g[h] + r_max[h] for h in heads]
    m_new = [jnp.maximum(g[h] + m_prev[h], m_loc[h]) for h in heads]
    s_old = [jnp.exp(g[h] + m_prev[h] - m_new[h]) for h in heads]
    s_new = [jnp.exp(m_loc[h] - m_new[h]) for h in heads]
    kw = [kf[h] * jnp.exp(r_col[h] - r_max[h]) for h in heads]
    ct_loc = [_dot_tn(kw[h].astype(BF16), vb[h]) for h in heads]
    for h in heads:
        ct_s[h] = s_old[h] * ct[h] + s_new[h] * ct_loc[h]
        n_s[h] = s_old[h] * n_prev[h] + s_new[h] * jnp.sum(kw[h], axis=0, keepdims=True)
        m_s[h] = jnp.broadcast_to(m_new[h], m_s.shape[1:])

    for h in heads:
        if final:
            hs = hf_ref[h] + hout[h]
            mu = jnp.mean(hs, axis=-1, keepdims=True)
            var = jnp.mean(jnp.square(hs - mu), axis=-1, keepdims=True)
            hn = (hs - mu) * lax.rsqrt(var + HEAD_NORM_EPS) * g_ref[h]
            out_ref[h] = (_sigmoid(o_ref[h]) * hn).astype(out_ref.dtype)
        else:
            out_ref[h] = hout[h]


def mlstm_direction(proj, gates, B, S, reverse, hf=None, norm_g=None):
    pc, cc, pr, cr = gates
    L = MLSTM_CHUNK
    nc = S // L
    T = B * S
    H = MLSTM_HEADS
    final = hf is not None

    def tok(b, c):
        return b * nc + (nc - 1 - c if reverse else c)

    def gspec(goff):
        return pl.BlockSpec((H, L, LANES), lambda b, c: (goff // H, tok(b, c), 0))

    in_specs = [gspec(G_MQ), gspec(G_MK), gspec(G_MV),
                pl.BlockSpec((L, LANES), lambda b, c: (tok(b, c), 0)),
                pl.BlockSpec((L, LANES), lambda b, c: (tok(b, c), 0)),
                pl.BlockSpec((LANES, L), lambda b, c: (0, tok(b, c))),
                pl.BlockSpec((LANES, L), lambda b, c: (0, tok(b, c)))]
    args = [proj, proj, proj, pc, cc, pr, cr]
    if final:
        in_specs += [gspec(0), gspec(G_MO), pl.BlockSpec((H, 1, LANES), lambda b, c: (0, 0, 0))]
        args += [hf, proj, norm_g]
    return pl.pallas_call(
        functools.partial(_mlstm_kernel, reverse=reverse, final=final),
        grid=(B, nc),
        in_specs=in_specs,
        out_specs=gspec(0),
        out_shape=jax.ShapeDtypeStruct((H, T, LANES), BF16 if final else F32),
        scratch_shapes=[pltpu.VMEM((H, HEAD_DIM, HEAD_DIM), F32), pltpu.VMEM((H, 1, HEAD_DIM), F32),
                        pltpu.VMEM((H, 1, LANES), F32)],
        name="mlstm_bwd" if reverse else "mlstm_fwd",
        compiler_params=_cparams(("arbitrary", "arbitrary")),
    )(*args)


def _gla_kernel(*refs, reverse, final):
    if final:
        (q_ref, k_ref, v_ref, pc_ref, wa_ref, ab_ref, of_ref, r_ref, g_ref, out_ref, st_s) = refs
    else:
        q_ref, k_ref, v_ref, pc_ref, wa_ref, ab_ref, out_ref, st_s = refs
    c = pl.program_id(1)
    NG, TS = q_ref.shape[0], q_ref.shape[1]
    L = GLA_CHUNK
    n_sub = TS // L

    @pl.when(c == 0)
    def _():
        st_s[...] = jnp.zeros(st_s.shape, F32)

    row = lax.broadcasted_iota(jnp.int32, (TS, TS), 0)
    col = lax.broadcasted_iota(jnp.int32, (TS, TS), 1)
    same_chunk = (row // L) == (col // L)
    causal = same_chunk & ((col >= row) if reverse else (col <= row))
    cum_mat = causal.astype(BF16)
    lane = lax.broadcasted_iota(jnp.int32, (TS, LANES), 1)
    head_mask = [lane < GLA_HEAD_K, lane >= GLA_HEAD_K]
    pcb = pc_ref[...].astype(BF16)
    order = list(range(n_sub - 1, -1, -1)) if reverse else list(range(n_sub))
    pairs = range(NG)
    heads = range(2 * NG)


    la = [_log_sigmoid(_dot(pcb, wa_ref[gi]) + ab_ref[gi]) * (1.0 / GLA_TAU) for gi in pairs]
    split = [_split3(x) for x in la]
    b = [_dot(cum_mat, hi) + _dot(cum_mat, mid) + _dot(cum_mat, lo) for hi, mid, lo in split]

    def per_chunk_rows(x, idx):
        return jnp.concatenate([jnp.broadcast_to(x[s * L + idx:s * L + idx + 1, :], (L, LANES))
                                for s in range(n_sub)], axis=0)

    g_full = [per_chunk_rows(x, 0 if reverse else L - 1) for x in b]
    b_mid = [per_chunk_rows(x, L // 2 - 1 if reverse else L // 2) for x in b]
    qf = [q_ref[gi] * (GLA_HEAD_K ** -0.5) for gi in pairs]
    kf = [k_ref[gi] for gi in pairs]
    qd = [qf[gi] * jnp.exp(b[gi] - b_mid[gi]) for gi in pairs]
    kd = [(kf[gi] * jnp.exp(b_mid[gi] - b[gi])).astype(BF16) for gi in pairs]
    kg = [kf[gi] * jnp.exp(g_full[gi] - b[gi]) for gi in pairs]
    qe = [qf[gi] * jnp.exp(b[gi]) for gi in pairs]
    decay = [[jnp.exp(b[gi][s * L:s * L + 1, :] if reverse else b[gi][s * L + L - 1:s * L + L, :])
              for s in range(n_sub)] for gi in pairs]

    vb = [v_ref[hd].astype(BF16) for hd in heads]
    qd_h = [jnp.where(head_mask[hd % 2], qd[hd // 2], 0.0).astype(BF16) for hd in heads]
    kg_h = [jnp.where(head_mask[hd % 2], kg[hd // 2], 0.0).astype(BF16) for hd in heads]
    qe_b = [qe[gi].astype(BF16) for gi in pairs]
    a = [_dot_nt(qd_h[hd], kd[hd // 2]) for hd in heads]
    a = [jnp.where(causal, x, 0.0).astype(BF16) for x in a]
    o_intra = [_dot(a[hd], vb[hd]) for hd in heads]

    st = [st_s[hd] for hd in heads]
    o_inter = [[None] * n_sub for _ in heads]
    for s in order:
        sl = slice(s * L, (s + 1) * L)
        for hd in heads:
            o_inter[hd][s] = _dot_nt(qe_b[hd // 2][sl], st[hd].astype(BF16))
        st_loc = [_dot_tn(vb[hd][sl], kg_h[hd][sl]) for hd in heads]
        st = [st[hd] * decay[hd // 2][s] + st_loc[hd] for hd in heads]
    for hd in heads:
        st_s[hd] = st[hd]

    for hd in heads:
        o = o_intra[hd] + jnp.concatenate(o_inter[hd], axis=0)
        if final:
            ot = of_ref[hd] + o
            on = ot * lax.rsqrt(jnp.mean(jnp.square(ot), axis=-1, keepdims=True) + HEAD_NORM_EPS) * g_ref[hd]
            rr = r_ref[hd]
            out_ref[hd] = (on * (rr * _sigmoid(rr))).astype(out_ref.dtype)
        else:
            out_ref[hd] = o


def gla_direction(proj, pc, wa, a_bias, B, S, reverse, of=None, norm_g=None):
    TS = GLA_STEP
    nc = S // TS
    T = B * S
    NG = GLA_HEADS // 2
    final = of is not None
    d = 1 if reverse else 0

    def tok(b, c):
        return b * nc + (nc - 1 - c if reverse else c)

    def gspec(goff, n):
        return pl.BlockSpec((n, TS, LANES), lambda b, c: (goff // n, tok(b, c), 0))

    in_specs = [gspec(G_GQ, NG), gspec(G_GK, NG), gspec(G_GV, GLA_HEADS),
                pl.BlockSpec((TS, LANES), lambda b, c: (tok(b, c), 0)),
                pl.BlockSpec((None, NG, LANES, LANES), lambda b, c: (d, 0, 0, 0)),
                pl.BlockSpec((None, NG, 1, LANES), lambda b, c: (d, 0, 0, 0))]
    args = [proj, proj, proj, pc, wa, a_bias]
    if final:
        in_specs += [gspec(0, GLA_HEADS), gspec(G_GR, GLA_HEADS),
                     pl.BlockSpec((GLA_HEADS, 1, LANES), lambda b, c: (0, 0, 0))]
        args += [of, proj, norm_g]
    return pl.pallas_call(
        functools.partial(_gla_kernel, reverse=reverse, final=final),
        grid=(B, nc),
        in_specs=in_specs,
        out_specs=gspec(0, GLA_HEADS),
        out_shape=jax.ShapeDtypeStruct((GLA_HEADS, T, LANES), BF16 if final else F32),
        scratch_shapes=[pltpu.VMEM((GLA_HEADS, HEAD_DIM, LANES), F32)],
        name="gla_bwd" if reverse else "gla_fwd",
        compiler_params=_cparams(("arbitrary", "arbitrary")),
    )(*args)


def _rope(t, cos2, sin2):
    return t * cos2 + pltpu.roll(t, shift=HEAD_DIM // 2, axis=1) * sin2


def _attn_kernel(q_ref, kp_ref, km_ref, kn_ref, vp_ref, vm_ref, vn_ref,
                 cp_ref, cm_ref, cn_ref, sp_ref, sm_ref, sn_ref, out_ref,
                 q_s, k_s, v_s, q4_s, k4_s, v4_s, st0_s, st4_s, o_s, *, seq_len):
    i = pl.program_id(1)
    TQ, HALO, QB, KB = ATTN_TILE, ATTN_HALO, ATTN_QB, ATTN_KB
    R = ATTN_DILATIONS[1]
    t0 = i * TQ

    q_s[...] = _rope(q_ref[...], cm_ref[...], sm_ref[...]) * (HEAD_DIM ** -0.5)
    k_s[0:HALO, :] = _rope(kp_ref[...], cp_ref[...], sp_ref[...])
    k_s[HALO:HALO + TQ, :] = _rope(km_ref[...], cm_ref[...], sm_ref[...])
    k_s[HALO + TQ:, :] = _rope(kn_ref[...], cn_ref[...], sn_ref[...])
    v_s[0:HALO, :] = vp_ref[...]
    v_s[HALO:HALO + TQ, :] = vm_ref[...]
    v_s[HALO + TQ:, :] = vn_ref[...]
    for j in range(R):
        q4_s[j] = q_s[pl.ds(j, TQ // R, stride=R), :]
        k4_s[j] = k_s[pl.ds(j, (TQ + 2 * HALO) // R, stride=R), :]
        v4_s[j] = v_s[pl.ds(j, (TQ + 2 * HALO) // R, stride=R), :]

    qi = lax.broadcasted_iota(jnp.int32, (QB, KB), 0)
    ki = lax.broadcasted_iota(jnp.int32, (QB, KB), 1)
    band = (ki >= qi) & (ki <= qi + 2 * ATTN_REACH)
    kcol = lax.broadcasted_iota(jnp.int32, (1, KB), 1)

    def block_group(blocks):
        n = range(len(blocks))
        qb = [blocks[u][0]().astype(BF16) for u in n]
        kb = [blocks[u][1]().astype(BF16) for u in n]
        vb = [blocks[u][2]().astype(BF16) for u in n]
        s = [_dot_nt(qb[u], kb[u]) for u in n]
        kpos = [blocks[u][3] for u in n]
        s = [jnp.where(band & (kpos[u] >= 0) & (kpos[u] < seq_len), s[u], -jnp.inf) for u in n]
        m = [jnp.max(s[u], axis=-1, keepdims=True) for u in n]
        e = [jnp.exp(s[u] - m[u]) for u in n]
        acc = [_dot(e[u].astype(BF16), vb[u]) for u in n]
        for u in n:
            store = blocks[u][4]
            store(0, acc[u])
            store(1, jnp.broadcast_to(m[u], (QB, LANES)))
            store(2, jnp.broadcast_to(jnp.sum(e[u], axis=-1, keepdims=True), (QB, LANES)))

    def dense_block(it):
        q0 = it * QB
        k0 = HALO + q0 - ATTN_REACH

        def store(which, val):
            st0_s[which, pl.ds(q0, QB), :] = val
        return (lambda: q_s[pl.ds(q0, QB), :], lambda: k_s[pl.ds(k0, KB), :], lambda: v_s[pl.ds(k0, KB), :],
                t0 - HALO + k0 + kcol, store)

    def mod4_block(it):
        j = it % R
        a0 = (it // R) * QB
        k0 = HALO // R + a0 - ATTN_REACH

        def store(which, val):
            st4_s[0, which, j, pl.ds(a0, QB), :] = val
        return (lambda: q4_s[j, pl.ds(a0, QB), :], lambda: k4_s[j, pl.ds(k0, KB), :],
                lambda: v4_s[j, pl.ds(k0, KB), :], t0 - HALO + R * (k0 + kcol) + j, store)

    def mod16_block(r):
        j = r % R
        m0 = r // R

        def store(which, val):
            st4_s[1, which, j, pl.ds(m0, QB, stride=R), :] = val
        return (lambda: q4_s[j, pl.ds(m0, QB, stride=R), :], lambda: k4_s[j, pl.ds(m0, KB, stride=R), :],
                lambda: v4_s[j, pl.ds(m0, KB, stride=R), :], t0 - HALO + r + ATTN_DILATIONS[2] * kcol, store)

    for make_block in (dense_block, mod4_block, mod16_block):
        def body(g, carry, make_block=make_block):
            block_group([make_block(g * ATTN_UNROLL + u) for u in range(ATTN_UNROLL)])
            return carry

        lax.fori_loop(0, (TQ // QB) // ATTN_UNROLL, body, 0)

    for j in range(R):
        rows = pl.ds(j, TQ // R, stride=R)
        parts = [(st0_s[0, rows, :], st0_s[1, rows, :], st0_s[2, rows, :]),
                 (st4_s[0, 0, j], st4_s[0, 1, j], st4_s[0, 2, j]),
                 (st4_s[1, 0, j], st4_s[1, 1, j], st4_s[1, 2, j])]
        m_all = jnp.maximum(jnp.maximum(parts[0][1], parts[1][1]), parts[2][1])
        num = jnp.zeros((TQ // R, LANES), F32)
        den = jnp.zeros((TQ // R, LANES), F32)
        for acc, m, l in parts:
            w = jnp.exp(m - m_all)
            num = num + w * acc
            den = den + w * l
        o_s[rows, :] = num / den
    out_ref[...] = o_s[...].astype(out_ref.dtype)


def dilated_attention(proj, cos2, sin2, B, S):
    TQ, HALO = ATTN_TILE, ATTN_HALO
    nt = S // TQ
    T = B * S
    hpt = TQ // HALO
    n_halo = T // HALO
    KT = TQ + 2 * HALO
    R = ATTN_DILATIONS[1]

    def main(goff):
        return pl.BlockSpec((None, TQ, LANES), lambda b, i, h: (goff + h, b * nt + i, 0))

    def prev(goff):
        return pl.BlockSpec((None, HALO, LANES),
                            lambda b, i, h: (goff + h, jnp.maximum((b * nt + i) * hpt - 1, 0), 0))

    def nxt(goff):
        return pl.BlockSpec((None, HALO, LANES),
                            lambda b, i, h: (goff + h, jnp.minimum((b * nt + i + 1) * hpt, n_halo - 1), 0))

    t_prev = pl.BlockSpec((HALO, LANES), lambda b, i, h: (jnp.maximum(i * hpt - 1, 0), 0))
    t_main = pl.BlockSpec((TQ, LANES), lambda b, i, h: (i, 0))
    t_next = pl.BlockSpec((HALO, LANES), lambda b, i, h: (jnp.minimum((i + 1) * hpt, S // HALO - 1), 0))

    return pl.pallas_call(
        functools.partial(_attn_kernel, seq_len=S),
        grid=(B, nt, ATTN_HEADS),
        in_specs=[main(G_AQ), prev(G_AK), main(G_AK), nxt(G_AK), prev(G_AV), main(G_AV), nxt(G_AV),
                  t_prev, t_main, t_next, t_prev, t_main, t_next],
        out_specs=main(0),
        out_shape=jax.ShapeDtypeStruct((ATTN_HEADS, T, LANES), BF16),
        scratch_shapes=[pltpu.VMEM((TQ, LANES), F32), pltpu.VMEM((KT, LANES), F32), pltpu.VMEM((KT, LANES), F32),
                        pltpu.VMEM((R, TQ // R, LANES), F32), pltpu.VMEM((R, KT // R, LANES), F32),
                        pltpu.VMEM((R, KT // R, LANES), F32), pltpu.VMEM((3, TQ, LANES), F32),
                        pltpu.VMEM((2, 3, R, TQ // R, LANES), F32), pltpu.VMEM((TQ, LANES), F32)],
        name="dilated_attention",
        compiler_params=_cparams(("parallel", "parallel", "arbitrary")),
    )(proj, proj, proj, proj, proj, proj, proj, cos2, cos2, cos2, sin2, sin2, sin2)


FFN_HALO = 16


def _up_conv_gate_kernel(xp_ref, xm_ref, xn_ref, wg_ref, wv_ref, cwg_ref, cwv_ref, cbg_ref, cbv_ref, o_ref,
                         xs_ref, *, n_tiles):
    i = pl.program_id(1)
    j = pl.program_id(2)
    tm = xm_ref.shape[0]
    hr = xp_ref.shape[0]

    @pl.when(j == 0)
    def _():
        xs_ref[0:hr, :] = xp_ref[...] * (i > 0).astype(BF16)
        xs_ref[hr:hr + tm, :] = xm_ref[...]
        xs_ref[hr + tm:, :] = xn_ref[...] * (i < n_tiles - 1).astype(BF16)

    xs = xs_ref[...]

    def conv(w_ref, cw_ref, cb_ref):
        u = _dot(xs, w_ref[...])
        rows = u.shape[0]
        u_prev = pltpu.roll(u, shift=1, axis=0)[hr:hr + tm]
        u_next = pltpu.roll(u, shift=rows - 1, axis=0)[hr:hr + tm]
        return cb_ref[...] + u_prev * cw_ref[0:1, :] + u[hr:hr + tm] * cw_ref[1:2, :] + u_next * cw_ref[2:3, :]

    gate = conv(wg_ref, cwg_ref, cbg_ref)
    val = conv(wv_ref, cwv_ref, cbv_ref)
    o_ref[...] = (gate * _sigmoid(gate) * val).astype(o_ref.dtype)


def up_conv_gate(xb, w_up, layer, conv_w, conv_b, B, S, tm=1024, tf=256):
    T, K = xb.shape
    nt = S // tm
    nf = D_FF // tf
    hb = tm // FFN_HALO
    n_hb = T // FFN_HALO

    def wspec(off, r):
        return pl.BlockSpec((r, tf), lambda b, i, j: (0, off + j))

    def upspec(off):
        return pl.BlockSpec((None, K, tf), lambda b, i, j: (layer, 0, off + j))

    return pl.pallas_call(
        functools.partial(_up_conv_gate_kernel, n_tiles=nt),
        grid=(B, nt, nf),
        in_specs=[pl.BlockSpec((FFN_HALO, K), lambda b, i, j: (jnp.maximum((b * nt + i) * hb - 1, 0), 0)),
                  pl.BlockSpec((tm, K), lambda b, i, j: (b * nt + i, 0)),
                  pl.BlockSpec((FFN_HALO, K), lambda b, i, j: (jnp.minimum((b * nt + i + 1) * hb, n_hb - 1), 0)),
                  upspec(0), upspec(nf), wspec(0, 3), wspec(nf, 3), wspec(0, 1), wspec(nf, 1)],
        out_specs=pl.BlockSpec((tm, tf), lambda b, i, j: (b * nt + i, j)),
        out_shape=jax.ShapeDtypeStruct((T, D_FF), BF16),
        scratch_shapes=[pltpu.VMEM((tm + 2 * FFN_HALO, K), BF16)],
        name="up_conv_gate",
        compiler_params=_cparams(("parallel", "parallel", "arbitrary")),
    )(xb, xb, xb, w_up, w_up, conv_w, conv_w, conv_b, conv_b)


def _ln_ple_kernel(z_ref, g_ref, b_ref, wg_ref, p_ref, wp_ref, o_ref, ob_ref, x_s, xb_s):
    j = pl.program_id(1)
    tn = wg_ref.shape[1]

    @pl.when(j == 0)
    def _():
        z = z_ref[...]
        mu = jnp.mean(z, axis=-1, keepdims=True)
        zc = z - mu
        var = jnp.mean(jnp.square(zc), axis=-1, keepdims=True)
        x = zc * lax.rsqrt(var + LN_EPS) * g_ref[...] + b_ref[...]
        xb_s[...] = x.astype(BF16)
        for jj in range(x_s.shape[0]):
            x_s[jj] = x[:, jj * tn:(jj + 1) * tn]

    gate = _sigmoid(_dot(xb_s[...], wg_ref[...]))
    emb = _dot(p_ref[...].astype(BF16), wp_ref[...])
    out = x_s[j] + gate * emb
    o_ref[...] = out
    ob_ref[...] = out.astype(BF16)


def ln_ple_update(z, g, b, p, w_gate, w_ple, layer, tm=512, tn=512):
    T, D = z.shape
    vec = pl.BlockSpec((1, D), lambda i, j: (0, 0))
    return pl.pallas_call(
        _ln_ple_kernel,
        grid=(T // tm, D // tn),
        in_specs=[pl.BlockSpec((tm, D), lambda i, j: (i, 0)), vec, vec,
                  pl.BlockSpec((None, D, tn), lambda i, j: (layer, 0, j)),
                  pl.BlockSpec((tm, PLE_DIM), lambda i, j: (i, 0)),
                  pl.BlockSpec((None, PLE_DIM, tn), lambda i, j: (layer, 0, j))],
        out_specs=[pl.BlockSpec((tm, tn), lambda i, j: (i, j)), pl.BlockSpec((tm, tn), lambda i, j: (i, j))],
        out_shape=[jax.ShapeDtypeStruct((T, D), F32), jax.ShapeDtypeStruct((T, D), BF16)],
        scratch_shapes=[pltpu.VMEM((D // tn, tm, tn), F32), pltpu.VMEM((tm, D), BF16)],
        name="ln_ple_update",
        compiler_params=_cparams(("parallel", "arbitrary")),
    )(z, g, b, w_gate, p, w_ple)


IN_PROJ_ROW_UNIT = 4 * MLSTM_HEADS


def _in_proj_segments():
    mw = MLSTM_HEADS * HEAD_DIM
    aw = ATTN_HEADS * HEAD_DIM
    gkw = GLA_HEADS * GLA_HEAD_K
    gvw = GLA_HEADS * HEAD_DIM
    o_mq, o_mk, o_mv, o_mo = 0, mw, 2 * mw, 3 * mw
    o_gate = 4 * mw
    o_aq = o_gate + 4 * MLSTM_HEADS
    o_ak, o_av = o_aq + aw, o_aq + 2 * aw
    o_gq = o_aq + 3 * aw
    o_gk = o_gq + gkw
    o_gv = o_gk + gkw
    o_gr = o_gv + gvw
    o_alpha = o_gr + gvw
    big = [(o_gv, gvw), (o_gr, gvw), (o_gq, gkw), (o_gk, gkw), (o_aq, aw), (o_mq, mw), (o_mk, mw),
           (o_mv, mw), (o_mo, mw), (o_ak, aw), (o_av, aw)]
    return big, (o_gate, 4 * MLSTM_HEADS), (o_alpha, 2 * GLA_RANK)


def _regroup_rows_kernel(tbl_ref, *refs):
    *in_refs, o_ref, small_ref = refs
    c = pl.program_id(0)
    unit = in_refs[0].shape[0]

    @pl.when(c < N_PROJ_GROUPS)
    def _():
        for u, r in enumerate(in_refs):
            o_ref[u * unit:(u + 1) * unit, :] = r[...].astype(o_ref.dtype)

    @pl.when(c == N_PROJ_GROUPS)
    def _():
        small_ref[0:unit, :] = in_refs[0][...].astype(small_ref.dtype)
        small_ref[unit:2 * unit, :] = in_refs[1][...].astype(small_ref.dtype)
        small_ref[2 * unit:, :] = jnp.zeros((small_ref.shape[0] - 2 * unit, small_ref.shape[1]), small_ref.dtype)


def regroup_input_weights(w_in_t, layer):
    K = w_in_t.shape[2]
    unit = IN_PROJ_ROW_UNIT
    per_group = LANES // unit
    big, (o_gate, n_gate), (o_alpha, n_alpha) = _in_proj_segments()
    assert n_gate == unit and n_alpha == unit and o_gate % unit == 0 and o_alpha % unit == 0
    tbl = []
    for off, width in big:
        assert off % unit == 0 and width % LANES == 0
        for g in range(width // LANES):
            tbl += [(off + g * LANES) // unit + u for u in range(per_group)]
    n_groups = len(tbl) // per_group
    assert n_groups == N_PROJ_GROUPS
    tbl += [o_gate // unit, o_alpha // unit] + [0] * (per_group - 2)
    grid_spec = pltpu.PrefetchScalarGridSpec(
        num_scalar_prefetch=1,
        grid=(n_groups + 1,),
        in_specs=[pl.BlockSpec((None, unit, K), lambda c, tbl, u=u: (layer, tbl[c * per_group + u], 0))
                  for u in range(per_group)],
        out_specs=[pl.BlockSpec((LANES, K), lambda c, tbl: (jnp.minimum(c, n_groups - 1), 0)),
                   pl.BlockSpec((LANES, K), lambda c, tbl: (0, 0))])
    return pl.pallas_call(
        _regroup_rows_kernel,
        grid_spec=grid_spec,
        out_shape=[jax.ShapeDtypeStruct((n_groups * LANES, K), BF16), jax.ShapeDtypeStruct((LANES, K), BF16)],
        name="regroup_input_weights",
        compiler_params=_cparams(("arbitrary",)),
    )(jnp.asarray(tbl, jnp.int32), *([w_in_t] * per_group))


def _prepare_layer_params(layer, w_in_t, big, mlstm_gate_b, mlstm_norm_g, gla_w_a2, gla_a_b, gla_norm_g,
                          ln1_g, ln1_b, conv_w, conv_b, ln2_g, ln2_b):
    n_gate = 4 * MLSTM_HEADS
    w_big_t, w_small_t = regroup_input_weights(w_in_t, layer)
    bias = jnp.pad(mlstm_gate_b.reshape(-1), (0, LANES - n_gate)).astype(F32)
    wa = jnp.zeros((2, GLA_HEADS // 2, LANES, LANES), F32)
    a2 = gla_w_a2.reshape(2, GLA_RANK, GLA_HEADS // 2, LANES).transpose(0, 2, 1, 3)
    wa = wa.at[0, :, GATE_A_F:GATE_A_F + GLA_RANK, :].set(a2[0])
    wa = wa.at[1, :, GATE_A_B:GATE_A_B + GLA_RANK, :].set(a2[1])
    return dict(
        w_big_t=w_big_t, w_small=w_small_t.T, w_small_t=w_small_t,
        bias_c=bias.reshape(1, LANES), bias_r=bias.reshape(LANES, 1),
        mlstm_norm_g=mlstm_norm_g.reshape(MLSTM_HEADS, 1, HEAD_DIM),
        wa=wa.astype(BF16), a_bias=gla_a_b.reshape(2, GLA_HEADS // 2, 1, LANES),
        gla_norm_g=gla_norm_g.reshape(GLA_HEADS, 1, HEAD_DIM),
        ln1_g=ln1_g.reshape(1, -1), ln1_b=ln1_b.reshape(1, -1), conv_w=conv_w, conv_b=conv_b.reshape(1, -1),
        ln2_g=ln2_g.reshape(1, -1), ln2_b=ln2_b.reshape(1, -1), layer=layer, **big)


def _rope_tables(S):
    half = HEAD_DIM // 2
    inv = ROPE_THETA ** (-jnp.arange(half, dtype=F32) / half)
    ang = jnp.arange(S, dtype=F32)[:, None] * inv[None, :]
    cos, sin = jnp.cos(ang), jnp.sin(ang)
    return jnp.concatenate([cos, cos], axis=-1), jnp.concatenate([-sin, sin], axis=-1)


def _encoder_layer(x, xb, p_i, prm, tables, B, S):
    proj = proj_matmul(xb, prm['w_big_t'])
    gates = gates_project(xb, prm['w_small'], prm['w_small_t'], prm['bias_c'], prm['bias_r'], MLSTM_CHUNK)
    hf = mlstm_direction(proj, gates, B, S, reverse=False)
    y_m = mlstm_direction(proj, gates, B, S, reverse=True, hf=hf, norm_g=prm['mlstm_norm_g'])
    y_a = dilated_attention(proj, tables[0], tables[1], B, S)
    of = gla_direction(proj, gates[0], prm['wa'], prm['a_bias'], B, S, reverse=False)
    y_g = gla_direction(proj, gates[0], prm['wa'], prm['a_bias'], B, S, reverse=True, of=of,
                        norm_g=prm['gla_norm_g'])
    layer = prm['layer']
    x, xb = mix_out_layer_norm(y_m, y_a, y_g, prm['w_out'], layer, x, prm['ln1_g'], prm['ln1_b'])
    hmid = up_conv_gate(xb, prm['w_up'], layer, prm['conv_w'], prm['conv_b'], B, S)
    z = matmul_residual(hmid, prm['w_down'], layer, x, tm=512, tn=512)
    return ln_ple_update(z, prm['ln2_g'], prm['ln2_b'], p_i, prm['w_ple_gate'], prm['w_ple'], layer)


def _trunk(x, p, layer_params):
    B, S, D = x.shape
    tables = _rope_tables(S)
    x = x.reshape(B * S, D)
    xb = x.astype(BF16)
    for i, prm in enumerate(layer_params):
        x, xb = _encoder_layer(x, xb, p[i].reshape(B * S, -1), prm, tables, B, S)
    return x.reshape(B, S, D)


def kernel(x_prompt, x_sample, p_prompt, p_sample, w_in, mlstm_gate_b, mlstm_norm_g, gla_w_a2, gla_a_b,
           gla_norm_g, w_out, ln1_g, ln1_b, w_up, conv_w, conv_b, w_down, ln2_g, ln2_b, w_ple, w_ple_gate):
    big = dict(w_out=w_out.astype(BF16), w_up=w_up.astype(BF16), w_down=w_down.astype(BF16),
               w_ple=w_ple.astype(BF16), w_ple_gate=w_ple_gate.astype(BF16))
    small = (mlstm_gate_b, mlstm_norm_g, gla_w_a2, gla_a_b, gla_norm_g, ln1_g, ln1_b, conv_w, conv_b, ln2_g, ln2_b)
    w_in_t = jnp.swapaxes(w_in, 1, 2)
    layer_params = [_prepare_layer_params(i, w_in_t, big, *(w[i] for w in small)) for i in range(w_in.shape[0])]
    return (_trunk(x_prompt, p_prompt, layer_params), _trunk(x_sample, p_sample, layer_params))
```

```python
import functools

import jax
import jax.numpy as jnp
from jax import lax
from jax.experimental import pallas as pl
from jax.experimental.pallas import tpu as pltpu

F32 = jnp.float32
BF16 = jnp.bfloat16

D_MODEL = 4096
HEAD_DIM = 128
MLSTM_HEADS = 8
ATTN_HEADS = 12
GLA_HEADS = 12
GLA_HEAD_K = 64
GLA_RANK = 16
GLA_TAU = 16.0
GLA_CHUNK = 64
ATTN_REACH = 64
ATTN_DILATIONS = (1, 4, 16)
ROPE_THETA = 10000.0
D_FF = 11008
PLE_DIM = 256
LN_EPS = 1e-5
HEAD_NORM_EPS = 1e-6
DEPTH = 2
DEEPNORM_ALPHA = (2 * DEPTH) ** 0.25

LANES = 128
VMEM_LIMIT = 56 * 1024 * 1024

G_GV, G_GR, G_GQ, G_GK = 0, 12, 24, 30
G_AQ = 36
G_MQ, G_MK, G_MV, G_MO = 48, 56, 64, 72
G_AK, G_AV = 80, 92
N_PROJ_GROUPS = 104
GATE_I_F, GATE_F_F, GATE_I_B, GATE_F_B, GATE_A_F, GATE_A_B = 0, 8, 16, 24, 32, 48

MLSTM_CHUNK = 256
GLA_STEP = 256
ATTN_TILE = 2048
ATTN_HALO = ATTN_REACH * ATTN_DILATIONS[-1]
ATTN_QB = 128
ATTN_KB = ATTN_QB + 2 * ATTN_REACH
ATTN_UNROLL = 8


def _cparams(sem):
    return pltpu.CompilerParams(dimension_semantics=sem, vmem_limit_bytes=VMEM_LIMIT)


def _log_sigmoid(x):
    return jnp.minimum(x, 0.0) - jnp.log1p(jnp.exp(-jnp.abs(x)))


def _sigmoid(x):
    return lax.logistic(x)


def _dot(a, b):
    return jnp.dot(a, b, preferred_element_type=F32)


def _dot_nt(a, b):
    return lax.dot_general(a, b, (((1,), (1,)), ((), ())), preferred_element_type=F32)


def _dot_tn(a, b):
    return lax.dot_general(a, b, (((0,), (0,)), ((), ())), preferred_element_type=F32)


def _split3(x):
    hi = x.astype(BF16)
    r1 = x - hi.astype(F32)
    mid = r1.astype(BF16)
    lo = (r1 - mid.astype(F32)).astype(BF16)
    return hi, mid, lo


def _tri_dot(tri, x):
    hi, mid, lo = _split3(x)
    return _dot(tri, hi) + _dot(tri, mid) + _dot(tri, lo)


def _dot_tri(x, tri):
    hi, mid, lo = _split3(x)
    return _dot(hi, tri) + _dot(mid, tri) + _dot(lo, tri)


def _proj_kernel(x_ref, wt_ref, o_ref):
    acc = _dot_nt(x_ref[...], wt_ref[...])
    for g in range(o_ref.shape[0]):
        o_ref[g] = acc[:, g * LANES:(g + 1) * LANES].astype(o_ref.dtype)


def proj_matmul(xb, wt, tm=1024, tn=1024):
    T, K = xb.shape
    N = wt.shape[0]
    return pl.pallas_call(
        _proj_kernel,
        grid=(N // tn, T // tm),
        in_specs=[pl.BlockSpec((tm, K), lambda j, i: (i, 0)),
                  pl.BlockSpec((tn, K), lambda j, i: (j, 0))],
        out_specs=pl.BlockSpec((tn // LANES, tm, LANES), lambda j, i: (j, i, 0)),
        out_shape=jax.ShapeDtypeStruct((N // LANES, T, LANES), F32),
        name="proj_matmul",
        compiler_params=_cparams(("parallel", "parallel")),
    )(xb, wt)


def _mm_residual_kernel(x_ref, w_ref, r_ref, o_ref):
    o_ref[...] = DEEPNORM_ALPHA * r_ref[...] + _dot(x_ref[...], w_ref[...])


def matmul_residual(xb, w, layer, res, tm, tn):
    T, K = xb.shape
    N = w.shape[2]
    return pl.pallas_call(
        _mm_residual_kernel,
        grid=(T // tm, N // tn),
        in_specs=[pl.BlockSpec((tm, K), lambda i, j: (i, 0)),
                  pl.BlockSpec((None, K, tn), lambda i, j: (layer, 0, j)),
                  pl.BlockSpec((tm, tn), lambda i, j: (i, j))],
        out_specs=pl.BlockSpec((tm, tn), lambda i, j: (i, j)),
        out_shape=jax.ShapeDtypeStruct((T, N), F32),
        name="matmul_residual",
        compiler_params=_cparams(("parallel", "parallel")),
    )(xb, w, res)


def _mix_out_ln_kernel(ym_ref, ya_ref, yg_ref, w_ref, x_ref, g_ref, b_ref, o_ref, ob_ref):
    j = pl.program_id(1)
    tn = w_ref.shape[1]
    d = o_ref.shape[1]
    n_slabs = d // tn
    parts = ([ym_ref[g] for g in range(ym_ref.shape[0])] + [ya_ref[g] for g in range(ya_ref.shape[0])]
             + [yg_ref[g] for g in range(yg_ref.shape[0])])
    y = jnp.concatenate(parts, axis=-1)
    o_ref[:, pl.ds(pl.multiple_of(j * tn, tn), tn)] = DEEPNORM_ALPHA * x_ref[...] + _dot(y, w_ref[...])

    @pl.when(j == n_slabs - 1)
    def _():
        slabs = [slice(s * tn, (s + 1) * tn) for s in range(n_slabs)]
        mu = sum(jnp.sum(o_ref[:, c], axis=-1, keepdims=True) for c in slabs) * (1.0 / d)
        var = sum(jnp.sum(jnp.square(o_ref[:, c] - mu), axis=-1, keepdims=True) for c in slabs) * (1.0 / d)
        inv = lax.rsqrt(var + LN_EPS)
        for c in slabs:
            out = (o_ref[:, c] - mu) * inv * g_ref[:, c] + b_ref[:, c]
            o_ref[:, c] = out
            ob_ref[:, c] = out.astype(BF16)


def mix_out_layer_norm(ym, ya, yg, w, layer, x, g, b, tm=512, tn=512):
    T = ym.shape[1]
    K, N = w.shape[1], w.shape[2]

    def gspec(a):
        return pl.BlockSpec((a.shape[0], tm, LANES), lambda i, j: (0, i, 0))

    vec = pl.BlockSpec((1, N), lambda i, j: (0, 0))
    row = pl.BlockSpec((tm, N), lambda i, j: (i, 0))
    return pl.pallas_call(
        _mix_out_ln_kernel,
        grid=(T // tm, N // tn),
        in_specs=[gspec(ym), gspec(ya), gspec(yg), pl.BlockSpec((None, K, tn), lambda i, j: (layer, 0, j)),
                  pl.BlockSpec((tm, tn), lambda i, j: (i, j)), vec, vec],
        out_specs=[row, row],
        out_shape=[jax.ShapeDtypeStruct((T, N), F32), jax.ShapeDtypeStruct((T, N), BF16)],
        name="mix_out_layer_norm",
        compiler_params=_cparams(("parallel", "arbitrary")),
    )(ym, ya, yg, w, x, g, b)


def _gates_kernel(x_ref, w_ref, wt_ref, bc_ref, br_ref, pc_ref, cc_ref, pr_ref, cr_ref, *, chunk):
    x = x_ref[...]
    tm = x.shape[0]
    pc = _dot(x, w_ref[...]) + bc_ref[...]
    pr = _dot_nt(wt_ref[...], x) + br_ref[...]
    pc_ref[...] = pc
    pr_ref[...] = pr
    ls_c = _log_sigmoid(pc)
    ls_r = _log_sigmoid(pr)
    row = lax.broadcasted_iota(jnp.int32, (chunk, chunk), 0)
    col = lax.broadcasted_iota(jnp.int32, (chunk, chunk), 1)
    lower = (row >= col).astype(BF16)
    upper = (row <= col).astype(BF16)
    lane = lax.broadcasted_iota(jnp.int32, (chunk, LANES), 1)
    sub = lax.broadcasted_iota(jnp.int32, (LANES, chunk), 0)
    bwd_c = (lane >= GATE_F_B) & (lane < GATE_F_B + MLSTM_HEADS)
    bwd_r = (sub >= GATE_F_B) & (sub < GATE_F_B + MLSTM_HEADS)
    for c in range(tm // chunk):
        sl = slice(c * chunk, (c + 1) * chunk)
        lc = ls_c[sl]
        cc_ref[sl, :] = jnp.where(bwd_c, _tri_dot(upper, lc), _tri_dot(lower, lc))
        lr = ls_r[:, sl]
        cr_ref[:, sl] = jnp.where(bwd_r, _dot_tri(lr, lower), _dot_tri(lr, upper))


def gates_project(xb, w_small, w_small_t, bias_c, bias_r, chunk, tm=1024):
    T, K = xb.shape
    return pl.pallas_call(
        functools.partial(_gates_kernel, chunk=chunk),
        grid=(T // tm,),
        in_specs=[pl.BlockSpec((tm, K), lambda i: (i, 0)),
                  pl.BlockSpec((K, LANES), lambda i: (0, 0)),
                  pl.BlockSpec((LANES, K), lambda i: (0, 0)),
                  pl.BlockSpec((1, LANES), lambda i: (0, 0)),
                  pl.BlockSpec((LANES, 1), lambda i: (0, 0))],
        out_specs=[pl.BlockSpec((tm, LANES), lambda i: (i, 0)),
                   pl.BlockSpec((tm, LANES), lambda i: (i, 0)),
                   pl.BlockSpec((LANES, tm), lambda i: (0, i)),
                   pl.BlockSpec((LANES, tm), lambda i: (0, i))],
        out_shape=[jax.ShapeDtypeStruct((T, LANES), F32), jax.ShapeDtypeStruct((T, LANES), F32),
                   jax.ShapeDtypeStruct((LANES, T), F32), jax.ShapeDtypeStruct((LANES, T), F32)],
        name="gates_project",
        compiler_params=_cparams(("parallel",)),
    )(xb, w_small, w_small_t, bias_c, bias_r)


def _mlstm_kernel(*refs, reverse, final):
    if final:
        (q_ref, k_ref, v_ref, pc_ref, cc_ref, pr_ref, cr_ref, hf_ref, o_ref, g_ref,
         out_ref, ct_s, n_s, m_s) = refs
    else:
        q_ref, k_ref, v_ref, pc_ref, cc_ref, pr_ref, cr_ref, out_ref, ct_s, n_s, m_s = refs
    c = pl.program_id(1)
    H, L = q_ref.shape[0], q_ref.shape[1]

    @pl.when(c == 0)
    def _():
        ct_s[...] = jnp.zeros(ct_s.shape, F32)
        n_s[...] = jnp.zeros(n_s.shape, F32)
        m_s[...] = jnp.zeros(m_s.shape, F32)

    pc = pc_ref[...]
    cc = cc_ref[...]
    lane = lax.broadcasted_iota(jnp.int32, (L, LANES), 1)
    t_idx = lax.broadcasted_iota(jnp.int32, (L, L), 0)
    s_idx = lax.broadcasted_iota(jnp.int32, (L, L), 1)
    causal = (s_idx >= t_idx) if reverse else (s_idx <= t_idx)

    heads = range(H)
    icol = [(GATE_I_B if reverse else GATE_I_F) + h for h in heads]
    fcol = [(GATE_F_B if reverse else GATE_F_F) + h for h in heads]
    qf = [q_ref[h] for h in heads]
    kf = [k_ref[h] * (HEAD_DIM ** -0.5) for h in heads]
    qb = [x.astype(BF16) for x in qf]
    kb = [x.astype(BF16) for x in kf]
    vb = [v_ref[h].astype(BF16) for h in heads]
    ct = [ct_s[h] for h in heads]
    n_prev = [n_s[h] for h in heads]
    m_prev = [m_s[h][:, 0:1] for h in heads]

    i_col = [jnp.sum(jnp.where(lane == icol[h], pc, 0.0), axis=-1, keepdims=True) for h in heads]
    b_col = [jnp.sum(jnp.where(lane == fcol[h], cc, 0.0), axis=-1, keepdims=True) for h in heads]
    r_col = [i_col[h] - b_col[h] for h in heads]
    b_row = [cr_ref[fcol[h]:fcol[h] + 1, :] for h in heads]
    r_row = [pr_ref[icol[h]:icol[h] + 1, :] - b_row[h] for h in heads]

    r_mat = [jnp.where(causal, r_row[h], -jnp.inf) for h in heads]
    m_row = [jnp.maximum(jnp.max(r_mat[h], axis=-1, keepdims=True), m_prev[h]) for h in heads]
    scores = [_dot_nt(qb[h], kb[h]) for h in heads]
    inter = [_dot(qb[h], ct[h].astype(BF16)) for h in heads]
    p = [jnp.exp(r_mat[h] - m_row[h]) * scores[h] for h in heads]
    w_inter = [jnp.exp(m_prev[h] - m_row[h]) for h in heads]
    pv = [_dot(p[h].astype(BF16), vb[h]) for h in heads]
    n_rows = [jnp.broadcast_to(n_prev[h], (8, HEAD_DIM)).astype(BF16) for h in heads]
    nq_inter = [_dot_nt(qb[h], n_rows[h])[:, 0:1] for h in heads]
    nq = [w_inter[h] * nq_inter[h] + jnp.sum(p[h], axis=-1, keepdims=True) for h in heads]
    den = [jnp.maximum(jnp.abs(nq[h]), jnp.exp(-(b_col[h] + m_row[h]))) for h in heads]
    hout = [(w_inter[h] * inter[h] + pv[h]) / den[h] for h in heads]

    g = [b_row[h][:, 0:1] if reverse else b_row[h][:, L - 1:L] for h in heads]
    r_max = [jnp.max(r_row[h], axis=-1, keepdims=True) for h in heads]
    m_loc = [g[h] + r_max[h] for h in heads]
    m_new = [jnp.maximum(g[h] + m_prev[h], m_loc[h]) for h in heads]
    s_old = [jnp.exp(g[h] + m_prev[h] - m_new[h]) for h in heads]
    s_new = [jnp.exp(m_loc[h] - m_new[h]) for h in heads]
    kw = [kf[h] * jnp.exp(r_col[h] - r_max[h]) for h in heads]
    ct_loc = [_dot_tn(kw[h].astype(BF16), vb[h]) for h in heads]
    for h in heads:
        ct_s[h] = s_old[h] * ct[h] + s_new[h] * ct_loc[h]
        n_s[h] = s_old[h] * n_prev[h] + s_new[h] * jnp.sum(kw[h], axis=0, keepdims=True)
        m_s[h] = jnp.broadcast_to(m_new[h], m_s.shape[1:])

    for h in heads:
        if final:
            hs = hf_ref[h] + hout[h]
            mu = jnp.mean(hs, axis=-1, keepdims=True)
            var = jnp.mean(jnp.square(hs - mu), axis=-1, keepdims=True)
            hn = (hs - mu) * lax.rsqrt(var + HEAD_NORM_EPS) * g_ref[h]
            out_ref[h] = (_sigmoid(o_ref[h]) * hn).astype(out_ref.dtype)
        else:
            out_ref[h] = hout[h]


def mlstm_direction(proj, gates, B, S, reverse, hf=None, norm_g=None):
    pc, cc, pr, cr = gates
    L = MLSTM_CHUNK
    nc = S // L
    T = B * S
    H = MLSTM_HEADS
    final = hf is not None

    def tok(b, c):
        return b * nc + (nc - 1 - c if reverse else c)

    def gspec(goff):
        return pl.BlockSpec((H, L, LANES), lambda b, c: (goff // H, tok(b, c), 0))

    in_specs = [gspec(G_MQ), gspec(G_MK), gspec(G_MV),
                pl.BlockSpec((L, LANES), lambda b, c: (tok(b, c), 0)),
                pl.BlockSpec((L, LANES), lambda b, c: (tok(b, c), 0)),
                pl.BlockSpec((LANES, L), lambda b, c: (0, tok(b, c))),
                pl.BlockSpec((LANES, L), lambda b, c: (0, tok(b, c)))]
    args = [proj, proj, proj, pc, cc, pr, cr]
    if final:
        in_specs += [gspec(0), gspec(G_MO), pl.BlockSpec((H, 1, LANES), lambda b, c: (0, 0, 0))]
        args += [hf, proj, norm_g]
    return pl.pallas_call(
        functools.partial(_mlstm_kernel, reverse=reverse, final=final),
        grid=(B, nc),
        in_specs=in_specs,
        out_specs=gspec(0),
        out_shape=jax.ShapeDtypeStruct((H, T, LANES), BF16 if final else F32),
        scratch_shapes=[pltpu.VMEM((H, HEAD_DIM, HEAD_DIM), F32), pltpu.VMEM((H, 1, HEAD_DIM), F32),
                        pltpu.VMEM((H, 1, LANES), F32)],
        name="mlstm_bwd" if reverse else "mlstm_fwd",
        compiler_params=_cparams(("arbitrary", "arbitrary")),
    )(*args)


def _gla_kernel(*refs, reverse, final):
    if final:
        (q_ref, k_ref, v_ref, pc_ref, wa_ref, ab_ref, of_ref, r_ref, g_ref, out_ref, st_s) = refs
    else:
        q_ref, k_ref, v_ref, pc_ref, wa_ref, ab_ref, out_ref, st_s = refs
    c = pl.program_id(1)
    NG, TS = q_ref.shape[0], q_ref.shape[1]
    L = GLA_CHUNK
    n_sub = TS // L

    @pl.when(c == 0)
    def _():
        st_s[...] = jnp.zeros(st_s.shape, F32)

    row = lax.broadcasted_iota(jnp.int32, (TS, TS), 0)
    col = lax.broadcasted_iota(jnp.int32, (TS, TS), 1)
    same_chunk = (row // L) == (col // L)
    causal = same_chunk & ((col >= row) if reverse else (col <= row))
    cum_mat = causal.astype(BF16)
    lane = lax.broadcasted_iota(jnp.int32, (TS, LANES), 1)
    head_mask = [lane < GLA_HEAD_K, lane >= GLA_HEAD_K]
    pcb = pc_ref[...].astype(BF16)
    order = list(range(n_sub - 1, -1, -1)) if reverse else list(range(n_sub))
    pairs = range(NG)
    heads = range(2 * NG)


    la = [_log_sigmoid(_dot(pcb, wa_ref[gi]) + ab_ref[gi]) * (1.0 / GLA_TAU) for gi in pairs]
    split = [_split3(x) for x in la]
    b = [_dot(cum_mat, hi) + _dot(cum_mat, mid) + _dot(cum_mat, lo) for hi, mid, lo in split]

    def per_chunk_rows(x, idx):
        return jnp.concatenate([jnp.broadcast_to(x[s * L + idx:s * L + idx + 1, :], (L, LANES))
                                for s in range(n_sub)], axis=0)

    g_full = [per_chunk_rows(x, 0 if reverse else L - 1) for x in b]
    b_mid = [per_chunk_rows(x, L // 2 - 1 if reverse else L // 2) for x in b]
    qf = [q_ref[gi] * (GLA_HEAD_K ** -0.5) for gi in pairs]
    kf = [k_ref[gi] for gi in pairs]
    qd = [qf[gi] * jnp.exp(b[gi] - b_mid[gi]) for gi in pairs]
    kd = [(kf[gi] * jnp.exp(b_mid[gi] - b[gi])).astype(BF16) for gi in pairs]
    kg = [kf[gi] * jnp.exp(g_full[gi] - b[gi]) for gi in pairs]
    qe = [qf[gi] * jnp.exp(b[gi]) for gi in pairs]
    decay = [[jnp.exp(b[gi][s * L:s * L + 1, :] if reverse else b[gi][s * L + L - 1:s * L + L, :])
              for s in range(n_sub)] for gi in pairs]

    vb = [v_ref[hd].astype(BF16) for hd in heads]
    qd_h = [jnp.where(head_mask[hd % 2], qd[hd // 2], 0.0).astype(BF16) for hd in heads]
    kg_h = [jnp.where(head_mask[hd % 2], kg[hd // 2], 0.0).astype(BF16) for hd in heads]
    qe_b = [qe[gi].astype(BF16) for gi in pairs]
    a = [_dot_nt(qd_h[hd], kd[hd // 2]) for hd in heads]
    a = [jnp.where(causal, x, 0.0).astype(BF16) for x in a]
    o_intra = [_dot(a[hd], vb[hd]) for hd in heads]

    st = [st_s[hd] for hd in heads]
    o_inter = [[None] * n_sub for _ in heads]
    for s in order:
        sl = slice(s * L, (s + 1) * L)
        for hd in heads:
            o_inter[hd][s] = _dot_nt(qe_b[hd // 2][sl], st[hd].astype(BF16))
        st_loc = [_dot_tn(vb[hd][sl], kg_h[hd][sl]) for hd in heads]
        st = [st[hd] * decay[hd // 2][s] + st_loc[hd] for hd in heads]
    for hd in heads:
        st_s[hd] = st[hd]

    for hd in heads:
        o = o_intra[hd] + jnp.concatenate(o_inter[hd], axis=0)
        if final:
            ot = of_ref[hd] + o
            on = ot * lax.rsqrt(jnp.mean(jnp.square(ot), axis=-1, keepdims=True) + HEAD_NORM_EPS) * g_ref[hd]
            rr = r_ref[hd]
            out_ref[hd] = (on * (rr * _sigmoid(rr))).astype(out_ref.dtype)
        else:
            out_ref[hd] = o


def gla_direction(proj, pc, wa, a_bias, B, S, reverse, of=None, norm_g=None):
    TS = GLA_STEP
    nc = S // TS
    T = B * S
    NG = GLA_HEADS // 2
    final = of is not None
    d = 1 if reverse else 0

    def tok(b, c):
        return b * nc + (nc - 1 - c if reverse else c)

    def gspec(goff, n):
        return pl.BlockSpec((n, TS, LANES), lambda b, c: (goff // n, tok(b, c), 0))

    in_specs = [gspec(G_GQ, NG), gspec(G_GK, NG), gspec(G_GV, GLA_HEADS),
                pl.BlockSpec((TS, LANES), lambda b, c: (tok(b, c), 0)),
                pl.BlockSpec((None, NG, LANES, LANES), lambda b, c: (d, 0, 0, 0)),
                pl.BlockSpec((None, NG, 1, LANES), lambda b, c: (d, 0, 0, 0))]
    args = [proj, proj, proj, pc, wa, a_bias]
    if final:
        in_specs += [gspec(0, GLA_HEADS), gspec(G_GR, GLA_HEADS),
                     pl.BlockSpec((GLA_HEADS, 1, LANES), lambda b, c: (0, 0, 0))]
        args += [of, proj, norm_g]
    return pl.pallas_call(
        functools.partial(_gla_kernel, reverse=reverse, final=final),
        grid=(B, nc),
        in_specs=in_specs,
        out_specs=gspec(0, GLA_HEADS),
        out_shape=jax.ShapeDtypeStruct((GLA_HEADS, T, LANES), BF16 if final else F32),
        scratch_shapes=[pltpu.VMEM((GLA_HEADS, HEAD_DIM, LANES), F32)],
        name="gla_bwd" if reverse else "gla_fwd",
        compiler_params=_cparams(("arbitrary", "arbitrary")),
    )(*args)


def _rope(t, cos2, sin2):
    return t * cos2 + pltpu.roll(t, shift=HEAD_DIM // 2, axis=1) * sin2


def _attn_kernel(q_ref, kp_ref, km_ref, kn_ref, vp_ref, vm_ref, vn_ref,
                 cp_ref, cm_ref, cn_ref, sp_ref, sm_ref, sn_ref, out_ref,
                 q_s, k_s, v_s, q4_s, k4_s, v4_s, st0_s, st4_s, o_s, *, seq_len):
    i = pl.program_id(1)
    TQ, HALO, QB, KB = ATTN_TILE, ATTN_HALO, ATTN_QB, ATTN_KB
    R = ATTN_DILATIONS[1]
    t0 = i * TQ

    q_s[...] = _rope(q_ref[...], cm_ref[...], sm_ref[...]) * (HEAD_DIM ** -0.5)
    k_s[0:HALO, :] = _rope(kp_ref[...], cp_ref[...], sp_ref[...])
    k_s[HALO:HALO + TQ, :] = _rope(km_ref[...], cm_ref[...], sm_ref[...])
    k_s[HALO + TQ:, :] = _rope(kn_ref[...], cn_ref[...], sn_ref[...])
    v_s[0:HALO, :] = vp_ref[...]
    v_s[HALO:HALO + TQ, :] = vm_ref[...]
    v_s[HALO + TQ:, :] = vn_ref[...]
    for j in range(R):
        q4_s[j] = q_s[pl.ds(j, TQ // R, stride=R), :]
        k4_s[j] = k_s[pl.ds(j, (TQ + 2 * HALO) // R, stride=R), :]
        v4_s[j] = v_s[pl.ds(j, (TQ + 2 * HALO) // R, stride=R), :]

    qi = lax.broadcasted_iota(jnp.int32, (QB, KB), 0)
    ki = lax.broadcasted_iota(jnp.int32, (QB, KB), 1)
    band = (ki >= qi) & (ki <= qi + 2 * ATTN_REACH)
    kcol = lax.broadcasted_iota(jnp.int32, (1, KB), 1)

    def block_group(blocks):
        n = range(len(blocks))
        qb = [blocks[u][0]().astype(BF16) for u in n]
        kb = [blocks[u][1]().astype(BF16) for u in n]
        vb = [blocks[u][2]().astype(BF16) for u in n]
        s = [_dot_nt(qb[u], kb[u]) for u in n]
        kpos = [blocks[u][3] for u in n]
        s = [jnp.where(band & (kpos[u] >= 0) & (kpos[u] < seq_len), s[u], -jnp.inf) for u in n]
        m = [jnp.max(s[u], axis=-1, keepdims=True) for u in n]
        e = [jnp.exp(s[u] - m[u]) for u in n]
        acc = [_dot(e[u].astype(BF16), vb[u]) for u in n]
        for u in n:
            store = blocks[u][4]
            store(0, acc[u])
            store(1, jnp.broadcast_to(m[u], (QB, LANES)))
            store(2, jnp.broadcast_to(jnp.sum(e[u], axis=-1, keepdims=True), (QB, LANES)))

    def dense_block(it):
        q0 = it * QB
        k0 = HALO + q0 - ATTN_REACH

        def store(which, val):
            st0_s[which, pl.ds(q0, QB), :] = val
        return (lambda: q_s[pl.ds(q0, QB), :], lambda: k_s[pl.ds(k0, KB), :], lambda: v_s[pl.ds(k0, KB), :],
                t0 - HALO + k0 + kcol, store)

    def mod4_block(it):
        j = it % R
        a0 = (it // R) * QB
        k0 = HALO // R + a0 - ATTN_REACH

        def store(which, val):
            st4_s[0, which, j, pl.ds(a0, QB), :] = val
        return (lambda: q4_s[j, pl.ds(a0, QB), :], lambda: k4_s[j, pl.ds(k0, KB), :],
                lambda: v4_s[j, pl.ds(k0, KB), :], t0 - HALO + R * (k0 + kcol) + j, store)

    def mod16_block(r):
        j = r % R
        m0 = r // R

        def store(which, val):
            st4_s[1, which, j, pl.ds(m0, QB, stride=R), :] = val
        return (lambda: q4_s[j, pl.ds(m0, QB, stride=R), :], lambda: k4_s[j, pl.ds(m0, KB, stride=R), :],
                lambda: v4_s[j, pl.ds(m0, KB, stride=R), :], t0 - HALO + r + ATTN_DILATIONS[2] * kcol, store)

    for make_block in (dense_block, mod4_block, mod16_block):
        def body(g, carry, make_block=make_block):
            block_group([make_block(g * ATTN_UNROLL + u) for u in range(ATTN_UNROLL)])
            return carry

        lax.fori_loop(0, (TQ // QB) // ATTN_UNROLL, body, 0)

    for j in range(R):
        rows = pl.ds(j, TQ // R, stride=R)
        parts = [(st0_s[0, rows, :], st0_s[1, rows, :], st0_s[2, rows, :]),
                 (st4_s[0, 0, j], st4_s[0, 1, j], st4_s[0, 2, j]),
                 (st4_s[1, 0, j], st4_s[1, 1, j], st4_s[1, 2, j])]
        m_all = jnp.maximum(jnp.maximum(parts[0][1], parts[1][1]), parts[2][1])
        num = jnp.zeros((TQ // R, LANES), F32)
        den = jnp.zeros((TQ // R, LANES), F32)
        for acc, m, l in parts:
            w = jnp.exp(m - m_all)
            num = num + w * acc
            den = den + w * l
        o_s[rows, :] = num / den
    out_ref[...] = o_s[...].astype(out_ref.dtype)


def dilated_attention(proj, cos2, sin2, B, S):
    TQ, HALO = ATTN_TILE, ATTN_HALO
    nt = S // TQ
    T = B * S
    hpt = TQ // HALO
    n_halo = T // HALO
    KT = TQ + 2 * HALO
    R = ATTN_DILATIONS[1]

    def main(goff):
        return pl.BlockSpec((None, TQ, LANES), lambda b, i, h: (goff + h, b * nt + i, 0))

    def prev(goff):
        return pl.BlockSpec((None, HALO, LANES),
                            lambda b, i, h: (goff + h, jnp.maximum((b * nt + i) * hpt - 1, 0), 0))

    def nxt(goff):
        return pl.BlockSpec((None, HALO, LANES),
                            lambda b, i, h: (goff + h, jnp.minimum((b * nt + i + 1) * hpt, n_halo - 1), 0))

    t_prev = pl.BlockSpec((HALO, LANES), lambda b, i, h: (jnp.maximum(i * hpt - 1, 0), 0))
    t_main = pl.BlockSpec((TQ, LANES), lambda b, i, h: (i, 0))
    t_next = pl.BlockSpec((HALO, LANES), lambda b, i, h: (jnp.minimum((i + 1) * hpt, S // HALO - 1), 0))

    return pl.pallas_call(
        functools.partial(_attn_kernel, seq_len=S),
        grid=(B, nt, ATTN_HEADS),
        in_specs=[main(G_AQ), prev(G_AK), main(G_AK), nxt(G_AK), prev(G_AV), main(G_AV), nxt(G_AV),
                  t_prev, t_main, t_next, t_prev, t_main, t_next],
        out_specs=main(0),
        out_shape=jax.ShapeDtypeStruct((ATTN_HEADS, T, LANES), BF16),
        scratch_shapes=[pltpu.VMEM((TQ, LANES), F32), pltpu.VMEM((KT, LANES), F32), pltpu.VMEM((KT, LANES), F32),
                        pltpu.VMEM((R, TQ // R, LANES), F32), pltpu.VMEM((R, KT // R, LANES), F32),
                        pltpu.VMEM((R, KT // R, LANES), F32), pltpu.VMEM((3, TQ, LANES), F32),
                        pltpu.VMEM((2, 3, R, TQ // R, LANES), F32), pltpu.VMEM((TQ, LANES), F32)],
        name="dilated_attention",
        compiler_params=_cparams(("parallel", "parallel", "arbitrary")),
    )(proj, proj, proj, proj, proj, proj, proj, cos2, cos2, cos2, sin2, sin2, sin2)


FFN_HALO = 16


def _up_conv_gate_kernel(xp_ref, xm_ref, xn_ref, wg_ref, wv_ref, cwg_ref, cwv_ref, cbg_ref, cbv_ref, o_ref,
                         xs_ref, *, n_tiles):
    i = pl.program_id(1)
    j = pl.program_id(2)
    tm = xm_ref.shape[0]
    hr = xp_ref.shape[0]

    @pl.when(j == 0)
    def _():
        xs_ref[0:hr, :] = xp_ref[...] * (i > 0).astype(BF16)
        xs_ref[hr:hr + tm, :] = xm_ref[...]
        xs_ref[hr + tm:, :] = xn_ref[...] * (i < n_tiles - 1).astype(BF16)

    xs = xs_ref[...]

    def conv(w_ref, cw_ref, cb_ref):
        u = _dot(xs, w_ref[...])
        rows = u.shape[0]
        u_prev = pltpu.roll(u, shift=1, axis=0)[hr:hr + tm]
        u_next = pltpu.roll(u, shift=rows - 1, axis=0)[hr:hr + tm]
        return cb_ref[...] + u_prev * cw_ref[0:1, :] + u[hr:hr + tm] * cw_ref[1:2, :] + u_next * cw_ref[2:3, :]

    gate = conv(wg_ref, cwg_ref, cbg_ref)
    val = conv(wv_ref, cwv_ref, cbv_ref)
    o_ref[...] = (gate * _sigmoid(gate) * val).astype(o_ref.dtype)


def up_conv_gate(xb, w_up, layer, conv_w, conv_b, B, S, tm=1024, tf=256):
    T, K = xb.shape
    nt = S // tm
    nf = D_FF // tf
    hb = tm // FFN_HALO
    n_hb = T // FFN_HALO

    def wspec(off, r):
        return pl.BlockSpec((r, tf), lambda b, i, j: (0, off + j))

    def upspec(off):
        return pl.BlockSpec((None, K, tf), lambda b, i, j: (layer, 0, off + j))

    return pl.pallas_call(
        functools.partial(_up_conv_gate_kernel, n_tiles=nt),
        grid=(B, nt, nf),
        in_specs=[pl.BlockSpec((FFN_HALO, K), lambda b, i, j: (jnp.maximum((b * nt + i) * hb - 1, 0), 0)),
                  pl.BlockSpec((tm, K), lambda b, i, j: (b * nt + i, 0)),
                  pl.BlockSpec((FFN_HALO, K), lambda b, i, j: (jnp.minimum((b * nt + i + 1) * hb, n_hb - 1), 0)),
                  upspec(0), upspec(nf), wspec(0, 3), wspec(nf, 3), wspec(0, 1), wspec(nf, 1)],
        out_specs=pl.BlockSpec((tm, tf), lambda b, i, j: (b * nt + i, j)),
        out_shape=jax.ShapeDtypeStruct((T, D_FF), BF16),
        scratch_shapes=[pltpu.VMEM((tm + 2 * FFN_HALO, K), BF16)],
        name="up_conv_gate",
        compiler_params=_cparams(("parallel", "parallel", "arbitrary")),
    )(xb, xb, xb, w_up, w_up, conv_w, conv_w, conv_b, conv_b)


def _ln_ple_kernel(z_ref, g_ref, b_ref, wg_ref, p_ref, wp_ref, o_ref, ob_ref, x_s, xb_s):
    j = pl.program_id(1)
    tn = wg_ref.shape[1]

    @pl.when(j == 0)
    def _():
        z = z_ref[...]
        mu = jnp.mean(z, axis=-1, keepdims=True)
        zc = z - mu
        var = jnp.mean(jnp.square(zc), axis=-1, keepdims=True)
        x = zc * lax.rsqrt(var + LN_EPS) * g_ref[...] + b_ref[...]
        xb_s[...] = x.astype(BF16)
        for jj in range(x_s.shape[0]):
            x_s[jj] = x[:, jj * tn:(jj + 1) * tn]

    gate = _sigmoid(_dot(xb_s[...], wg_ref[...]))
    emb = _dot(p_ref[...].astype(BF16), wp_ref[...])
    out = x_s[j] + gate * emb
    o_ref[...] = out
    ob_ref[...] = out.astype(BF16)


def ln_ple_update(z, g, b, p, w_gate, w_ple, layer, tm=512, tn=512):
    T, D = z.shape
    vec = pl.BlockSpec((1, D), lambda i, j: (0, 0))
    return pl.pallas_call(
        _ln_ple_kernel,
        grid=(T // tm, D // tn),
        in_specs=[pl.BlockSpec((tm, D), lambda i, j: (i, 0)), vec, vec,
                  pl.BlockSpec((None, D, tn), lambda i, j: (layer, 0, j)),
                  pl.BlockSpec((tm, PLE_DIM), lambda i, j: (i, 0)),
                  pl.BlockSpec((None, PLE_DIM, tn), lambda i, j: (layer, 0, j))],
        out_specs=[pl.BlockSpec((tm, tn), lambda i, j: (i, j)), pl.BlockSpec((tm, tn), lambda i, j: (i, j))],
        out_shape=[jax.ShapeDtypeStruct((T, D), F32), jax.ShapeDtypeStruct((T, D), BF16)],
        scratch_shapes=[pltpu.VMEM((D // tn, tm, tn), F32), pltpu.VMEM((tm, D), BF16)],
        name="ln_ple_update",
        compiler_params=_cparams(("parallel", "arbitrary")),
    )(z, g, b, w_gate, p, w_ple)


IN_PROJ_ROW_UNIT = 4 * MLSTM_HEADS


def _in_proj_segments():
    mw = MLSTM_HEADS * HEAD_DIM
    aw = ATTN_HEADS * HEAD_DIM
    gkw = GLA_HEADS * GLA_HEAD_K
    gvw = GLA_HEADS * HEAD_DIM
    o_mq, o_mk, o_mv, o_mo = 0, mw, 2 * mw, 3 * mw
    o_gate = 4 * mw
    o_aq = o_gate + 4 * MLSTM_HEADS
    o_ak, o_av = o_aq + aw, o_aq + 2 * aw
    o_gq = o_aq + 3 * aw
    o_gk = o_gq + gkw
    o_gv = o_gk + gkw
    o_gr = o_gv + gvw
    o_alpha = o_gr + gvw
    big = [(o_gv, gvw), (o_gr, gvw), (o_gq, gkw), (o_gk, gkw), (o_aq, aw), (o_mq, mw), (o_mk, mw),
           (o_mv, mw), (o_mo, mw), (o_ak, aw), (o_av, aw)]
    return big, (o_gate, 4 * MLSTM_HEADS), (o_alpha, 2 * GLA_RANK)


def _regroup_rows_kernel(tbl_ref, *refs):
    *in_refs, o_ref, small_ref = refs
    c = pl.program_id(0)
    unit = in_refs[0].shape[0]

    @pl.when(c < N_PROJ_GROUPS)
    def _():
        for u, r in enumerate(in_refs):
            o_ref[u * unit:(u + 1) * unit, :] = r[...].astype(o_ref.dtype)

    @pl.when(c == N_PROJ_GROUPS)
    def _():
        small_ref[0:unit, :] = in_refs[0][...].astype(small_ref.dtype)
        small_ref[unit:2 * unit, :] = in_refs[1][...].astype(small_ref.dtype)
        small_ref[2 * unit:, :] = jnp.zeros((small_ref.shape[0] - 2 * unit, small_ref.shape[1]), small_ref.dtype)


def regroup_input_weights(w_in_t, layer):
    K = w_in_t.shape[2]
    unit = IN_PROJ_ROW_UNIT
    per_group = LANES // unit
    big, (o_gate, n_gate), (o_alpha, n_alpha) = _in_proj_segments()
    assert n_gate == unit and n_alpha == unit and o_gate % unit == 0 and o_alpha % unit == 0
    tbl = []
    for off, width in big:
        assert off % unit == 0 and width % LANES == 0
        for g in range(width // LANES):
            tbl += [(off + g * LANES) // unit + u for u in range(per_group)]
    n_groups = len(tbl) // per_group
    assert n_groups == N_PROJ_GROUPS
    tbl += [o_gate // unit, o_alpha // unit] + [0] * (per_group - 2)
    grid_spec = pltpu.PrefetchScalarGridSpec(
        num_scalar_prefetch=1,
        grid=(n_groups + 1,),
        in_specs=[pl.BlockSpec((None, unit, K), lambda c, tbl, u=u: (layer, tbl[c * per_group + u], 0))
                  for u in range(per_group)],
        out_specs=[pl.BlockSpec((LANES, K), lambda c, tbl: (jnp.minimum(c, n_groups - 1), 0)),
                   pl.BlockSpec((LANES, K), lambda c, tbl: (0, 0))])
    return pl.pallas_call(
        _regroup_rows_kernel,
        grid_spec=grid_spec,
        out_shape=[jax.ShapeDtypeStruct((n_groups * LANES, K), BF16), jax.ShapeDtypeStruct((LANES, K), BF16)],
        name="regroup_input_weights",
        compiler_params=_cparams(("arbitrary",)),
    )(jnp.asarray(tbl, jnp.int32), *([w_in_t] * per_group))


def _prepare_layer_params(layer, w_in_t, big, mlstm_gate_b, mlstm_norm_g, gla_w_a2, gla_a_b, gla_norm_g,
                          ln1_g, ln1_b, conv_w, conv_b, ln2_g, ln2_b):
    n_gate = 4 * MLSTM_HEADS
    w_big_t, w_small_t = regroup_input_weights(w_in_t, layer)
    bias = jnp.pad(mlstm_gate_b.reshape(-1), (0, LANES - n_gate)).astype(F32)
    wa = jnp.zeros((2, GLA_HEADS // 2, LANES, LANES), F32)
    a2 = gla_w_a2.reshape(2, GLA_RANK, GLA_HEADS // 2, LANES).transpose(0, 2, 1, 3)
    wa = wa.at[0, :, GATE_A_F:GATE_A_F + GLA_RANK, :].set(a2[0])
    wa = wa.at[1, :, GATE_A_B:GATE_A_B + GLA_RANK, :].set(a2[1])
    return dict(
        w_big_t=w_big_t, w_small=w_small_t.T, w_small_t=w_small_t,
        bias_c=bias.reshape(1, LANES), bias_r=bias.reshape(LANES, 1),
        mlstm_norm_g=mlstm_norm_g.reshape(MLSTM_HEADS, 1, HEAD_DIM),
        wa=wa.astype(BF16), a_bias=gla_a_b.reshape(2, GLA_HEADS // 2, 1, LANES),
        gla_norm_g=gla_norm_g.reshape(GLA_HEADS, 1, HEAD_DIM),
        ln1_g=ln1_g.reshape(1, -1), ln1_b=ln1_b.reshape(1, -1), conv_w=conv_w, conv_b=conv_b.reshape(1, -1),
        ln2_g=ln2_g.reshape(1, -1), ln2_b=ln2_b.reshape(1, -1), layer=layer, **big)


def _rope_tables(S):
    half = HEAD_DIM // 2
    inv = ROPE_THETA ** (-jnp.arange(half, dtype=F32) / half)
    ang = jnp.arange(S, dtype=F32)[:, None] * inv[None, :]
    cos, sin = jnp.cos(ang), jnp.sin(ang)
    return jnp.concatenate([cos, cos], axis=-1), jnp.concatenate([-sin, sin], axis=-1)


def _encoder_layer(x, xb, p_i, prm, tables, B, S):
    proj = proj_matmul(xb, prm['w_big_t'])
    gates = gates_project(xb, prm['w_small'], prm['w_small_t'], prm['bias_c'], prm['bias_r'], MLSTM_CHUNK)
    hf = mlstm_direction(proj, gates, B, S, reverse=False)
    y_m = mlstm_direction(proj, gates, B, S, reverse=True, hf=hf, norm_g=prm['mlstm_norm_g'])
    y_a = dilated_attention(proj, tables[0], tables[1], B, S)
    of = gla_direction(proj, gates[0], prm['wa'], prm['a_bias'], B, S, reverse=False)
    y_g = gla_direction(proj, gates[0], prm['wa'], prm['a_bias'], B, S, reverse=True, of=of,
                        norm_g=prm['gla_norm_g'])
    layer = prm['layer']
    x, xb = mix_out_layer_norm(y_m, y_a, y_g, prm['w_out'], layer, x, prm['ln1_g'], prm['ln1_b'])
    hmid = up_conv_gate(xb, prm['w_up'], layer, prm['conv_w'], prm['conv_b'], B, S)
    z = matmul_residual(hmid, prm['w_down'], layer, x, tm=512, tn=512)
    return ln_ple_update(z, prm['ln2_g'], prm['ln2_b'], p_i, prm['w_ple_gate'], prm['w_ple'], layer)


def _trunk(x, p, layer_params):
    B, S, D = x.shape
    tables = _rope_tables(S)
    x = x.reshape(B * S, D)
    xb = x.astype(BF16)
    for i, prm in enumerate(layer_params):
        x, xb = _encoder_layer(x, xb, p[i].reshape(B * S, -1), prm, tables, B, S)
    return x.reshape(B, S, D)


def kernel(x_prompt, x_sample, p_prompt, p_sample, w_in, mlstm_gate_b, mlstm_norm_g, gla_w_a2, gla_a_b,
           gla_norm_g, w_out, ln1_g, ln1_b, w_up, conv_w, conv_b, w_down, ln2_g, ln2_b, w_ple, w_ple_gate):
    big = dict(w_out=w_out.astype(BF16), w_up=w_up.astype(BF16), w_down=w_down.astype(BF16),
               w_ple=w_ple.astype(BF16), w_ple_gate=w_ple_gate.astype(BF16))
    small = (mlstm_gate_b, mlstm_norm_g, gla_w_a2, gla_a_b, gla_norm_g, ln1_g, ln1_b, conv_w, conv_b, ln2_g, ln2_b)
    w_in_t = jnp.swapaxes(w_in, 1, 2)
    layer_params = [_prepare_layer_params(i, w_in_t, big, *(w[i] for w in small)) for i in range(w_in.shape[0])]
    return (_trunk(x_prompt, p_prompt, layer_params), _trunk(x_sample, p_sample, layer_params))
```
